```python
import math
import jax, jax.numpy as jnp
from jax import lax
import numpy as np

D_MODEL = 2048
BATCH = 2
SEQ = 4096
DEPTH = 1
DEC_BATCH = 16
DEC_SEQ = 16
PAST_LEN = 2048

CHUNK = 64
ATTN_WIDTH = D_MODEL // 2
LRU_WIDTH = D_MODEL - ATTN_WIDTH
HEAD_DIM = 128
HALF_DIM = HEAD_DIM // 2
N_HEADS = ATTN_WIDTH // HEAD_DIM
LRU_BLOCKS = 16
LRU_BLOCK_DIM = LRU_WIDTH // LRU_BLOCKS
CONV_WIDTH = 4
LRU_C = 8.0
D_FF = 4 * D_MODEL
ROPE_THETA = 10000.0
Q_BLOCK = 128
LN_EPS = 1e-5
RMS_EPS = 1e-5
NEG_INF = -1e30
DEEPNORM_ALPHA = (2.0 * DEPTH) ** 0.25
DEEPNORM_BETA = (8.0 * DEPTH) ** -0.25
IN_COLS = 3 * ATTN_WIDTH + 2 * LRU_WIDTH

kernel_name = "hymba_diffattn_rglru_deepnorm_stream_step"


def lambda_init(layer):
    return 0.8 - 0.6 * math.exp(-0.3 * layer)


def layer_norm(x, g, b):
    xf = x.astype(jnp.float32)
    mu = jnp.mean(xf, -1, keepdims=True)
    var = jnp.mean(jnp.square(xf - mu), -1, keepdims=True)
    y = (xf - mu) * lax.rsqrt(var + LN_EPS) * g.astype(jnp.float32) + b.astype(jnp.float32)
    return y.astype(x.dtype)


def rope(x, pos):
    inv = ROPE_THETA ** (-jnp.arange(0, HALF_DIM, 2, dtype=jnp.float32) / HALF_DIM)
    ang = pos.astype(jnp.float32)[:, None] * inv[None, :]
    cos = jnp.concatenate([jnp.cos(ang), jnp.cos(ang)], -1)[:, None, None, :]
    sin = jnp.concatenate([jnp.sin(ang), jnp.sin(ang)], -1)[:, None, None, :]
    x1, x2 = jnp.split(x.astype(jnp.float32), 2, axis=-1)
    rot = jnp.concatenate([-x2, x1], -1)
    return (x.astype(jnp.float32) * cos + rot * sin).astype(x.dtype)


def diff_attn_prompt(q, k, v, lam):
    B, S = q.shape[0], q.shape[1]
    nblk = S // Q_BLOCK
    k_chunk = jnp.arange(S) // CHUNK
    qb = q.reshape(B, nblk, Q_BLOCK, N_HEADS, 2, HALF_DIM).transpose(1, 0, 2, 3, 4, 5)
    scale = HALF_DIM ** -0.5

    def block(args):
        q_blk, i = args
        s = jnp.einsum('bqhcd,bkhcd->bhcqk', q_blk, k).astype(jnp.float32) * scale
        q_chunk = (i * Q_BLOCK + jnp.arange(Q_BLOCK)) // CHUNK
        mask = q_chunk[:, None] >= k_chunk[None, :]
        p = jax.nn.softmax(jnp.where(mask, s, NEG_INF), axis=-1)
        w = p[:, :, 0] - lam * p[:, :, 1]
        return jnp.einsum('bhqk,bkhe->bqhe', w.astype(v.dtype), v)

    o = lax.map(block, (qb, jnp.arange(nblk)))
    return o.transpose(1, 0, 2, 3, 4).reshape(B, S, N_HEADS, HEAD_DIM)


def diff_attn_sample(q, k, v, cache_k, cache_v, lam):
    B, P = cache_k.shape[0], cache_k.shape[1]
    k_all = jnp.concatenate([cache_k.reshape(B, P, N_HEADS, 2, HALF_DIM), k], axis=1)
    v_all = jnp.concatenate([cache_v, v], axis=1)
    s = jnp.einsum('bqhcd,bkhcd->bhcqk', q, k_all).astype(jnp.float32) * HALF_DIM ** -0.5
    p = jax.nn.softmax(s, axis=-1)
    w = p[:, :, 0] - lam * p[:, :, 1]
    return jnp.einsum('bhqk,bkhe->bqhe', w.astype(v_all.dtype), v_all)


def causal_conv(xb, buf, w, b):
    T = xb.shape[1]
    xp = jnp.concatenate([buf, xb], axis=1)
    y = b + sum(xp[:, j:j + T] * w[j] for j in range(CONV_WIDTH))
    return y, xp[:, -(CONV_WIDTH - 1):]


def rg_lru(xc, h0, w_a, b_a, w_i, b_i, lru_lambda):
    B, T = xc.shape[0], xc.shape[1]
    xr = xc.reshape(B, T, LRU_BLOCKS, LRU_BLOCK_DIM)
    r = jax.nn.sigmoid(jnp.einsum('bthi,hij->bthj', xr, w_a).reshape(B, T, LRU_WIDTH) + b_a)
    ig = jax.nn.sigmoid(jnp.einsum('bthi,hij->bthj', xr, w_i).reshape(B, T, LRU_WIDTH) + b_i)
    log_a = -LRU_C * r.astype(jnp.float32) * jax.nn.softplus(-lru_lambda.astype(jnp.float32))
    a = jnp.exp(log_a)
    u = jnp.sqrt(-jnp.expm1(2.0 * log_a)) * (ig * xc).astype(jnp.float32)

    def step(h, au):
        a_t, u_t = au
        h = a_t * h + u_t
        return h, h

    h_last, hs = lax.scan(step, h0.astype(jnp.float32), (a.transpose(1, 0, 2), u.transpose(1, 0, 2)))
    return hs.transpose(1, 0, 2).astype(xc.dtype), h_last.astype(h0.dtype)


def trunk_layer(x, pos, cache_k, cache_v, conv_buf, h0, p, lam_init):
    B, T = x.shape[0], x.shape[1]
    proj = x @ p['w_in']
    q, k, v, xb, g = jnp.split(proj, [ATTN_WIDTH, 2 * ATTN_WIDTH, 3 * ATTN_WIDTH, 3 * ATTN_WIDTH + LRU_WIDTH], axis=-1)
    q = rope(q.reshape(B, T, N_HEADS, 2, HALF_DIM), pos)
    k = rope(k.reshape(B, T, N_HEADS, 2, HALF_DIM), pos)
    v = v.reshape(B, T, N_HEADS, HEAD_DIM)
    lam = (jnp.exp(jnp.sum(p['lambda_q1'].astype(jnp.float32) * p['lambda_k1'].astype(jnp.float32)))
           - jnp.exp(jnp.sum(p['lambda_q2'].astype(jnp.float32) * p['lambda_k2'].astype(jnp.float32)))
           + lam_init)
    if cache_k is None:
        o = diff_attn_prompt(q, k, v, lam)
    else:
        o = diff_attn_sample(q, k, v, cache_k, cache_v, lam)
    of = o.astype(jnp.float32)
    of = of * lax.rsqrt(jnp.mean(of * of, -1, keepdims=True) + RMS_EPS) * p['subln_g'].astype(jnp.float32)
    o_attn = (of * (1.0 - lam_init)).astype(x.dtype).reshape(B, T, ATTN_WIDTH)
    xc, new_buf = causal_conv(xb, conv_buf, p['conv_w'], p['conv_b'])
    hs, h_last = rg_lru(xc, h0, p['w_rg_a'], p['b_rg_a'], p['w_rg_i'], p['b_rg_i'], p['lru_lambda'])
    o_lru = hs * jax.nn.gelu(g)
    mix = jnp.concatenate([o_attn, o_lru], axis=-1) @ p['w_out']
    x = layer_norm(DEEPNORM_ALPHA * x + mix, p['ln1_g'], p['ln1_b'])
    ffn = jnp.square(jax.nn.relu(x @ p['w_up'])) @ p['w_down']
    x = layer_norm(DEEPNORM_ALPHA * x + ffn, p['ln2_g'], p['ln2_b'])
    return x, k.reshape(B, T, N_HEADS, HEAD_DIM), v, h_last, new_buf


def setup_inputs(seed: int = 0) -> dict:
    key = jax.random.key(seed)
    ks = jax.random.split(key, 24)
    f32 = jnp.float32
    nrm = lambda k, shape, s: jax.random.normal(k, shape, f32) * s
    col_scale = jnp.concatenate([jnp.ones((2 * ATTN_WIDTH,), f32),
                                 jnp.full((ATTN_WIDTH,), DEEPNORM_BETA, f32),
                                 jnp.ones((2 * LRU_WIDTH,), f32)])
    a0 = jax.random.uniform(ks[13], (DEPTH, LRU_WIDTH), f32, 0.9, 0.999)
    return {
        'x_prompt': nrm(ks[0], (BATCH, SEQ, D_MODEL), 1.0),
        'x_sample': nrm(ks[1], (DEC_BATCH, DEC_SEQ, D_MODEL), 1.0),
        'cache_k': nrm(ks[2], (DEPTH, DEC_BATCH, PAST_LEN, N_HEADS, HEAD_DIM), 1.0),
        'cache_v': nrm(ks[3], (DEPTH, DEC_BATCH, PAST_LEN, N_HEADS, HEAD_DIM), 1.0),
        'state_h': nrm(ks[4], (DEPTH, DEC_BATCH, LRU_WIDTH), 0.5),
        'state_conv': nrm(ks[5], (DEPTH, DEC_BATCH, CONV_WIDTH - 1, LRU_WIDTH), 1.0),
        'w_in': nrm(ks[6], (DEPTH, D_MODEL, IN_COLS), D_MODEL ** -0.5) * col_scale,
        'lambda_q1': nrm(ks[7], (DEPTH, HALF_DIM), 0.1),
        'lambda_k1': nrm(ks[8], (DEPTH, HALF_DIM), 0.1),
        'lambda_q2': nrm(ks[9], (DEPTH, HALF_DIM), 0.1),
        'lambda_k2': nrm(ks[10], (DEPTH, HALF_DIM), 0.1),
        'subln_g': 1.0 + nrm(ks[11], (DEPTH, HEAD_DIM), 0.01),
        'conv_w': nrm(ks[12], (DEPTH, CONV_WIDTH, LRU_WIDTH), CONV_WIDTH ** -0.5),
        'conv_b': nrm(ks[14], (DEPTH, LRU_WIDTH), 0.01),
        'w_rg_a': nrm(ks[15], (DEPTH, LRU_BLOCKS, LRU_BLOCK_DIM, LRU_BLOCK_DIM), LRU_BLOCK_DIM ** -0.5),
        'b_rg_a': nrm(ks[16], (DEPTH, LRU_WIDTH), 0.01),
        'w_rg_i': nrm(ks[17], (DEPTH, LRU_BLOCKS, LRU_BLOCK_DIM, LRU_BLOCK_DIM), LRU_BLOCK_DIM ** -0.5),
        'b_rg_i': nrm(ks[18], (DEPTH, LRU_WIDTH), 0.01),
        'lru_lambda': jnp.log(a0 / (1.0 - a0)),
        'w_out': nrm(ks[19], (DEPTH, D_MODEL, D_MODEL), D_MODEL ** -0.5 * DEEPNORM_BETA),
        'ln1_g': 1.0 + nrm(ks[20], (DEPTH, D_MODEL), 0.01),
        'ln1_b': nrm(ks[21], (DEPTH, D_MODEL), 0.01),
        'w_up': nrm(ks[22], (DEPTH, D_MODEL, D_FF), D_MODEL ** -0.5 * DEEPNORM_BETA),
        'w_down': nrm(ks[23], (DEPTH, D_FF, D_MODEL), D_FF ** -0.5 * DEEPNORM_BETA),
        'ln2_g': 1.0 + nrm(jax.random.fold_in(ks[20], 1), (DEPTH, D_MODEL), 0.01),
        'ln2_b': nrm(jax.random.fold_in(ks[21], 1), (DEPTH, D_MODEL), 0.01),
    }


def reference(x_prompt, x_sample, cache_k, cache_v, state_h, state_conv,
              w_in, lambda_q1, lambda_k1, lambda_q2, lambda_k2, subln_g,
              conv_w, conv_b, w_rg_a, b_rg_a, w_rg_i, b_rg_i, lru_lambda,
              w_out, ln1_g, ln1_b, w_up, w_down, ln2_g, ln2_b):
    B, S = x_prompt.shape[0], x_prompt.shape[1]
    T = x_sample.shape[1]
    P = cache_k.shape[2]
    pos_prompt = jnp.arange(S, dtype=jnp.int32)
    pos_sample = P + jnp.arange(T, dtype=jnp.int32)
    yp, ys = x_prompt, x_sample
    kp_l, vp_l, hp_l, cp_l, ks_l, vs_l, hs_l, cs_l = [], [], [], [], [], [], [], []
    for l in range(DEPTH):
        p = {'w_in': w_in[l], 'lambda_q1': lambda_q1[l], 'lambda_k1': lambda_k1[l],
             'lambda_q2': lambda_q2[l], 'lambda_k2': lambda_k2[l], 'subln_g': subln_g[l],
             'conv_w': conv_w[l], 'conv_b': conv_b[l], 'w_rg_a': w_rg_a[l], 'b_rg_a': b_rg_a[l],
             'w_rg_i': w_rg_i[l], 'b_rg_i': b_rg_i[l], 'lru_lambda': lru_lambda[l],
             'w_out': w_out[l], 'ln1_g': ln1_g[l], 'ln1_b': ln1_b[l],
             'w_up': w_up[l], 'w_down': w_down[l], 'ln2_g': ln2_g[l], 'ln2_b': ln2_b[l]}
        lam0 = lambda_init(l)
        zero_buf = jnp.zeros((B, CONV_WIDTH - 1, LRU_WIDTH), x_prompt.dtype)
        zero_h = jnp.zeros((B, LRU_WIDTH), state_h.dtype)
        yp, kp, vp, hp, cp = trunk_layer(yp, pos_prompt, None, None, zero_buf, zero_h, p, lam0)
        ys, kn, vn, hn, cn = trunk_layer(ys, pos_sample, cache_k[l], cache_v[l], state_conv[l], state_h[l], p, lam0)
        kp_l.append(kp); vp_l.append(vp); hp_l.append(hp); cp_l.append(cp)
        ks_l.append(kn); vs_l.append(vn); hs_l.append(hn); cs_l.append(cn)
    k_prompt = jnp.stack(kp_l); v_prompt = jnp.stack(vp_l)
    h_prompt = jnp.stack(hp_l); conv_prompt = jnp.stack(cp_l)
    k_sample = jnp.stack(ks_l); v_sample = jnp.stack(vs_l)
    h_sample = jnp.stack(hs_l); conv_sample = jnp.stack(cs_l)
    return (yp, ys, k_prompt, v_prompt, h_prompt, conv_prompt, k_sample, v_sample, h_sample, conv_sample)
```

```python
import functools
import math

import jax
import jax.numpy as jnp
from jax import lax
from jax.experimental import pallas as pl
from jax.experimental.pallas import tpu as pltpu

CHUNK = 64
HEAD_DIM = 128
HALF_DIM = HEAD_DIM // 2
LRU_BLOCKS = 16
CONV_WIDTH = 4
LRU_C = 8.0
ROPE_THETA = 10000.0
LN_EPS = 1e-5
RMS_EPS = 1e-5
NEG_INF = -1e30

SUBLANES = 8
LANES = 128
MXU_DIM = 256
VMEM_LIMIT_BYTES = 56 * 1024 * 1024

BF16 = jnp.bfloat16
F32 = jnp.float32


def _lambda_init(layer):
    return 0.8 - 0.6 * math.exp(-0.3 * layer)


def _dot(a, b):
    return jnp.dot(a, b, preferred_element_type=F32)


def _dot_nt(a, b):
    return lax.dot_general(a, b, (((1,), (1,)), ((), ())), preferred_element_type=F32)


def _layer_norm(z, g, b):
    mu = jnp.mean(z, axis=-1, keepdims=True)
    d = z - mu
    var = jnp.mean(d * d, axis=-1, keepdims=True)
    return d * lax.rsqrt(var + LN_EPS) * g + b


def _lam_kernel(q1_ref, k1_ref, q2_ref, k2_ref, o_ref, *, lam_init):
    s1 = jnp.sum(q1_ref[...] * k1_ref[...], axis=-1, keepdims=True)
    s2 = jnp.sum(q2_ref[...] * k2_ref[...], axis=-1, keepdims=True)
    lam = jnp.exp(s1) - jnp.exp(s2) + lam_init
    o_ref[...] = jnp.broadcast_to(lam, o_ref.shape)


def _lam_call(lq1, lk1, lq2, lk2, lam_init):
    return pl.pallas_call(
        functools.partial(_lam_kernel, lam_init=lam_init),
        out_shape=jax.ShapeDtypeStruct((SUBLANES, LANES), F32),
        name="lam",
    )(lq1, lk1, lq2, lk2)


def _group_scan(a, u, sub):
    for s in (1, 2, 4):
        keep = sub >= s
        a_sh = pltpu.roll(a, s, axis=0)
        u_sh = pltpu.roll(u, s, axis=0)
        u = u + jnp.where(keep, a * u_sh, 0.0)
        a = a * jnp.where(keep, a_sh, 1.0)
    return a, u


def _in_lru_kernel(x_ref, w_ref, cos_ref, sin_ref, cw_ref, cb_ref, wg_ref, ba_ref, bi_ref,
                   lam_ref, h0_ref, c0_ref,
                   q_ref, k32_ref, k16_ref, v32_ref, v16_ref, ol_ref, hl_ref, cl_ref,
                   xp_scr, xc32_scr, xc16_scr, gate_scr, g_scr, h_scr,
                   *, tm, seq_rows, tiles_per_seq, aw, lw):
    carried = seq_rows == tm
    nseq = tm // seq_rows
    gps = seq_rows // SUBLANES
    i = pl.program_id(0)

    xb16 = x_ref[...].astype(BF16)

    cos = cos_ref[...]
    sin = sin_ref[...]
    lane = lax.broadcasted_iota(jnp.int32, (tm, LANES), 1)
    first_half = (lane % HALF_DIM) < (HALF_DIM // 2)

    def rope(xs):
        rot = jnp.where(first_half, pltpu.roll(xs, LANES - HALF_DIM // 2, axis=1),
                        pltpu.roll(xs, HALF_DIM // 2, axis=1))
        return xs * cos + rot * sin

    qp = _dot(xb16, w_ref[:, 0:aw])
    for h in range(aw // LANES):
        sl = slice(h * LANES, (h + 1) * LANES)
        q_ref[:, sl] = (rope(qp[:, sl]) * (HALF_DIM ** -0.5)).astype(BF16)
    kp = _dot(xb16, w_ref[:, aw:2 * aw])
    for h in range(aw // LANES):
        sl = slice(h * LANES, (h + 1) * LANES)
        kr = rope(kp[:, sl])
        k32_ref[:, sl] = kr
        k16_ref[:, sl] = kr.astype(BF16)
    vp = _dot(xb16, w_ref[:, 2 * aw:3 * aw])
    v32_ref[...] = vp
    v16_ref[...] = vp.astype(BF16)

    xbp = _dot(xb16, w_ref[:, 3 * aw:3 * aw + lw])
    g_scr[...] = _dot(xb16, w_ref[:, 3 * aw + lw:3 * aw + 2 * lw])

    if carried:
        @pl.when(i % tiles_per_seq == 0)
        def _():
            xp_scr[0, 0:SUBLANES, :] = jnp.zeros((SUBLANES, lw), F32)
            h_scr[...] = jnp.zeros_like(h_scr)
    else:
        for sq in range(nseq):
            xp_scr[sq, SUBLANES - (CONV_WIDTH - 1):SUBLANES, :] = c0_ref[sq]
    for sq in range(nseq):
        xp_scr[sq, SUBLANES:, :] = xbp[sq * seq_rows:(sq + 1) * seq_rows, :]

    cw = cw_ref[...]
    cb = cb_ref[...]
    for sq in range(nseq):
        for gg in range(gps):
            r0 = SUBLANES + gg * SUBLANES
            xc = cb + cw[CONV_WIDTH - 1:CONV_WIDTH, :] * xp_scr[sq, r0:r0 + SUBLANES, :]
            for j in range(CONV_WIDTH - 1):
                d = CONV_WIDTH - 1 - j
                xc = xc + cw[j:j + 1, :] * xp_scr[sq, r0 - d:r0 - d + SUBLANES, :]
            g0 = (sq * gps + gg) * SUBLANES
            xc32_scr[g0:g0 + SUBLANES, :] = xc
            xc16_scr[g0:g0 + SUBLANES, :] = xc.astype(BF16)

    for sq in range(nseq):
        cl_ref[sq] = xp_scr[sq, SUBLANES + seq_rows - (CONV_WIDTH - 1):SUBLANES + seq_rows, :]
    if carried:
        xp_scr[0, 0:SUBLANES, :] = xp_scr[0, seq_rows:seq_rows + SUBLANES, :]

    for blk in range(lw // MXU_DIM):
        sl = slice(blk * MXU_DIM, (blk + 1) * MXU_DIM)
        gate_scr[:, 2 * blk * MXU_DIM:2 * (blk + 1) * MXU_DIM] = _dot(xc16_scr[:, sl], wg_ref[blk])

    lam = lam_ref[...]
    nl = -lam
    softplus = jnp.maximum(nl, 0.0) + jnp.log1p(jnp.exp(-jnp.abs(nl)))
    decay = -LRU_C * softplus
    ba = ba_ref[...]
    bi = bi_ref[...]
    sub = lax.broadcasted_iota(jnp.int32, (SUBLANES, lw), 0)
    for sq in range(nseq):
        hin = h_scr[...] if carried else h0_ref[sq]
        for gg in range(gps):
            g0 = (sq * gps + gg) * SUBLANES
            rows = slice(g0, g0 + SUBLANES)
            pre = gate_scr[rows, :]
            ra = jnp.concatenate(
                [pre[:, 2 * b * MXU_DIM:(2 * b + 1) * MXU_DIM] for b in range(lw // MXU_DIM)], axis=1)
            ri = jnp.concatenate(
                [pre[:, (2 * b + 1) * MXU_DIM:(2 * b + 2) * MXU_DIM] for b in range(lw // MXU_DIM)], axis=1)
            r = jax.nn.sigmoid(ra + ba)
            ig = jax.nn.sigmoid(ri + bi)
            xc = xc32_scr[rows, :]
            log_a = decay * r
            a = jnp.exp(log_a)
            u = jnp.sqrt((1.0 + a * a) * jnp.tanh(-log_a)) * (ig * xc)
            acum, ucum = _group_scan(a, u, sub)
            hs = acum * hin + ucum
            hin = hs[SUBLANES - 1:SUBLANES, :]
            ol_ref[rows, :] = (hs * jax.nn.gelu(g_scr[rows, :])).astype(BF16)
        hl_ref[sq] = hin
        if carried:
            h_scr[...] = hin


def _in_lru_call(x2d, w_in16, cos, sin, conv_w, conv_b, wg, b_a, b_i, lru_lambda, h0, c0,
                 *, tm, seq_rows, rows_per_batch):
    rows, d_model = x2d.shape
    aw = (w_in16.shape[1] - 2 * conv_w.shape[1]) // 3
    lw = conv_w.shape[1]
    carried = seq_rows == tm
    n_tiles = rows // tm
    tiles_per_seq = rows_per_batch // tm if carried else 1
    nseq_tile = tm // seq_rows
    nseq_total = rows // rows_per_batch
    if carried:
        state_map = lambda i: (i // tiles_per_seq, 0, 0)
    else:
        state_map = lambda i: (i, 0, 0)
    const2 = lambda i: (0, 0)
    const3 = lambda i: (0, 0, 0)
    row_map = lambda i: (i, 0)
    resident = dict(pipeline_mode=pl.Buffered(1))
    in_specs = [
        pl.BlockSpec((tm, d_model), row_map),
        pl.BlockSpec(w_in16.shape, const2, **resident),
        pl.BlockSpec((tm, LANES), (lambda i: (i % tiles_per_seq, 0)) if carried else const2),
        pl.BlockSpec((tm, LANES), (lambda i: (i % tiles_per_seq, 0)) if carried else const2),
        pl.BlockSpec(conv_w.shape, const2),
        pl.BlockSpec(conv_b.shape, const2),
        pl.BlockSpec(wg.shape, const3, **resident),
        pl.BlockSpec(b_a.shape, const2),
        pl.BlockSpec(b_i.shape, const2),
        pl.BlockSpec(lru_lambda.shape, const2),
        pl.BlockSpec((nseq_tile, 1, lw), state_map),
        pl.BlockSpec((nseq_tile, CONV_WIDTH - 1, lw), state_map),
    ]
    out_shape = (
        jax.ShapeDtypeStruct((rows, aw), BF16),
        jax.ShapeDtypeStruct((rows, aw), F32),
        jax.ShapeDtypeStruct((rows, aw), BF16),
        jax.ShapeDtypeStruct((rows, aw), F32),
        jax.ShapeDtypeStruct((rows, aw), BF16),
        jax.ShapeDtypeStruct((rows, lw), BF16),
        jax.ShapeDtypeStruct((nseq_total, 1, lw), F32),
        jax.ShapeDtypeStruct((nseq_total, CONV_WIDTH - 1, lw), F32),
    )
    out_specs = (
        pl.BlockSpec((tm, aw), row_map),
        pl.BlockSpec((tm, aw), row_map),
        pl.BlockSpec((tm, aw), row_map),
        pl.BlockSpec((tm, aw), row_map),
        pl.BlockSpec((tm, aw), row_map),
        pl.BlockSpec((tm, lw), row_map),
        pl.BlockSpec((nseq_tile, 1, lw), state_map),
        pl.BlockSpec((nseq_tile, CONV_WIDTH - 1, lw), state_map),
    )
    scratch = [
        pltpu.VMEM((nseq_tile, SUBLANES + seq_rows, lw), F32),
        pltpu.VMEM((tm, lw), F32),
        pltpu.VMEM((tm, lw), BF16),
        pltpu.VMEM((tm, 2 * lw), F32),
        pltpu.VMEM((tm, lw), F32),
        pltpu.VMEM((1, lw), F32),
    ]
    kern = functools.partial(_in_lru_kernel, tm=tm, seq_rows=seq_rows, tiles_per_seq=tiles_per_seq,
                             aw=aw, lw=lw)
    return pl.pallas_call(
        kern, grid=(n_tiles,), in_specs=in_specs, out_specs=out_specs, out_shape=out_shape,
        scratch_shapes=scratch,
        compiler_params=pltpu.CompilerParams(dimension_semantics=("arbitrary",),
                                             vmem_limit_bytes=VMEM_LIMIT_BYTES),
        name="in_lru",
    )(x2d, w_in16, cos, sin, conv_w, conv_b, wg, b_a, b_i, lru_lambda, h0, c0)


def _subln(o, g, lam_init):
    ms = jnp.mean(o * o, axis=-1, keepdims=True)
    return o * lax.rsqrt(ms + RMS_EPS) * g * (1.0 - lam_init)


def _attn_kernel(lam_ref, g_ref, q_ref, k_ref, v_ref, o_ref, *, tq, lam_init):
    i = pl.program_id(2)
    q = q_ref[...]
    lane = lax.broadcasted_iota(jnp.int32, (tq, HEAD_DIM), 1)
    zero = jnp.zeros_like(q)
    qq = jnp.concatenate([jnp.where(lane < HALF_DIM, q, zero),
                          jnp.where(lane >= HALF_DIM, q, zero)], axis=0)

    def step(kj, vj, carry, mask):
        m, l, acc = carry
        s = _dot_nt(qq, kj)
        if mask is not None:
            s = jnp.where(mask, s, NEG_INF)
        m_new = jnp.maximum(m, jnp.max(s, axis=-1, keepdims=True))
        alpha = jnp.exp(m - m_new)
        p = jnp.exp(s - m_new)
        l = alpha * l + jnp.sum(p, axis=-1, keepdims=True)
        acc = alpha * acc + _dot(p.astype(BF16), vj)
        return m_new, l, acc

    def body(j, carry):
        r0 = pl.multiple_of(j * tq, tq)
        return step(k_ref[pl.ds(r0, tq), :], v_ref[pl.ds(r0, tq), :], carry, None)

    init = (jnp.full((2 * tq, 1), NEG_INF, F32), jnp.zeros((2 * tq, 1), F32),
            jnp.zeros((2 * tq, HEAD_DIM), F32))
    carry = lax.fori_loop(0, i, body, init)
    row = lax.broadcasted_iota(jnp.int32, (2 * tq, tq), 0) % tq
    col = lax.broadcasted_iota(jnp.int32, (2 * tq, tq), 1)
    mask = (row // CHUNK) >= (col // CHUNK)
    r0 = pl.multiple_of(i * tq, tq)
    m, l, acc = step(k_ref[pl.ds(r0, tq), :], v_ref[pl.ds(r0, tq), :], carry, mask)

    o = acc / l
    lam = lam_ref[0:1, :]
    o = o[0:tq, :] - lam * o[tq:2 * tq, :]
    o_ref[...] = _subln(o, g_ref[...], lam_init).astype(BF16)


def _attn_call(lam_tile, subln_g, q16, k16, v16, *, tq, lam_init):
    b, s, aw = q16.shape
    nh = aw // HEAD_DIM
    seq_spec = pl.BlockSpec((None, s, HEAD_DIM), lambda bi, hi, qi: (bi, 0, hi))
    tile_spec = pl.BlockSpec((None, tq, HEAD_DIM), lambda bi, hi, qi: (bi, qi, hi))
    return pl.pallas_call(
        functools.partial(_attn_kernel, tq=tq, lam_init=lam_init),
        grid=(b, nh, s // tq),
        in_specs=[pl.BlockSpec((SUBLANES, LANES), lambda bi, hi, qi: (0, 0)),
                  pl.BlockSpec((1, HEAD_DIM), lambda bi, hi, qi: (0, 0)),
                  tile_spec, seq_spec, seq_spec],
        out_specs=tile_spec,
        out_shape=jax.ShapeDtypeStruct((b, s, aw), BF16),
        compiler_params=pltpu.CompilerParams(
            dimension_semantics=("arbitrary", "arbitrary", "arbitrary"),
            vmem_limit_bytes=VMEM_LIMIT_BYTES),
        name="attn",
    )(lam_tile, subln_g, q16, k16, v16)


def _dec_attn_kernel(lam_ref, g_ref, q_ref, kn_ref, vn_ref, ck_ref, cv_ref, o_ref,
                     *, t, nh, lam_init):
    aw = nh * HEAD_DIM
    rows = 2 * nh * t
    q = q_ref[...]
    qt = jnp.concatenate([q] * (2 * nh), axis=0)
    rgrp = lax.broadcasted_iota(jnp.int32, (rows, aw), 0) // t
    comp = rgrp // nh
    head = rgrp % nh
    cgrp = lax.broadcasted_iota(jnp.int32, (rows, aw), 1) // HALF_DIM
    qbd = jnp.where(cgrp == 2 * head + comp, qt, jnp.zeros_like(qt))

    ck = ck_ref[...].astype(BF16)
    cv = cv_ref[...].astype(BF16)
    kn = kn_ref[...]
    vn = vn_ref[...]
    s_old = _dot_nt(qbd, ck)
    s_new = _dot_nt(qbd, kn)
    m = jnp.maximum(jnp.max(s_old, axis=-1, keepdims=True), jnp.max(s_new, axis=-1, keepdims=True))
    p_old = jnp.exp(s_old - m)
    p_new = jnp.exp(s_new - m)
    inv = 1.0 / (jnp.sum(p_old, axis=-1, keepdims=True) + jnp.sum(p_new, axis=-1, keepdims=True))
    lam = lam_ref[0:1, 0:1]
    half = rows // 2
    coef = jnp.concatenate([inv[0:half], lam * inv[half:rows]], axis=0)
    p_old = p_old * coef
    p_new = p_new * coef
    w_old = (p_old[0:half] - p_old[half:rows]).astype(BF16)
    w_new = (p_new[0:half] - p_new[half:rows]).astype(BF16)
    full = _dot(w_old, cv) + _dot(w_new, vn)
    g = g_ref[...]
    for h in range(nh):
        o = full[h * t:(h + 1) * t, h * HEAD_DIM:(h + 1) * HEAD_DIM]
        o_ref[:, h * HEAD_DIM:(h + 1) * HEAD_DIM] = _subln(o, g, lam_init).astype(BF16)


def _dec_attn_call(lam_tile, subln_g, q16, kn16, vn16, cache_k, cache_v, *, lam_init):
    bd, t, aw = q16.shape
    p = cache_k.shape[1]
    nh = aw // HEAD_DIM
    new_spec = pl.BlockSpec((None, t, aw), lambda b: (b, 0, 0))
    cache_spec = pl.BlockSpec((None, p, aw), lambda b: (b, 0, 0))
    return pl.pallas_call(
        functools.partial(_dec_attn_kernel, t=t, nh=nh, lam_init=lam_init),
        grid=(bd,),
        in_specs=[pl.BlockSpec((SUBLANES, LANES), lambda b: (0, 0)),
                  pl.BlockSpec((1, HEAD_DIM), lambda b: (0, 0)),
                  new_spec, new_spec, new_spec, cache_spec, cache_spec],
        out_specs=new_spec,
        out_shape=jax.ShapeDtypeStruct((bd, t, aw), BF16),
        compiler_params=pltpu.CompilerParams(dimension_semantics=("arbitrary",),
                                             vmem_limit_bytes=VMEM_LIMIT_BYTES),
        name="dec_attn",
    )(lam_tile, subln_g, q16, kn16, vn16, cache_k, cache_v)


def _out_mlp_kernel(x_ref, oa_ref, ol_ref, wo_ref, g1_ref, b1_ref, wu_ref, wd_ref, g2_ref, b2_ref,
                    y_ref, x1_scr, *, alpha):
    j = pl.program_id(1)

    @pl.when(j == 0)
    def _():
        cat = jnp.concatenate([oa_ref[...], ol_ref[...]], axis=1)
        x1 = _layer_norm(alpha * x_ref[...] + _dot(cat, wo_ref[...]), g1_ref[...], b1_ref[...])
        x1_scr[...] = x1.astype(BF16)
        y_ref[...] = alpha * x1

    hid = jnp.square(jnp.maximum(_dot(x1_scr[...], wu_ref[...]), 0.0)).astype(BF16)
    y_ref[...] += _dot(hid, wd_ref[...])

    @pl.when(j == pl.num_programs(1) - 1)
    def _():
        y_ref[...] = _layer_norm(y_ref[...], g2_ref[...], b2_ref[...])


def _out_mlp_call(x2d, oa16, ol16, w_out16, g1, b1, w_up16, w_down16, g2, b2, *, tm, tf, alpha):
    rows, d = x2d.shape
    aw = oa16.shape[1]
    lw = ol16.shape[1]
    d_ff = w_up16.shape[1]
    row = lambda i, j: (i, 0)
    const = lambda i, j: (0, 0)
    return pl.pallas_call(
        functools.partial(_out_mlp_kernel, alpha=alpha),
        grid=(rows // tm, d_ff // tf),
        in_specs=[pl.BlockSpec((tm, d), row),
                  pl.BlockSpec((tm, aw), row),
                  pl.BlockSpec((tm, lw), row),
                  pl.BlockSpec(w_out16.shape, const, pipeline_mode=pl.Buffered(1)),
                  pl.BlockSpec((1, d), const), pl.BlockSpec((1, d), const),
                  pl.BlockSpec((d, tf), lambda i, j: (0, j)),
                  pl.BlockSpec((tf, d), lambda i, j: (j, 0)),
                  pl.BlockSpec((1, d), const), pl.BlockSpec((1, d), const)],
        out_specs=pl.BlockSpec((tm, d), row),
        out_shape=jax.ShapeDtypeStruct((rows, d), F32),
        scratch_shapes=[pltpu.VMEM((tm, d), BF16)],
        compiler_params=pltpu.CompilerParams(dimension_semantics=("arbitrary", "arbitrary"),
                                             vmem_limit_bytes=VMEM_LIMIT_BYTES),
        name="out_mlp",
    )(x2d, oa16, ol16, w_out16, g1, b1, w_up16, w_down16, g2, b2)


def _rope_tables(pos):
    inv = ROPE_THETA ** (-jnp.arange(0, HALF_DIM, 2, dtype=F32) / HALF_DIM)
    ang = pos.astype(F32)[:, None] * inv[None, :]
    cos = jnp.tile(jnp.cos(ang), (1, 4))
    sin = jnp.sin(ang)
    sin = jnp.tile(jnp.concatenate([-sin, sin], axis=-1), (1, 2))
    return cos, sin


def _gate_weights(w_a, w_i):
    per = MXU_DIM // w_a.shape[-1]

    def dense(w):
        nb, bd, _ = w.shape
        w4 = w.reshape(nb // per, per, bd, bd)
        eye = jnp.eye(per, dtype=w.dtype)
        return jnp.einsum('gpij,pq->gpiqj', w4, eye).reshape(nb // per, per * bd, per * bd)

    return jnp.concatenate([dense(w_a), dense(w_i)], axis=-1).astype(BF16)


def _pick_tile(n, pref):
    t = min(n, pref)
    while n % t:
        t //= 2
    return t


def kernel(x_prompt, x_sample, cache_k, cache_v, state_h, state_conv, w_in, lambda_q1, lambda_k1, lambda_q2, lambda_k2, subln_g, conv_w, conv_b, w_rg_a, b_rg_a, w_rg_i, b_rg_i, lru_lambda, w_out, ln1_g, ln1_b, w_up, w_down, ln2_g, ln2_b):
    depth = w_in.shape[0]
    b, s, d = x_prompt.shape
    bd, t, _ = x_sample.shape
    p = cache_k.shape[2]
    lw = conv_w.shape[-1]
    aw = (w_in.shape[-1] - 2 * lw) // 3
    nh = aw // HEAD_DIM
    alpha = (2.0 * depth) ** 0.25

    cos_p, sin_p = _rope_tables(jnp.arange(s, dtype=jnp.int32))
    cos_s, sin_s = _rope_tables(p + jnp.arange(t, dtype=jnp.int32))
    cos_s = jnp.tile(cos_s, (bd, 1))
    sin_s = jnp.tile(sin_s, (bd, 1))

    tm_a = _pick_tile(s, 256)
    tq = _pick_tile(s, 256)
    tm_c = _pick_tile(b * s, 512)
    tf = _pick_tile(w_up.shape[-1], 512)

    yp = x_prompt.reshape(b * s, d)
    ys = x_sample.reshape(bd * t, d)
    outs = [[] for _ in range(8)]
    for l in range(depth):
        lam0 = _lambda_init(l)
        w_in16 = w_in[l].astype(BF16)
        w_out16 = w_out[l].astype(BF16)
        w_up16 = w_up[l].astype(BF16)
        w_down16 = w_down[l].astype(BF16)
        wg = _gate_weights(w_rg_a[l], w_rg_i[l])
        row = lambda v: v[l].reshape(1, -1)
        lam_tile = _lam_call(row(lambda_q1), row(lambda_k1), row(lambda_q2), row(lambda_k2), lam0)
        g_sub = row(subln_g)
        lru_args = (conv_w[l], row(conv_b), wg, row(b_rg_a), row(b_rg_i), row(lru_lambda))
        ln_args = (w_out16, row(ln1_g), row(ln1_b), w_up16, w_down16, row(ln2_g), row(ln2_b))

        q16, k32, k16, v32, v16, ol16, hp, cp = _in_lru_call(
            yp, w_in16, cos_p, sin_p, *lru_args,
            jnp.zeros((b, 1, lw), F32), jnp.zeros((b, CONV_WIDTH - 1, lw), F32),
            tm=tm_a, seq_rows=tm_a, rows_per_batch=s)
        oa16 = _attn_call(lam_tile, g_sub, q16.reshape(b, s, aw), k16.reshape(b, s, aw),
                          v16.reshape(b, s, aw), tq=tq, lam_init=lam0)
        yp = _out_mlp_call(yp, oa16.reshape(b * s, aw), ol16, *ln_args, tm=tm_c, tf=tf, alpha=alpha)
        outs[0].append(k32.reshape(b, s, nh, HEAD_DIM))
        outs[1].append(v32.reshape(b, s, nh, HEAD_DIM))
        outs[2].append(hp.reshape(b, lw))
        outs[3].append(cp)

        q16, k32, k16, v32, v16, ol16, hn, cn = _in_lru_call(
            ys, w_in16, cos_s, sin_s, *lru_args,
            state_h[l].reshape(bd, 1, lw), state_conv[l],
            tm=bd * t, seq_rows=t, rows_per_batch=t)
        oa16 = _dec_attn_call(lam_tile, g_sub, q16.reshape(bd, t, aw), k16.reshape(bd, t, aw),
                              v16.reshape(bd, t, aw), cache_k[l].reshape(bd, p, aw),
                              cache_v[l].reshape(bd, p, aw), lam_init=lam0)
        ys = _out_mlp_call(ys, oa16.reshape(bd * t, aw), ol16, *ln_args,
                           tm=_pick_tile(bd * t, 512), tf=tf, alpha=alpha)
        outs[4].append(k32.reshape(bd, t, nh, HEAD_DIM))
        outs[5].append(v32.reshape(bd, t, nh, HEAD_DIM))
        outs[6].append(hn.reshape(bd, lw))
        outs[7].append(cn)

    stk = [jnp.stack(o) for o in outs]
    return (yp.reshape(b, s, d), ys.reshape(bd, t, d), stk[0], stk[1], stk[2], stk[3],
            stk[4], stk[5], stk[6], stk[7])
```

```python
import functools
import math

import jax
import jax.numpy as jnp
from jax import lax
from jax.experimental import pallas as pl
from jax.experimental.pallas import tpu as pltpu

CHUNK = 64
HEAD_DIM = 128
HALF_DIM = HEAD_DIM // 2
LRU_BLOCKS = 16
CONV_WIDTH = 4
LRU_C = 8.0
ROPE_THETA = 10000.0
LN_EPS = 1e-5
RMS_EPS = 1e-5
NEG_INF = -1e30

SUBLANES = 8
LANES = 128
MXU_DIM = 256
VMEM_LIMIT_BYTES = 56 * 1024 * 1024

BF16 = jnp.bfloat16
F32 = jnp.float32


def _lambda_init(layer):
    return 0.8 - 0.6 * math.exp(-0.3 * layer)


def _dot(a, b):
    return jnp.dot(a, b, preferred_element_type=F32)


def _dot_nt(a, b):
    return lax.dot_general(a, b, (((1,), (1,)), ((), ())), preferred_element_type=F32)


def _layer_norm(z, g, b):
    mu = jnp.mean(z, axis=-1, keepdims=True)
    d = z - mu
    var = jnp.mean(d * d, axis=-1, keepdims=True)
    return d * lax.rsqrt(var + LN_EPS) * g + b


def _lam_kernel(q1_ref, k1_ref, q2_ref, k2_ref, o_ref, *, lam_init):
    s1 = jnp.sum(q1_ref[...] * k1_ref[...], axis=-1, keepdims=True)
    s2 = jnp.sum(q2_ref[...] * k2_ref[...], axis=-1, keepdims=True)
    lam = jnp.exp(s1) - jnp.exp(s2) + lam_init
    o_ref[...] = jnp.broadcast_to(lam, o_ref.shape)


def _lam_call(lq1, lk1, lq2, lk2, lam_init):
    return pl.pallas_call(
        functools.partial(_lam_kernel, lam_init=lam_init),
        out_shape=jax.ShapeDtypeStruct((SUBLANES, LANES), F32),
        name="lam",
    )(lq1, lk1, lq2, lk2)


def _group_scan(a, u, sub):
    for s in (1, 2, 4):
        keep = sub >= s
        a_sh = pltpu.roll(a, s, axis=0)
        u_sh = pltpu.roll(u, s, axis=0)
        u = u + jnp.where(keep, a * u_sh, 0.0)
        a = a * jnp.where(keep, a_sh, 1.0)
    return a, u


def _in_lru_kernel(x_ref, w_ref, cos_ref, sin_ref, cw_ref, cb_ref, wg_ref, ba_ref, bi_ref,
                   lam_ref, h0_ref, c0_ref,
                   q_ref, k32_ref, k16_ref, v32_ref, v16_ref, ol_ref, hl_ref, cl_ref,
                   xp_scr, xc32_scr, xc16_scr, gate_scr, g_scr, h_scr,
                   *, tm, seq_rows, tiles_per_seq, aw, lw):
    carried = seq_rows == tm
    nseq = tm // seq_rows
    gps = seq_rows // SUBLANES
    i = pl.program_id(0)

    xb16 = x_ref[...].astype(BF16)

    cos = cos_ref[...]
    sin = sin_ref[...]
    lane = lax.broadcasted_iota(jnp.int32, (tm, LANES), 1)
    first_half = (lane % HALF_DIM) < (HALF_DIM // 2)

    def rope(xs):
        rot = jnp.where(first_half, pltpu.roll(xs, LANES - HALF_DIM // 2, axis=1),
                        pltpu.roll(xs, HALF_DIM // 2, axis=1))
        return xs * cos + rot * sin

    qp = _dot(xb16, w_ref[:, 0:aw])
    for h in range(aw // LANES):
        sl = slice(h * LANES, (h + 1) * LANES)
        q_ref[:, sl] = (rope(qp[:, sl]) * (HALF_DIM ** -0.5)).astype(BF16)
    kp = _dot(xb16, w_ref[:, aw:2 * aw])
    for h in range(aw // LANES):
        sl = slice(h * LANES, (h + 1) * LANES)
        kr = rope(kp[:, sl])
        k32_ref[:, sl] = kr
        k16_ref[:, sl] = kr.astype(BF16)
    vp = _dot(xb16, w_ref[:, 2 * aw:3 * aw])
    v32_ref[...] = vp
    v16_ref[...] = vp.astype(BF16)

    xbp = _dot(xb16, w_ref[:, 3 * aw:3 * aw + lw])
    g_scr[...] = _dot(xb16, w_ref[:, 3 * aw + lw:3 * aw + 2 * lw])

    if carried:
        @pl.when(i % tiles_per_seq == 0)
        def _():
            xp_scr[0, 0:SUBLANES, :] = jnp.zeros((SUBLANES, lw), F32)
            h_scr[...] = jnp.zeros_like(h_scr)
    else:
        for sq in range(nseq):
            xp_scr[sq, SUBLANES - (CONV_WIDTH - 1):SUBLANES, :] = c0_ref[sq]
    for sq in range(nseq):
        xp_scr[sq, SUBLANES:, :] = xbp[sq * seq_rows:(sq + 1) * seq_rows, :]

    cw = cw_ref[...]
    cb = cb_ref[...]
    for sq in range(nseq):
        for gg in range(gps):
            r0 = SUBLANES + gg * SUBLANES
            xc = cb + cw[CONV_WIDTH - 1:CONV_WIDTH, :] * xp_scr[sq, r0:r0 + SUBLANES, :]
            for j in range(CONV_WIDTH - 1):
                d = CONV_WIDTH - 1 - j
                xc = xc + cw[j:j + 1, :] * xp_scr[sq, r0 - d:r0 - d + SUBLANES, :]
            g0 = (sq * gps + gg) * SUBLANES
            xc32_scr[g0:g0 + SUBLANES, :] = xc
            xc16_scr[g0:g0 + SUBLANES, :] = xc.astype(BF16)

    for sq in range(nseq):
        cl_ref[sq] = xp_scr[sq, SUBLANES + seq_rows - (CONV_WIDTH - 1):SUBLANES + seq_rows, :]
    if carried:
        xp_scr[0, 0:SUBLANES, :] = xp_scr[0, seq_rows:seq_rows + SUBLANES, :]

    for blk in range(lw // MXU_DIM):
        sl = slice(blk * MXU_DIM, (blk + 1) * MXU_DIM)
        gate_scr[:, 2 * blk * MXU_DIM:2 * (blk + 1) * MXU_DIM] = _dot(xc16_scr[:, sl], wg_ref[blk])

    lam = lam_ref[...]
    nl = -lam
    softplus = jnp.maximum(nl, 0.0) + jnp.log1p(jnp.exp(-jnp.abs(nl)))
    decay = -LRU_C * softplus
    ba = ba_ref[...]
    bi = bi_ref[...]
    sub = lax.broadcasted_iota(jnp.int32, (SUBLANES, lw), 0)
    for sq in range(nseq):
        hin = h_scr[...] if carried else h0_ref[sq]
        for gg in range(gps):
            g0 = (sq * gps + gg) * SUBLANES
            rows = slice(g0, g0 + SUBLANES)
            pre = gate_scr[rows, :]
            ra = jnp.concatenate(
                [pre[:, 2 * b * MXU_DIM:(2 * b + 1) * MXU_DIM] for b in range(lw // MXU_DIM)], axis=1)
            ri = jnp.concatenate(
                [pre[:, (2 * b + 1) * MXU_DIM:(2 * b + 2) * MXU_DIM] for b in range(lw // MXU_DIM)], axis=1)
            r = jax.nn.sigmoid(ra + ba)
            ig = jax.nn.sigmoid(ri + bi)
            xc = xc32_scr[rows, :]
            log_a = decay * r
            a = jnp.exp(log_a)
            u = jnp.sqrt((1.0 + a * a) * jnp.tanh(-log_a)) * (ig * xc)
            acum, ucum = _group_scan(a, u, sub)
            hs = acum * hin + ucum
            hin = hs[SUBLANES - 1:SUBLANES, :]
            ol_ref[rows, :] = (hs * jax.nn.gelu(g_scr[rows, :])).astype(BF16)
        hl_ref[sq] = hin
        if carried:
            h_scr[...] = hin


def _in_lru_call(x2d, w_in16, cos, sin, conv_w, conv_b, wg, b_a, b_i, lru_lambda, h0, c0,
                 *, tm, seq_rows, rows_per_batch):
    rows, d_model = x2d.shape
    aw = (w_in16.shape[1] - 2 * conv_w.shape[1]) // 3
    lw = conv_w.shape[1]
    carried = seq_rows == tm
    n_tiles = rows // tm
    tiles_per_seq = rows_per_batch // tm if carried else 1
    nseq_tile = tm // seq_rows
    nseq_total = rows // rows_per_batch
    if carried:
        state_map = lambda i: (i // tiles_per_seq, 0, 0)
    else:
        state_map = lambda i: (i, 0, 0)
    const2 = lambda i: (0, 0)
    const3 = lambda i: (0, 0, 0)
    row_map = lambda i: (i, 0)
    resident = dict(pipeline_mode=pl.Buffered(1))
    in_specs = [
        pl.BlockSpec((tm, d_model), row_map),
        pl.BlockSpec(w_in16.shape, const2, **resident),
        pl.BlockSpec((tm, LANES), (lambda i: (i % tiles_per_seq, 0)) if carried else const2),
        pl.BlockSpec((tm, LANES), (lambda i: (i % tiles_per_seq, 0)) if carried else const2),
        pl.BlockSpec(conv_w.shape, const2),
        pl.BlockSpec(conv_b.shape, const2),
        pl.BlockSpec(wg.shape, const3, **resident),
        pl.BlockSpec(b_a.shape, const2),
        pl.BlockSpec(b_i.shape, const2),
        pl.BlockSpec(lru_lambda.shape, const2),
        pl.BlockSpec((nseq_tile, 1, lw), state_map),
        pl.BlockSpec((nseq_tile, CONV_WIDTH - 1, lw), state_map),
    ]
    out_shape = (
        jax.ShapeDtypeStruct((rows, aw), BF16),
        jax.ShapeDtypeStruct((rows, aw), F32),
        jax.ShapeDtypeStruct((rows, aw), BF16),
        jax.ShapeDtypeStruct((rows, aw), F32),
        jax.ShapeDtypeStruct((rows, aw), BF16),
        jax.ShapeDtypeStruct((rows, lw), BF16),
        jax.ShapeDtypeStruct((nseq_total, 1, lw), F32),
        jax.ShapeDtypeStruct((nseq_total, CONV_WIDTH - 1, lw), F32),
    )
    out_specs = (
        pl.BlockSpec((tm, aw), row_map),
        pl.BlockSpec((tm, aw), row_map),
        pl.BlockSpec((tm, aw), row_map),
        pl.BlockSpec((tm, aw), row_map),
        pl.BlockSpec((tm, aw), row_map),
        pl.BlockSpec((tm, lw), row_map),
        pl.BlockSpec((nseq_tile, 1, lw), state_map),
        pl.BlockSpec((nseq_tile, CONV_WIDTH - 1, lw), state_map),
    )
    scratch = [
        pltpu.VMEM((nseq_tile, SUBLANES + seq_rows, lw), F32),
        pltpu.VMEM((tm, lw), F32),
        pltpu.VMEM((tm, lw), BF16),
        pltpu.VMEM((tm, 2 * lw), F32),
        pltpu.VMEM((tm, lw), F32),
        pltpu.VMEM((1, lw), F32),
    ]
    kern = functools.partial(_in_lru_kernel, tm=tm, seq_rows=seq_rows, tiles_per_seq=tiles_per_seq,
                             aw=aw, lw=lw)
    return pl.pallas_call(
        kern, grid=(n_tiles,), in_specs=in_specs, out_specs=out_specs, out_shape=out_shape,
        scratch_shapes=scratch,
        compiler_params=pltpu.CompilerParams(dimension_semantics=("arbitrary",),
                                             vmem_limit_bytes=VMEM_LIMIT_BYTES),
        name="in_lru",
    )(x2d, w_in16, cos, sin, conv_w, conv_b, wg, b_a, b_i, lru_lambda, h0, c0)


def _attn_kernel(lam_ref, g_ref, q_ref, k_ref, vt_ref, o_ref, s_scr, m_scr, l_scr, acc_scr,
                 *, tq, tk, lam_init):
    i = pl.program_id(2)
    q = q_ref[...]
    lane = lax.broadcasted_iota(jnp.int32, (tq, HEAD_DIM), 1)
    zero = jnp.zeros_like(q)
    qq = jnp.concatenate([jnp.where(lane < HALF_DIM, q, zero),
                          jnp.where(lane >= HALF_DIM, q, zero)], axis=0)

    def scores(j, slot):
        kj = k_ref[pl.ds(pl.multiple_of(j * tk, tk), tk), :]
        s_scr[slot] = _dot_nt(kj, qq)

    def softmax_pv(j, slot, masked):
        s = s_scr[slot]
        if masked:
            key = j * tk + lax.broadcasted_iota(jnp.int32, (tk, 2 * tq), 0)
            qry = i * tq + lax.broadcasted_iota(jnp.int32, (tk, 2 * tq), 1) % tq
            s = jnp.where((key // CHUNK) <= (qry // CHUNK), s, NEG_INF)
        m = m_scr[...]
        m_new = jnp.maximum(m, jnp.max(s, axis=0, keepdims=True))
        alpha = jnp.exp(m - m_new)
        p = jnp.exp(s - m_new)
        m_scr[...] = m_new
        l_scr[...] = alpha * l_scr[...] + jnp.sum(p, axis=0, keepdims=True)
        acc_scr[...] = alpha * acc_scr[...] + _dot(vt_ref[j], p.astype(BF16))

    m_scr[...] = jnp.full(m_scr.shape, NEG_INF, F32)
    l_scr[...] = jnp.zeros(l_scr.shape, F32)
    acc_scr[...] = jnp.zeros(acc_scr.shape, F32)
    n_full = (i * tq) // tk

    scores(0, 0)

    def pair(jj, _):
        scores(2 * jj + 1, 1)
        softmax_pv(2 * jj, 0, False)
        scores(2 * jj + 2, 0)
        softmax_pv(2 * jj + 1, 1, False)
        return 0

    lax.fori_loop(0, n_full // 2, pair, 0)

    @pl.when(n_full % 2 == 0)
    def _():
        softmax_pv(n_full, 0, True)

    @pl.when(n_full % 2 == 1)
    def _():
        scores(n_full, 1)
        softmax_pv(n_full - 1, 0, False)
        softmax_pv(n_full, 1, True)

    lam = lam_ref[0:1, 0:1]
    o = acc_scr[...] / l_scr[...]
    o = o[:, 0:tq] - lam * o[:, tq:2 * tq]
    ms = jnp.mean(o * o, axis=0, keepdims=True)
    o = o * lax.rsqrt(ms + RMS_EPS) * g_ref[...] * (1.0 - lam_init)
    o_ref[...] = o.T.astype(BF16)


def _attn_call(lam_tile, subln_g, q16, k16, vt16, *, tq, tk, lam_init):
    b, s, aw = q16.shape
    nh = aw // HEAD_DIM
    tile_spec = pl.BlockSpec((None, tq, HEAD_DIM), lambda bi, hi, qi: (bi, qi, hi))
    return pl.pallas_call(
        functools.partial(_attn_kernel, tq=tq, tk=tk, lam_init=lam_init),
        grid=(b, nh, s // tq),
        in_specs=[pl.BlockSpec((SUBLANES, LANES), lambda bi, hi, qi: (0, 0)),
                  pl.BlockSpec((HEAD_DIM, 1), lambda bi, hi, qi: (0, 0)),
                  tile_spec,
                  pl.BlockSpec((None, s, HEAD_DIM), lambda bi, hi, qi: (bi, 0, hi)),
                  pl.BlockSpec((None, None, s // tk, HEAD_DIM, tk), lambda bi, hi, qi: (bi, hi, 0, 0, 0))],
        out_specs=tile_spec,
        out_shape=jax.ShapeDtypeStruct((b, s, aw), BF16),
        scratch_shapes=[pltpu.VMEM((2, tk, 2 * tq), F32),
                        pltpu.VMEM((1, 2 * tq), F32),
                        pltpu.VMEM((1, 2 * tq), F32),
                        pltpu.VMEM((HEAD_DIM, 2 * tq), F32)],
        compiler_params=pltpu.CompilerParams(
            dimension_semantics=("arbitrary", "arbitrary", "arbitrary"),
            vmem_limit_bytes=VMEM_LIMIT_BYTES),
        name="attn",
    )(lam_tile, subln_g.reshape(HEAD_DIM, 1), q16, k16, vt16)


def _dec_attn_kernel(lam_ref, g_ref, q_ref, kn_ref, vn_ref, ck_ref, cv_ref, o_ref,
                     *, t, nh, kc, lam_init):
    rows = 2 * nh * t
    p_len = ck_ref.shape[0]
    q = q_ref[...]
    lane = lax.broadcasted_iota(jnp.int32, (t, HEAD_DIM), 1)
    parts = []
    for c in range(2):
        keep = (lane < HALF_DIM) if c == 0 else (lane >= HALF_DIM)
        for h in range(nh):
            qh = q[:, h * HEAD_DIM:(h + 1) * HEAD_DIM]
            parts.append(jnp.where(keep, qh, jnp.zeros_like(qh)))
    qall = jnp.concatenate(parts, axis=0)

    def scores(kf):
        s = _dot_nt(qall, kf.astype(BF16))
        row_head = (lax.broadcasted_iota(jnp.int32, s.shape, 0) // t) % nh
        col_head = lax.broadcasted_iota(jnp.int32, s.shape, 1) % nh
        return jnp.where(row_head == col_head, s, NEG_INF)

    def update(carry, s, vf):
        m, l, acc = carry
        m_new = jnp.maximum(m, jnp.max(s, axis=-1, keepdims=True))
        alpha = jnp.exp(m - m_new)
        p = jnp.exp(s - m_new)
        l = alpha * l + jnp.sum(p, axis=-1, keepdims=True)
        acc = alpha * acc + _dot(p.astype(BF16), vf.astype(BF16))
        return m_new, l, acc

    carry = (jnp.full((rows, 1), NEG_INF, F32), jnp.zeros((rows, 1), F32),
             jnp.zeros((rows, HEAD_DIM), F32))
    for j in range(p_len // kc):
        kf = ck_ref[j * kc:(j + 1) * kc].reshape(kc * nh, HEAD_DIM)
        vf = cv_ref[j * kc:(j + 1) * kc].reshape(kc * nh, HEAD_DIM)
        carry = update(carry, scores(kf), vf)
    kf = kn_ref[...].reshape(t * nh, HEAD_DIM)
    vf = vn_ref[...].reshape(t * nh, HEAD_DIM)
    _, l, acc = update(carry, scores(kf), vf)

    o = acc / l
    half = rows // 2
    o = o[0:half] - lam_ref[0:1, 0:1] * o[half:rows]
    ms = jnp.mean(o * o, axis=-1, keepdims=True)
    o = o * lax.rsqrt(ms + RMS_EPS) * g_ref[...] * (1.0 - lam_init)
    for h in range(nh):
        o_ref[:, h * HEAD_DIM:(h + 1) * HEAD_DIM] = o[h * t:(h + 1) * t].astype(BF16)


def _dec_attn_call(lam_tile, subln_g, q16, kn32, vn32, cache_k, cache_v, *, layer, lam_init):
    bd, t, aw = q16.shape
    _, p, nh, hd = cache_k.shape
    new_spec = pl.BlockSpec((None, t, nh, hd), lambda b: (b, 0, 0, 0))
    cache_spec = pl.BlockSpec((None, p, nh, hd), lambda b: (layer * bd + b, 0, 0, 0))
    io_spec = pl.BlockSpec((None, t, aw), lambda b: (b, 0, 0))
    return pl.pallas_call(
        functools.partial(_dec_attn_kernel, t=t, nh=nh, kc=_pick_tile(p, 256), lam_init=lam_init),
        grid=(bd,),
        in_specs=[pl.BlockSpec((SUBLANES, LANES), lambda b: (0, 0)),
                  pl.BlockSpec((1, HEAD_DIM), lambda b: (0, 0)),
                  io_spec, new_spec, new_spec, cache_spec, cache_spec],
        out_specs=io_spec,
        out_shape=jax.ShapeDtypeStruct((bd, t, aw), BF16),
        compiler_params=pltpu.CompilerParams(dimension_semantics=("arbitrary",),
                                             vmem_limit_bytes=VMEM_LIMIT_BYTES),
        name="dec_attn",
    )(lam_tile, subln_g, q16, kn32, vn32, cache_k, cache_v)


def _out_mlp_kernel(x_ref, oa_ref, ol_ref, wo_ref, g1_ref, b1_ref, wu_ref, wd_ref, g2_ref, b2_ref,
                    y_ref, x1_scr, *, alpha):
    j = pl.program_id(1)

    @pl.when(j == 0)
    def _():
        cat = jnp.concatenate([oa_ref[...], ol_ref[...]], axis=1)
        x1 = _layer_norm(alpha * x_ref[...] + _dot(cat, wo_ref[...]), g1_ref[...], b1_ref[...])
        x1_scr[...] = x1.astype(BF16)
        y_ref[...] = alpha * x1

    hid = jnp.square(jnp.maximum(_dot(x1_scr[...], wu_ref[...]), 0.0)).astype(BF16)
    y_ref[...] += _dot(hid, wd_ref[...])

    @pl.when(j == pl.num_programs(1) - 1)
    def _():
        y_ref[...] = _layer_norm(y_ref[...], g2_ref[...], b2_ref[...])


def _out_mlp_call(x2d, oa16, ol16, w_out16, g1, b1, w_up16, w_down16, g2, b2, *, tm, tf, alpha):
    rows, d = x2d.shape
    aw = oa16.shape[1]
    lw = ol16.shape[1]
    d_ff = w_up16.shape[1]
    row = lambda i, j: (i, 0)
    const = lambda i, j: (0, 0)
    return pl.pallas_call(
        functools.partial(_out_mlp_kernel, alpha=alpha),
        grid=(rows // tm, d_ff // tf),
        in_specs=[pl.BlockSpec((tm, d), row),
                  pl.BlockSpec((tm, aw), row),
                  pl.BlockSpec((tm, lw), row),
                  pl.BlockSpec(w_out16.shape, const, pipeline_mode=pl.Buffered(1)),
                  pl.BlockSpec((1, d), const), pl.BlockSpec((1, d), const),
                  pl.BlockSpec((d, tf), lambda i, j: (0, j)),
                  pl.BlockSpec((tf, d), lambda i, j: (j, 0)),
                  pl.BlockSpec((1, d), const), pl.BlockSpec((1, d), const)],
        out_specs=pl.BlockSpec((tm, d), row),
        out_shape=jax.ShapeDtypeStruct((rows, d), F32),
        scratch_shapes=[pltpu.VMEM((tm, d), BF16)],
        compiler_params=pltpu.CompilerParams(dimension_semantics=("arbitrary", "arbitrary"),
                                             vmem_limit_bytes=VMEM_LIMIT_BYTES),
        name="out_mlp",
    )(x2d, oa16, ol16, w_out16, g1, b1, w_up16, w_down16, g2, b2)


def _rope_tables(pos):
    inv = ROPE_THETA ** (-jnp.arange(0, HALF_DIM, 2, dtype=F32) / HALF_DIM)
    ang = pos.astype(F32)[:, None] * inv[None, :]
    cos = jnp.tile(jnp.cos(ang), (1, 4))
    sin = jnp.sin(ang)
    sin = jnp.tile(jnp.concatenate([-sin, sin], axis=-1), (1, 2))
    return cos, sin


def _gate_weights(w_a, w_i):
    per = MXU_DIM // w_a.shape[-1]

    def dense(w):
        nb, bd, _ = w.shape
        w4 = w.reshape(nb // per, per, bd, bd)
        eye = jnp.eye(per, dtype=w.dtype)
        return jnp.einsum('gpij,pq->gpiqj', w4, eye).reshape(nb // per, per * bd, per * bd)

    return jnp.concatenate([dense(w_a), dense(w_i)], axis=-1).astype(BF16)


def _pick_tile(n, pref):
    t = min(n, pref)
    while n % t:
        t //= 2
    return t


def kernel(x_prompt, x_sample, cache_k, cache_v, state_h, state_conv, w_in, lambda_q1, lambda_k1, lambda_q2, lambda_k2, subln_g, conv_w, conv_b, w_rg_a, b_rg_a, w_rg_i, b_rg_i, lru_lambda, w_out, ln1_g, ln1_b, w_up, w_down, ln2_g, ln2_b):
    depth = w_in.shape[0]
    b, s, d = x_prompt.shape
    bd, t, _ = x_sample.shape
    p = cache_k.shape[2]
    lw = conv_w.shape[-1]
    aw = (w_in.shape[-1] - 2 * lw) // 3
    nh = aw // HEAD_DIM
    alpha = (2.0 * depth) ** 0.25

    cos_p, sin_p = _rope_tables(jnp.arange(s, dtype=jnp.int32))
    cos_s, sin_s = _rope_tables(p + jnp.arange(t, dtype=jnp.int32))
    cos_s = jnp.tile(cos_s, (bd, 1))
    sin_s = jnp.tile(sin_s, (bd, 1))

    tm_a = _pick_tile(s, 256)
    tq = _pick_tile(s, 256)
    tk = _pick_tile(s, 512)
    tm_c = _pick_tile(b * s, 512)
    tf = _pick_tile(w_up.shape[-1], 1024)

    yp = x_prompt.reshape(b * s, d)
    ys = x_sample.reshape(bd * t, d)
    outs = [[] for _ in range(8)]
    for l in range(depth):
        lam0 = _lambda_init(l)
        w_in16 = w_in[l].astype(BF16)
        w_out16 = w_out[l].astype(BF16)
        w_up16 = w_up[l].astype(BF16)
        w_down16 = w_down[l].astype(BF16)
        wg = _gate_weights(w_rg_a[l], w_rg_i[l])
        row = lambda v: v[l].reshape(1, -1)
        lam_tile = _lam_call(row(lambda_q1), row(lambda_k1), row(lambda_q2), row(lambda_k2), lam0)
        g_sub = row(subln_g)
        lru_args = (conv_w[l], row(conv_b), wg, row(b_rg_a), row(b_rg_i), row(lru_lambda))
        ln_args = (w_out16, row(ln1_g), row(ln1_b), w_up16, w_down16, row(ln2_g), row(ln2_b))

        q16, k32, k16, v32, v16, ol16, hp, cp = _in_lru_call(
            yp, w_in16, cos_p, sin_p, *lru_args,
            jnp.zeros((b, 1, lw), F32), jnp.zeros((b, CONV_WIDTH - 1, lw), F32),
            tm=tm_a, seq_rows=tm_a, rows_per_batch=s)
        vt16 = v16.reshape(b, s // tk, tk, nh, HEAD_DIM).transpose(0, 3, 1, 4, 2)
        oa16 = _attn_call(lam_tile, g_sub, q16.reshape(b, s, aw), k16.reshape(b, s, aw),
                          vt16, tq=tq, tk=tk, lam_init=lam0)
        yp = _out_mlp_call(yp, oa16.reshape(b * s, aw), ol16, *ln_args, tm=tm_c, tf=tf, alpha=alpha)
        outs[0].append(k32.reshape(b, s, nh, HEAD_DIM))
        outs[1].append(v32.reshape(b, s, nh, HEAD_DIM))
        outs[2].append(hp.reshape(b, lw))
        outs[3].append(cp)

        q16, k32, k16, v32, v16, ol16, hn, cn = _in_lru_call(
            ys, w_in16, cos_s, sin_s, *lru_args,
            state_h[l].reshape(bd, 1, lw), state_conv[l],
            tm=bd * t, seq_rows=t, rows_per_batch=t)
        oa16 = _dec_attn_call(lam_tile, g_sub, q16.reshape(bd, t, aw),
                              k32.reshape(bd, t, nh, HEAD_DIM), v32.reshape(bd, t, nh, HEAD_DIM),
                              cache_k.reshape(depth * bd, p, nh, HEAD_DIM),
                              cache_v.reshape(depth * bd, p, nh, HEAD_DIM), layer=l, lam_init=lam0)
        ys = _out_mlp_call(ys, oa16.reshape(bd * t, aw), ol16, *ln_args,
                           tm=_pick_tile(bd * t, 512), tf=tf, alpha=alpha)
        outs[4].append(k32.reshape(bd, t, nh, HEAD_DIM))
        outs[5].append(v32.reshape(bd, t, nh, HEAD_DIM))
        outs[6].append(hn.reshape(bd, lw))
        outs[7].append(cn)

    stk = [jnp.stack(o) for o in outs]
    return (yp.reshape(b, s, d), ys.reshape(bd, t, d), stk[0], stk[1], stk[2], stk[3],
            stk[4], stk[5], stk[6], stk[7])
```

```python
import functools
import math

import jax
import jax.numpy as jnp
from jax import lax
from jax.experimental import pallas as pl
from jax.experimental.pallas import tpu as pltpu

CHUNK = 64
HEAD_DIM = 128
HALF_DIM = HEAD_DIM // 2
LRU_BLOCKS = 16
CONV_WIDTH = 4
LRU_C = 8.0
ROPE_THETA = 10000.0
LN_EPS = 1e-5
RMS_EPS = 1e-5
NEG_INF = -1e30

SUBLANES = 8
LANES = 128
MXU_DIM = 256
ONES_ROWS = 16
VMEM_LIMIT_BYTES = 56 * 1024 * 1024

Q_SCALE = HALF_DIM ** -0.5 * math.log2(math.e)

BF16 = jnp.bfloat16
F32 = jnp.float32


def _lambda_init(layer):
    return 0.8 - 0.6 * math.exp(-0.3 * layer)


def _dot(a, b):
    return jnp.dot(a, b, preferred_element_type=F32)


def _dot_nt(a, b):
    return lax.dot_general(a, b, (((1,), (1,)), ((), ())), preferred_element_type=F32)


def _layer_norm(z, g, b):
    mu = jnp.mean(z, axis=-1, keepdims=True)
    d = z - mu
    var = jnp.mean(d * d, axis=-1, keepdims=True)
    return d * lax.rsqrt(var + LN_EPS) * g + b


def _lam_kernel(q1_ref, k1_ref, q2_ref, k2_ref, o_ref, *, lam_init):
    s1 = jnp.sum(q1_ref[...] * k1_ref[...], axis=-1, keepdims=True)
    s2 = jnp.sum(q2_ref[...] * k2_ref[...], axis=-1, keepdims=True)
    lam = jnp.exp(s1) - jnp.exp(s2) + lam_init
    o_ref[...] = jnp.broadcast_to(lam, o_ref.shape)


def _lam_call(lq1, lk1, lq2, lk2, lam_init):
    return pl.pallas_call(
        functools.partial(_lam_kernel, lam_init=lam_init),
        out_shape=jax.ShapeDtypeStruct((SUBLANES, LANES), F32),
        name="lam",
    )(lq1, lk1, lq2, lk2)


def _group_scan(a, u, sub):
    for s in (1, 2, 4):
        keep = sub >= s
        a_sh = pltpu.roll(a, s, axis=0)
        u_sh = pltpu.roll(u, s, axis=0)
        u = u + jnp.where(keep, a * u_sh, 0.0)
        a = a * jnp.where(keep, a_sh, 1.0)
    return a, u


def _in_lru_kernel(x_ref, w_ref, cos_ref, sin_ref, cw_ref, cb_ref, wg_ref, ba_ref, bi_ref,
                   lam_ref, h0_ref, c0_ref,
                   q_ref, k32_ref, k16_ref, v32_ref, v16_ref, ol_ref, hl_ref, cl_ref,
                   x16_scr, xp_scr, xc32_scr, xc16_scr, gate_scr, g_scr, h_scr, decay_scr,
                   *, tm, seq_rows, tiles_per_seq, aw, lw):
    carried = seq_rows == tm
    nseq = tm // seq_rows
    gps = seq_rows // SUBLANES
    i = pl.program_id(0)

    x16_scr[...] = x_ref[...].astype(BF16)

    xbp = _dot(x16_scr[...], w_ref[:, 3 * aw:3 * aw + lw])
    g_scr[...] = _dot(x16_scr[...], w_ref[:, 3 * aw + lw:3 * aw + 2 * lw])

    if carried:
        @pl.when(i % tiles_per_seq == 0)
        def _():
            xp_scr[0, 0:SUBLANES, :] = jnp.zeros((SUBLANES, lw), F32)
            h_scr[...] = jnp.zeros_like(h_scr)
    else:
        for sq in range(nseq):
            xp_scr[sq, SUBLANES - (CONV_WIDTH - 1):SUBLANES, :] = c0_ref[sq]
    for sq in range(nseq):
        xp_scr[sq, SUBLANES:, :] = xbp[sq * seq_rows:(sq + 1) * seq_rows, :]

    cw = cw_ref[...]
    cb = cb_ref[...]
    for sq in range(nseq):
        for gg in range(gps):
            r0 = SUBLANES + gg * SUBLANES
            xc = cb + cw[CONV_WIDTH - 1:CONV_WIDTH, :] * xp_scr[sq, r0:r0 + SUBLANES, :]
            for j in range(CONV_WIDTH - 1):
                d = CONV_WIDTH - 1 - j
                xc = xc + cw[j:j + 1, :] * xp_scr[sq, r0 - d:r0 - d + SUBLANES, :]
            g0 = (sq * gps + gg) * SUBLANES
            xc32_scr[g0:g0 + SUBLANES, :] = xc
            xc16_scr[g0:g0 + SUBLANES, :] = xc.astype(BF16)

    for sq in range(nseq):
        cl_ref[sq] = xp_scr[sq, SUBLANES + seq_rows - (CONV_WIDTH - 1):SUBLANES + seq_rows, :]
    if carried:
        xp_scr[0, 0:SUBLANES, :] = xp_scr[0, seq_rows:seq_rows + SUBLANES, :]

    for blk in range(lw // MXU_DIM):
        sl = slice(blk * MXU_DIM, (blk + 1) * MXU_DIM)
        gate_scr[:, 2 * blk * MXU_DIM:2 * (blk + 1) * MXU_DIM] = _dot(xc16_scr[:, sl], wg_ref[blk])

    def rope(xs):
        lane = lax.broadcasted_iota(jnp.int32, (tm, LANES), 1)
        first_half = (lane % HALF_DIM) < (HALF_DIM // 2)
        rot = jnp.where(first_half, pltpu.roll(xs, LANES - HALF_DIM // 2, axis=1),
                        pltpu.roll(xs, HALF_DIM // 2, axis=1))
        return xs * cos_ref[...] + rot * sin_ref[...]

    pieces_per_seg = aw // MXU_DIM
    n_pieces = 3 * pieces_per_seg

    def qkv_piece(p):
        seg, c0 = p // pieces_per_seg, (p % pieces_per_seg) * MXU_DIM
        pre = _dot(x16_scr[...], w_ref[:, seg * aw + c0:seg * aw + c0 + MXU_DIM])
        for h in range(MXU_DIM // LANES):
            sl = slice(c0 + h * LANES, c0 + (h + 1) * LANES)
            ph = pre[:, h * LANES:(h + 1) * LANES]
            if seg == 0:
                q_ref[:, sl] = (rope(ph) * Q_SCALE).astype(BF16)
            elif seg == 1:
                kr = rope(ph)
                k32_ref[:, sl] = kr
                k16_ref[:, sl] = kr.astype(BF16)
            else:
                v32_ref[:, sl] = ph
                v16_ref[:, sl] = ph.astype(BF16)

    nl = -lam_ref[...]
    softplus = jnp.maximum(nl, 0.0) + jnp.log1p(jnp.exp(-jnp.abs(nl)))
    decay_scr[...] = -LRU_C * softplus

    def sigmoid(z):
        return 0.5 * jnp.tanh(0.5 * z) + 0.5

    def lru_group(sq, gg):
        g0 = (sq * gps + gg) * SUBLANES
        rows = slice(g0, g0 + SUBLANES)
        hin = h0_ref[sq] if (gg == 0 and not carried) else h_scr[...]
        pre = gate_scr[rows, :]
        ra = jnp.concatenate(
            [pre[:, 2 * b * MXU_DIM:(2 * b + 1) * MXU_DIM] for b in range(lw // MXU_DIM)], axis=1)
        ri = jnp.concatenate(
            [pre[:, (2 * b + 1) * MXU_DIM:(2 * b + 2) * MXU_DIM] for b in range(lw // MXU_DIM)], axis=1)
        r = sigmoid(ra + ba_ref[...])
        ig = sigmoid(ri + bi_ref[...])
        log_a = decay_scr[...] * r
        a = jnp.exp(log_a)
        y = (1.0 + a * a) * jnp.tanh(-log_a)
        root = jnp.where(y > 0.0, y * lax.rsqrt(y), 0.0)
        u = root * (ig * xc32_scr[rows, :])
        sub = lax.broadcasted_iota(jnp.int32, (SUBLANES, lw), 0)
        acum, ucum = _group_scan(a, u, sub)
        hs = acum * hin + ucum
        h_scr[...] = hs[SUBLANES - 1:SUBLANES, :]
        ol_ref[rows, :] = (hs * jax.nn.gelu(g_scr[rows, :])).astype(BF16)
        if gg == gps - 1:
            hl_ref[sq] = hs[SUBLANES - 1:SUBLANES, :]

    n_groups = nseq * gps
    for p in range(n_pieces):
        @pl.when(i >= 0)
        def _(p=p):
            qkv_piece(p)
            for g in range(p * n_groups // n_pieces, (p + 1) * n_groups // n_pieces):
                lru_group(g // gps, g % gps)


def _in_lru_call(x2d, w_in16, cos, sin, conv_w, conv_b, wg, b_a, b_i, lru_lambda, h0, c0,
                 *, tm, seq_rows, rows_per_batch):
    rows, d_model = x2d.shape
    aw = (w_in16.shape[1] - 2 * conv_w.shape[1]) // 3
    lw = conv_w.shape[1]
    carried = seq_rows == tm
    n_tiles = rows // tm
    tiles_per_seq = rows_per_batch // tm if carried else 1
    nseq_tile = tm // seq_rows
    nseq_total = rows // rows_per_batch
    if carried:
        state_map = lambda i: (i // tiles_per_seq, 0, 0)
    else:
        state_map = lambda i: (i, 0, 0)
    const2 = lambda i: (0, 0)
    const3 = lambda i: (0, 0, 0)
    row_map = lambda i: (i, 0)
    resident = dict(pipeline_mode=pl.Buffered(1))
    in_specs = [
        pl.BlockSpec((tm, d_model), row_map),
        pl.BlockSpec(w_in16.shape, const2, **resident),
        pl.BlockSpec((tm, LANES), (lambda i: (i % tiles_per_seq, 0)) if carried else const2),
        pl.BlockSpec((tm, LANES), (lambda i: (i % tiles_per_seq, 0)) if carried else const2),
        pl.BlockSpec(conv_w.shape, const2),
        pl.BlockSpec(conv_b.shape, const2),
        pl.BlockSpec(wg.shape, const3, **resident),
        pl.BlockSpec(b_a.shape, const2),
        pl.BlockSpec(b_i.shape, const2),
        pl.BlockSpec(lru_lambda.shape, const2),
        pl.BlockSpec((nseq_tile, 1, lw), state_map),
        pl.BlockSpec((nseq_tile, CONV_WIDTH - 1, lw), state_map),
    ]
    out_shape = (
        jax.ShapeDtypeStruct((rows, aw), BF16),
        jax.ShapeDtypeStruct((rows, aw), F32),
        jax.ShapeDtypeStruct((rows, aw), BF16),
        jax.ShapeDtypeStruct((rows, aw), F32),
        jax.ShapeDtypeStruct((rows, aw), BF16),
        jax.ShapeDtypeStruct((rows, lw), BF16),
        jax.ShapeDtypeStruct((nseq_total, 1, lw), F32),
        jax.ShapeDtypeStruct((nseq_total, CONV_WIDTH - 1, lw), F32),
    )
    out_specs = (
        pl.BlockSpec((tm, aw), row_map),
        pl.BlockSpec((tm, aw), row_map),
        pl.BlockSpec((tm, aw), row_map),
        pl.BlockSpec((tm, aw), row_map),
        pl.BlockSpec((tm, aw), row_map),
        pl.BlockSpec((tm, lw), row_map),
        pl.BlockSpec((nseq_tile, 1, lw), state_map),
        pl.BlockSpec((nseq_tile, CONV_WIDTH - 1, lw), state_map),
    )
    scratch = [
        pltpu.VMEM((tm, d_model), BF16),
        pltpu.VMEM((nseq_tile, SUBLANES + seq_rows, lw), F32),
        pltpu.VMEM((tm, lw), F32),
        pltpu.VMEM((tm, lw), BF16),
        pltpu.VMEM((tm, 2 * lw), F32),
        pltpu.VMEM((tm, lw), F32),
        pltpu.VMEM((1, lw), F32),
        pltpu.VMEM((1, lw), F32),
    ]
    kern = functools.partial(_in_lru_kernel, tm=tm, seq_rows=seq_rows, tiles_per_seq=tiles_per_seq,
                             aw=aw, lw=lw)
    return pl.pallas_call(
        kern, grid=(n_tiles,), in_specs=in_specs, out_specs=out_specs, out_shape=out_shape,
        scratch_shapes=scratch,
        compiler_params=pltpu.CompilerParams(dimension_semantics=("arbitrary",),
                                             vmem_limit_bytes=VMEM_LIMIT_BYTES),
        name="in_lru",
    )(x2d, w_in16, cos, sin, conv_w, conv_b, wg, b_a, b_i, lru_lambda, h0, c0)


def _attn_kernel(lam_ref, g_ref, bias_ref, q_ref, k_ref, vt_ref, o_ref, s_scr, m_scr, acc_scr,
                 *, t, lam_init):
    i = pl.program_id(2)
    q = q_ref[...]
    lane = lax.broadcasted_iota(jnp.int32, (t, HEAD_DIM), 1)
    zero = jnp.zeros_like(q)
    qq = jnp.concatenate([jnp.where(lane < HALF_DIM, q, zero),
                          jnp.where(lane >= HALF_DIM, q, zero)], axis=0)

    def scores(j, slot):
        kj = k_ref[pl.ds(pl.multiple_of(j * t, t), t), :]
        s_scr[slot] = _dot_nt(kj, qq)

    def softmax_pv(j, slot, masked):
        s = s_scr[slot]
        if masked:
            s = s + bias_ref[...]
        m = m_scr[...]
        m_new = jnp.maximum(m, jnp.max(s, axis=0, keepdims=True))
        alpha = jnp.exp2(m - m_new)
        p = jnp.exp2(s - m_new)
        m_scr[...] = m_new
        acc_scr[...] = alpha * acc_scr[...] + _dot(vt_ref[j], p.astype(BF16))

    m_scr[...] = jnp.full(m_scr.shape, NEG_INF, F32)
    acc_scr[...] = jnp.zeros(acc_scr.shape, F32)

    scores(0, 0)

    def pair(jj, _):
        scores(2 * jj + 1, 1)
        softmax_pv(2 * jj, 0, False)
        scores(2 * jj + 2, 0)
        softmax_pv(2 * jj + 1, 1, False)
        return 0

    lax.fori_loop(0, i // 2, pair, 0)

    @pl.when(i % 2 == 0)
    def _():
        softmax_pv(i, 0, True)

    @pl.when(i % 2 == 1)
    def _():
        scores(i, 1)
        softmax_pv(i - 1, 0, False)
        softmax_pv(i, 1, True)

    lam = lam_ref[0:1, 0:1]
    acc = acc_scr[...]
    o = acc[0:HEAD_DIM] / acc[HEAD_DIM:HEAD_DIM + 1]
    o = o[:, 0:t] - lam * o[:, t:2 * t]
    ms = jnp.mean(o * o, axis=0, keepdims=True)
    o = o * lax.rsqrt(ms + RMS_EPS) * g_ref[...] * (1.0 - lam_init)
    o_ref[...] = o.T.astype(BF16)


def _attn_call(lam_tile, subln_g, q16, k16, v16, *, t, lam_init):
    b, s, aw = q16.shape
    nh = aw // HEAD_DIM
    vt = v16.reshape(b, s // t, t, nh, HEAD_DIM).transpose(0, 3, 1, 4, 2)
    vt = jnp.concatenate([vt, jnp.ones((b, nh, s // t, ONES_ROWS, t), BF16)], axis=3)
    key_chunk = jnp.arange(t, dtype=jnp.int32)[:, None] // CHUNK
    qry_chunk = (jnp.arange(2 * t, dtype=jnp.int32)[None, :] % t) // CHUNK
    bias = jnp.where(key_chunk <= qry_chunk, 0.0, NEG_INF).astype(F32)
    tile_spec = pl.BlockSpec((None, t, HEAD_DIM), lambda bi, hi, qi: (bi, qi, hi))
    const = lambda bi, hi, qi: (0, 0)
    return pl.pallas_call(
        functools.partial(_attn_kernel, t=t, lam_init=lam_init),
        grid=(b, nh, s // t),
        in_specs=[pl.BlockSpec((SUBLANES, LANES), const),
                  pl.BlockSpec((HEAD_DIM, 1), const),
                  pl.BlockSpec((t, 2 * t), const),
                  tile_spec,
                  pl.BlockSpec((None, s, HEAD_DIM), lambda bi, hi, qi: (bi, 0, hi)),
                  pl.BlockSpec((None, None, s // t, HEAD_DIM + ONES_ROWS, t),
                               lambda bi, hi, qi: (bi, hi, 0, 0, 0))],
        out_specs=tile_spec,
        out_shape=jax.ShapeDtypeStruct((b, s, aw), BF16),
        scratch_shapes=[pltpu.VMEM((2, t, 2 * t), F32),
                        pltpu.VMEM((1, 2 * t), F32),
                        pltpu.VMEM((HEAD_DIM + ONES_ROWS, 2 * t), F32)],
        compiler_params=pltpu.CompilerParams(
            dimension_semantics=("arbitrary", "arbitrary", "arbitrary"),
            vmem_limit_bytes=VMEM_LIMIT_BYTES),
        name="attn",
    )(lam_tile, subln_g.reshape(HEAD_DIM, 1), bias, q16, k16, vt)


def _dec_attn_kernel(lam_ref, g_ref, q_ref, kn_ref, vn_ref, ck_ref, cv_ref, o_ref,
                     *, t, nh, kc, lam_init):
    rows = 2 * nh * t
    p_len = ck_ref.shape[0]
    q = q_ref[...]
    lane = lax.broadcasted_iota(jnp.int32, (t, HEAD_DIM), 1)
    parts = []
    for c in range(2):
        keep = (lane < HALF_DIM) if c == 0 else (lane >= HALF_DIM)
        for h in range(nh):
            qh = q[:, h * HEAD_DIM:(h + 1) * HEAD_DIM]
            parts.append(jnp.where(keep, qh, jnp.zeros_like(qh)))
    qall = jnp.concatenate(parts, axis=0)

    def scores(kf):
        s = _dot_nt(qall, kf.astype(BF16))
        row_head = (lax.broadcasted_iota(jnp.int32, s.shape, 0) // t) % nh
        col_head = lax.broadcasted_iota(jnp.int32, s.shape, 1) % nh
        return jnp.where(row_head == col_head, s, NEG_INF)

    def update(carry, s, vf):
        m, l, acc = carry
        m_new = jnp.maximum(m, jnp.max(s, axis=-1, keepdims=True))
        alpha = jnp.exp2(m - m_new)
        p = jnp.exp2(s - m_new)
        l = alpha * l + jnp.sum(p, axis=-1, keepdims=True)
        acc = alpha * acc + _dot(p.astype(BF16), vf.astype(BF16))
        return m_new, l, acc

    carry = (jnp.full((rows, 1), NEG_INF, F32), jnp.zeros((rows, 1), F32),
             jnp.zeros((rows, HEAD_DIM), F32))
    for j in range(p_len // kc):
        kf = ck_ref[j * kc:(j + 1) * kc].reshape(kc * nh, HEAD_DIM)
        vf = cv_ref[j * kc:(j + 1) * kc].reshape(kc * nh, HEAD_DIM)
        carry = update(carry, scores(kf), vf)
    kf = kn_ref[...].reshape(t * nh, HEAD_DIM)
    vf = vn_ref[...].reshape(t * nh, HEAD_DIM)
    _, l, acc = update(carry, scores(kf), vf)

    o = acc / l
    half = rows // 2
    o = o[0:half] - lam_ref[0:1, 0:1] * o[half:rows]
    ms = jnp.mean(o * o, axis=-1, keepdims=True)
    o = o * lax.rsqrt(ms + RMS_EPS) * g_ref[...] * (1.0 - lam_init)
    for h in range(nh):
        o_ref[:, h * HEAD_DIM:(h + 1) * HEAD_DIM] = o[h * t:(h + 1) * t].astype(BF16)


def _dec_attn_call(lam_tile, subln_g, q16, kn32, vn32, cache_k, cache_v, *, layer, lam_init):
    bd, t, aw = q16.shape
    _, p, nh, hd = cache_k.shape
    new_spec = pl.BlockSpec((None, t, nh, hd), lambda b: (b, 0, 0, 0))
    cache_spec = pl.BlockSpec((None, p, nh, hd), lambda b: (layer * bd + b, 0, 0, 0))
    io_spec = pl.BlockSpec((None, t, aw), lambda b: (b, 0, 0))
    return pl.pallas_call(
        functools.partial(_dec_attn_kernel, t=t, nh=nh, kc=_pick_tile(p, 256), lam_init=lam_init),
        grid=(bd,),
        in_specs=[pl.BlockSpec((SUBLANES, LANES), lambda b: (0, 0)),
                  pl.BlockSpec((1, HEAD_DIM), lambda b: (0, 0)),
                  io_spec, new_spec, new_spec, cache_spec, cache_spec],
        out_specs=io_spec,
        out_shape=jax.ShapeDtypeStruct((bd, t, aw), BF16),
        compiler_params=pltpu.CompilerParams(dimension_semantics=("arbitrary",),
                                             vmem_limit_bytes=VMEM_LIMIT_BYTES),
        name="dec_attn",
    )(lam_tile, subln_g, q16, kn32, vn32, cache_k, cache_v)


def _out_mlp_kernel(x_ref, oa_ref, ol_ref, wo_ref, g1_ref, b1_ref, wu_ref, wd_ref, g2_ref, b2_ref,
                    y_ref, x1_scr, *, alpha):
    j = pl.program_id(1)

    @pl.when(j == 0)
    def _():
        cat = jnp.concatenate([oa_ref[...], ol_ref[...]], axis=1)
        x1 = _layer_norm(alpha * x_ref[...] + _dot(cat, wo_ref[...]), g1_ref[...], b1_ref[...])
        x1_scr[...] = x1.astype(BF16)
        y_ref[...] = alpha * x1

    hid = jnp.square(jnp.maximum(_dot(x1_scr[...], wu_ref[...]), 0.0)).astype(BF16)
    y_ref[...] += _dot(hid, wd_ref[...])

    @pl.when(j == pl.num_programs(1) - 1)
    def _():
        y_ref[...] = _layer_norm(y_ref[...], g2_ref[...], b2_ref[...])


def _out_mlp_call(x2d, oa16, ol16, w_out16, g1, b1, w_up16, w_down16, g2, b2, *, tm, tf, alpha):
    rows, d = x2d.shape
    aw = oa16.shape[1]
    lw = ol16.shape[1]
    d_ff = w_up16.shape[1]
    row = lambda i, j: (i, 0)
    const = lambda i, j: (0, 0)
    return pl.pallas_call(
        functools.partial(_out_mlp_kernel, alpha=alpha),
        grid=(rows // tm, d_ff // tf),
        in_specs=[pl.BlockSpec((tm, d), row),
                  pl.BlockSpec((tm, aw), row),
                  pl.BlockSpec((tm, lw), row),
                  pl.BlockSpec(w_out16.shape, const, pipeline_mode=pl.Buffered(1)),
                  pl.BlockSpec((1, d), const), pl.BlockSpec((1, d), const),
                  pl.BlockSpec((d, tf), lambda i, j: (0, j)),
                  pl.BlockSpec((tf, d), lambda i, j: (j, 0)),
                  pl.BlockSpec((1, d), const), pl.BlockSpec((1, d), const)],
        out_specs=pl.BlockSpec((tm, d), row),
        out_shape=jax.ShapeDtypeStruct((rows, d), F32),
        scratch_shapes=[pltpu.VMEM((tm, d), BF16)],
        compiler_params=pltpu.CompilerParams(dimension_semantics=("arbitrary", "arbitrary"),
                                             vmem_limit_bytes=VMEM_LIMIT_BYTES),
        name="out_mlp",
    )(x2d, oa16, ol16, w_out16, g1, b1, w_up16, w_down16, g2, b2)


def _rope_tables(pos):
    inv = ROPE_THETA ** (-jnp.arange(0, HALF_DIM, 2, dtype=F32) / HALF_DIM)
    ang = pos.astype(F32)[:, None] * inv[None, :]
    cos = jnp.tile(jnp.cos(ang), (1, 4))
    sin = jnp.sin(ang)
    sin = jnp.tile(jnp.concatenate([-sin, sin], axis=-1), (1, 2))
    return cos, sin


def _gate_weights(w_a, w_i):
    per = MXU_DIM // w_a.shape[-1]

    def dense(w):
        nb, bd, _ = w.shape
        w4 = w.reshape(nb // per, per, bd, bd)
        eye = jnp.eye(per, dtype=w.dtype)
        return jnp.einsum('gpij,pq->gpiqj', w4, eye).reshape(nb // per, per * bd, per * bd)

    return jnp.concatenate([dense(w_a), dense(w_i)], axis=-1).astype(BF16)


def _pick_tile(n, pref):
    t = min(n, pref)
    while n % t:
        t //= 2
    return t


def kernel(x_prompt, x_sample, cache_k, cache_v, state_h, state_conv, w_in, lambda_q1, lambda_k1, lambda_q2, lambda_k2, subln_g, conv_w, conv_b, w_rg_a, b_rg_a, w_rg_i, b_rg_i, lru_lambda, w_out, ln1_g, ln1_b, w_up, w_down, ln2_g, ln2_b):
    depth = w_in.shape[0]
    b, s, d = x_prompt.shape
    bd, t, _ = x_sample.shape
    p = cache_k.shape[2]
    lw = conv_w.shape[-1]
    aw = (w_in.shape[-1] - 2 * lw) // 3
    nh = aw // HEAD_DIM
    alpha = (2.0 * depth) ** 0.25

    cos_p, sin_p = _rope_tables(jnp.arange(s, dtype=jnp.int32))
    cos_s, sin_s = _rope_tables(p + jnp.arange(t, dtype=jnp.int32))
    cos_s = jnp.tile(cos_s, (bd, 1))
    sin_s = jnp.tile(sin_s, (bd, 1))

    tm_a = _pick_tile(s, 256)
    tq = _pick_tile(s, 512)
    tm_c = _pick_tile(b * s, 512)
    tf = _pick_tile(w_up.shape[-1], 1024)

    yp = x_prompt.reshape(b * s, d)
    ys = x_sample.reshape(bd * t, d)
    outs = [[] for _ in range(8)]
    for l in range(depth):
        lam0 = _lambda_init(l)
        w_in16 = w_in[l].astype(BF16)
        w_out16 = w_out[l].astype(BF16)
        w_up16 = w_up[l].astype(BF16)
        w_down16 = w_down[l].astype(BF16)
        wg = _gate_weights(w_rg_a[l], w_rg_i[l])
        row = lambda v: v[l].reshape(1, -1)
        lam_tile = _lam_call(row(lambda_q1), row(lambda_k1), row(lambda_q2), row(lambda_k2), lam0)
        g_sub = row(subln_g)
        lru_args = (conv_w[l], row(conv_b), wg, row(b_rg_a), row(b_rg_i), row(lru_lambda))
        ln_args = (w_out16, row(ln1_g), row(ln1_b), w_up16, w_down16, row(ln2_g), row(ln2_b))

        q16, k32, k16, v32, v16, ol16, hp, cp = _in_lru_call(
            yp, w_in16, cos_p, sin_p, *lru_args,
            jnp.zeros((b, 1, lw), F32), jnp.zeros((b, CONV_WIDTH - 1, lw), F32),
            tm=tm_a, seq_rows=tm_a, rows_per_batch=s)
        oa16 = _attn_call(lam_tile, g_sub, q16.reshape(b, s, aw), k16.reshape(b, s, aw),
                          v16.reshape(b, s, aw), t=tq, lam_init=lam0)
        yp = _out_mlp_call(yp, oa16.reshape(b * s, aw), ol16, *ln_args, tm=tm_c, tf=tf, alpha=alpha)
        outs[0].append(k32.reshape(b, s, nh, HEAD_DIM))
        outs[1].append(v32.reshape(b, s, nh, HEAD_DIM))
        outs[2].append(hp.reshape(b, lw))
        outs[3].append(cp)

        q16, k32, k16, v32, v16, ol16, hn, cn = _in_lru_call(
            ys, w_in16, cos_s, sin_s, *lru_args,
            state_h[l].reshape(bd, 1, lw), state_conv[l],
            tm=bd * t, seq_rows=t, rows_per_batch=t)
        oa16 = _dec_attn_call(lam_tile, g_sub, q16.reshape(bd, t, aw),
                              k32.reshape(bd, t, nh, HEAD_DIM), v32.reshape(bd, t, nh, HEAD_DIM),
                              cache_k.reshape(depth * bd, p, nh, HEAD_DIM),
                              cache_v.reshape(depth * bd, p, nh, HEAD_DIM), layer=l, lam_init=lam0)
        ys = _out_mlp_call(ys, oa16.reshape(bd * t, aw), ol16, *ln_args,
                           tm=_pick_tile(bd * t, 512), tf=tf, alpha=alpha)
        outs[4].append(k32.reshape(bd, t, nh, HEAD_DIM))
        outs[5].append(v32.reshape(bd, t, nh, HEAD_DIM))
        outs[6].append(hn.reshape(bd, lw))
        outs[7].append(cn)

    stk = [jnp.stack(o) for o in outs]
    return (yp.reshape(b, s, d), ys.reshape(bd, t, d), stk[0], stk[1], stk[2], stk[3],
            stk[4], stk[5], stk[6], stk[7])
```

```python
import functools
import math

import jax
import jax.numpy as jnp
from jax import lax
from jax.experimental import pallas as pl
from jax.experimental.pallas import tpu as pltpu

CHUNK = 64
HEAD_DIM = 128
HALF_DIM = HEAD_DIM // 2
LRU_BLOCKS = 16
CONV_WIDTH = 4
LRU_C = 8.0
ROPE_THETA = 10000.0
LN_EPS = 1e-5
RMS_EPS = 1e-5
NEG_INF = -1e30

SUBLANES = 8
LANES = 128
MXU_DIM = 256
ONES_ROWS = 16
VMEM_LIMIT_BYTES = 56 * 1024 * 1024

Q_SCALE = HALF_DIM ** -0.5 * math.log2(math.e)

BF16 = jnp.bfloat16
F32 = jnp.float32


def _lambda_init(layer):
    return 0.8 - 0.6 * math.exp(-0.3 * layer)


def _dot(a, b):
    return jnp.dot(a, b, preferred_element_type=F32)


def _dot_nt(a, b):
    return lax.dot_general(a, b, (((1,), (1,)), ((), ())), preferred_element_type=F32)


def _layer_norm(z, g, b):
    mu = jnp.mean(z, axis=-1, keepdims=True)
    d = z - mu
    var = jnp.mean(d * d, axis=-1, keepdims=True)
    return d * lax.rsqrt(var + LN_EPS) * g + b


def _lam_kernel(q1_ref, k1_ref, q2_ref, k2_ref, o_ref, *, lam_init):
    s1 = jnp.sum(q1_ref[...] * k1_ref[...], axis=-1, keepdims=True)
    s2 = jnp.sum(q2_ref[...] * k2_ref[...], axis=-1, keepdims=True)
    lam = jnp.exp(s1) - jnp.exp(s2) + lam_init
    o_ref[...] = jnp.broadcast_to(lam, o_ref.shape)


def _lam_call(lq1, lk1, lq2, lk2, lam_init):
    return pl.pallas_call(
        functools.partial(_lam_kernel, lam_init=lam_init),
        out_shape=jax.ShapeDtypeStruct((SUBLANES, LANES), F32),
        name="lam",
    )(lq1, lk1, lq2, lk2)


def _group_scan(a, u, sub):
    for s in (1, 2, 4):
        keep = sub >= s
        a_sh = pltpu.roll(a, s, axis=0)
        u_sh = pltpu.roll(u, s, axis=0)
        u = u + jnp.where(keep, a * u_sh, 0.0)
        a = a * jnp.where(keep, a_sh, 1.0)
    return a, u


def _in_lru_kernel(x_ref, w_ref, cos_ref, sin_ref, cw_ref, cb_ref, wg_ref, ba_ref, bi_ref,
                   lam_ref, h0_ref, c0_ref,
                   q_ref, k32_ref, k16_ref, v32_ref, v16_ref, ol_ref, hl_ref, cl_ref,
                   xp_scr, xc32_scr, xc16_scr, gate_scr, g_scr, h_scr,
                   *, tm, seq_rows, tiles_per_seq, aw, lw):
    carried = seq_rows == tm
    nseq = tm // seq_rows
    gps = seq_rows // SUBLANES
    i = pl.program_id(0)

    xb16 = x_ref[...].astype(BF16)

    cos = cos_ref[...]
    sin = sin_ref[...]
    lane = lax.broadcasted_iota(jnp.int32, (tm, LANES), 1)
    first_half = (lane % HALF_DIM) < (HALF_DIM // 2)

    def rope(xs):
        rot = jnp.where(first_half, pltpu.roll(xs, LANES - HALF_DIM // 2, axis=1),
                        pltpu.roll(xs, HALF_DIM // 2, axis=1))
        return xs * cos + rot * sin

    pieces_per_seg = aw // MXU_DIM
    n_pieces = 3 * pieces_per_seg

    def qkv_piece(p):
        seg, c0 = p // pieces_per_seg, (p % pieces_per_seg) * MXU_DIM
        pre = _dot(xb16, w_ref[:, seg * aw + c0:seg * aw + c0 + MXU_DIM])
        for h in range(MXU_DIM // LANES):
            sl = slice(c0 + h * LANES, c0 + (h + 1) * LANES)
            ph = pre[:, h * LANES:(h + 1) * LANES]
            if seg == 0:
                q_ref[:, sl] = (rope(ph) * Q_SCALE).astype(BF16)
            elif seg == 1:
                kr = rope(ph)
                k32_ref[:, sl] = kr
                k16_ref[:, sl] = kr.astype(BF16)
            else:
                v32_ref[:, sl] = ph
                v16_ref[:, sl] = ph.astype(BF16)

    xbp = _dot(xb16, w_ref[:, 3 * aw:3 * aw + lw])
    g_scr[...] = _dot(xb16, w_ref[:, 3 * aw + lw:3 * aw + 2 * lw])

    if carried:
        @pl.when(i % tiles_per_seq == 0)
        def _():
            xp_scr[0, 0:SUBLANES, :] = jnp.zeros((SUBLANES, lw), F32)
            h_scr[...] = jnp.zeros_like(h_scr)
    else:
        for sq in range(nseq):
            xp_scr[sq, SUBLANES - (CONV_WIDTH - 1):SUBLANES, :] = c0_ref[sq]
    for sq in range(nseq):
        xp_scr[sq, SUBLANES:, :] = xbp[sq * seq_rows:(sq + 1) * seq_rows, :]

    sub = lax.broadcasted_iota(jnp.int32, (SUBLANES, lw), 0)
    cwb = [jnp.broadcast_to(cw_ref[j:j + 1, :], (SUBLANES, lw)) for j in range(CONV_WIDTH)]
    cbb = jnp.broadcast_to(cb_ref[...], (SUBLANES, lw))
    for sq in range(nseq):
        prev = xp_scr[sq, 0:SUBLANES, :]
        prev_rolled = [pltpu.roll(prev, d, axis=0) for d in range(1, CONV_WIDTH)]
        for gg in range(gps):
            r0 = SUBLANES + gg * SUBLANES
            cur = xp_scr[sq, r0:r0 + SUBLANES, :]
            cur_rolled = [pltpu.roll(cur, d, axis=0) for d in range(1, CONV_WIDTH)]
            xc = cbb + cwb[CONV_WIDTH - 1] * cur
            for d in range(1, CONV_WIDTH):
                shifted = jnp.where(sub >= d, cur_rolled[d - 1], prev_rolled[d - 1])
                xc = xc + cwb[CONV_WIDTH - 1 - d] * shifted
            prev_rolled = cur_rolled
            g0 = (sq * gps + gg) * SUBLANES
            xc32_scr[g0:g0 + SUBLANES, :] = xc
            xc16_scr[g0:g0 + SUBLANES, :] = xc.astype(BF16)

    for sq in range(nseq):
        cl_ref[sq] = xp_scr[sq, SUBLANES + seq_rows - (CONV_WIDTH - 1):SUBLANES + seq_rows, :]
    if carried:
        xp_scr[0, 0:SUBLANES, :] = xp_scr[0, seq_rows:seq_rows + SUBLANES, :]

    for blk in range(lw // MXU_DIM):
        sl = slice(blk * MXU_DIM, (blk + 1) * MXU_DIM)
        gate_scr[:, 2 * blk * MXU_DIM:2 * (blk + 1) * MXU_DIM] = _dot(xc16_scr[:, sl], wg_ref[blk])

    nl = -lam_ref[...]
    softplus = jnp.maximum(nl, 0.0) + jnp.log1p(jnp.exp(-jnp.abs(nl)))
    decay = jnp.broadcast_to(-LRU_C * softplus, (SUBLANES, lw))
    ba = jnp.broadcast_to(ba_ref[...], (SUBLANES, lw))
    bi = jnp.broadcast_to(bi_ref[...], (SUBLANES, lw))

    def sigmoid(z):
        return 0.5 * jnp.tanh(0.5 * z) + 0.5

    n_groups = nseq * gps
    pieces_done = 0
    after = None
    for sq in range(nseq):
        hin = h_scr[...] if carried else h0_ref[sq]
        for gg in range(gps):
            while pieces_done * n_groups < (sq * gps + gg) * n_pieces:
                qkv_piece(pieces_done)
                pieces_done += 1
            g0 = (sq * gps + gg) * SUBLANES
            rows = slice(g0, g0 + SUBLANES)
            pre = gate_scr[rows, :]
            ra = jnp.concatenate(
                [pre[:, 2 * b * MXU_DIM:(2 * b + 1) * MXU_DIM] for b in range(lw // MXU_DIM)], axis=1)
            ri = jnp.concatenate(
                [pre[:, (2 * b + 1) * MXU_DIM:(2 * b + 2) * MXU_DIM] for b in range(lw // MXU_DIM)], axis=1)
            if after is not None:
                ra = ra + after
                ri = ri + after
            r = sigmoid(ra + ba)
            ig = sigmoid(ri + bi)
            log_a = decay * r
            a = jnp.exp(log_a)
            y = (1.0 + a * a) * jnp.tanh(-log_a)
            root = jnp.where(y > 0.0, y * lax.rsqrt(y), 0.0)
            u = root * (ig * xc32_scr[rows, :])
            acum, ucum = _group_scan(a, u, sub)
            hs = acum * hin + ucum
            hin = hs[SUBLANES - 1:SUBLANES, :]
            out = hs * jax.nn.gelu(g_scr[rows, :])
            ol_ref[rows, :] = out.astype(BF16)
            after = pltpu.bitcast(
                lax.shift_right_logical(pltpu.bitcast(out, jnp.uint32), jnp.uint32(32)), F32)
        hl_ref[sq] = hin
        if carried:
            h_scr[...] = hin
    while pieces_done < n_pieces:
        qkv_piece(pieces_done)
        pieces_done += 1


def _in_lru_call(x2d, w_in16, cos, sin, conv_w, conv_b, wg, b_a, b_i, lru_lambda, h0, c0,
                 *, tm, seq_rows, rows_per_batch):
    rows, d_model = x2d.shape
    aw = (w_in16.shape[1] - 2 * conv_w.shape[1]) // 3
    lw = conv_w.shape[1]
    carried = seq_rows == tm
    n_tiles = rows // tm
    tiles_per_seq = rows_per_batch // tm if carried else 1
    nseq_tile = tm // seq_rows
    nseq_total = rows // rows_per_batch
    if carried:
        state_map = lambda i: (i // tiles_per_seq, 0, 0)
    else:
        state_map = lambda i: (i, 0, 0)
    const2 = lambda i: (0, 0)
    const3 = lambda i: (0, 0, 0)
    row_map = lambda i: (i, 0)
    resident = dict(pipeline_mode=pl.Buffered(1))
    in_specs = [
        pl.BlockSpec((tm, d_model), row_map),
        pl.BlockSpec(w_in16.shape, const2, **resident),
        pl.BlockSpec((tm, LANES), (lambda i: (i % tiles_per_seq, 0)) if carried else const2),
        pl.BlockSpec((tm, LANES), (lambda i: (i % tiles_per_seq, 0)) if carried else const2),
        pl.BlockSpec(conv_w.shape, const2),
        pl.BlockSpec(conv_b.shape, const2),
        pl.BlockSpec(wg.shape, const3, **resident),
        pl.BlockSpec(b_a.shape, const2),
        pl.BlockSpec(b_i.shape, const2),
        pl.BlockSpec(lru_lambda.shape, const2),
        pl.BlockSpec((nseq_tile, 1, lw), state_map),
        pl.BlockSpec((nseq_tile, CONV_WIDTH - 1, lw), state_map),
    ]
    out_shape = (
        jax.ShapeDtypeStruct((rows, aw), BF16),
        jax.ShapeDtypeStruct((rows, aw), F32),
        jax.ShapeDtypeStruct((rows, aw), BF16),
        jax.ShapeDtypeStruct((rows, aw), F32),
        jax.ShapeDtypeStruct((rows, aw), BF16),
        jax.ShapeDtypeStruct((rows, lw), BF16),
        jax.ShapeDtypeStruct((nseq_total, 1, lw), F32),
        jax.ShapeDtypeStruct((nseq_total, CONV_WIDTH - 1, lw), F32),
    )
    out_specs = (
        pl.BlockSpec((tm, aw), row_map),
        pl.BlockSpec((tm, aw), row_map),
        pl.BlockSpec((tm, aw), row_map),
        pl.BlockSpec((tm, aw), row_map),
        pl.BlockSpec((tm, aw), row_map),
        pl.BlockSpec((tm, lw), row_map),
        pl.BlockSpec((nseq_tile, 1, lw), state_map),
        pl.BlockSpec((nseq_tile, CONV_WIDTH - 1, lw), state_map),
    )
    scratch = [
        pltpu.VMEM((nseq_tile, SUBLANES + seq_rows, lw), F32),
        pltpu.VMEM((tm, lw), F32),
        pltpu.VMEM((tm, lw), BF16),
        pltpu.VMEM((tm, 2 * lw), F32),
        pltpu.VMEM((tm, lw), F32),
        pltpu.VMEM((1, lw), F32),
    ]
    kern = functools.partial(_in_lru_kernel, tm=tm, seq_rows=seq_rows, tiles_per_seq=tiles_per_seq,
                             aw=aw, lw=lw)
    return pl.pallas_call(
        kern, grid=(n_tiles,), in_specs=in_specs, out_specs=out_specs, out_shape=out_shape,
        scratch_shapes=scratch,
        compiler_params=pltpu.CompilerParams(dimension_semantics=("arbitrary",),
                                             vmem_limit_bytes=VMEM_LIMIT_BYTES),
        name="in_lru",
    )(x2d, w_in16, cos, sin, conv_w, conv_b, wg, b_a, b_i, lru_lambda, h0, c0)


def _attn_kernel(lam_ref, g_ref, bias_ref, q_ref, k_ref, vt_ref, o_ref, s_scr, m_scr, acc_scr,
                 *, t, lam_init):
    i = pl.program_id(2)
    q = q_ref[...]
    lane = lax.broadcasted_iota(jnp.int32, (t, HEAD_DIM), 1)
    zero = jnp.zeros_like(q)
    qq = jnp.concatenate([jnp.where(lane < HALF_DIM, q, zero),
                          jnp.where(lane >= HALF_DIM, q, zero)], axis=0)

    def scores(j, slot):
        kj = k_ref[pl.ds(pl.multiple_of(j * t, t), t), :]
        s_scr[slot] = _dot_nt(kj, qq)

    def softmax_pv(j, slot, masked):
        s = s_scr[slot]
        if masked:
            s = s + bias_ref[...]
        m = m_scr[...]
        m_new = jnp.maximum(m, jnp.max(s, axis=0, keepdims=True))
        alpha = jnp.exp2(m - m_new)
        p = jnp.exp2(s - m_new)
        m_scr[...] = m_new
        acc_scr[...] = alpha * acc_scr[...] + _dot(vt_ref[j], p.astype(BF16))

    m_scr[...] = jnp.full(m_scr.shape, NEG_INF, F32)
    acc_scr[...] = jnp.zeros(acc_scr.shape, F32)

    scores(0, 0)

    def pair(jj, _):
        scores(2 * jj + 1, 1)
        softmax_pv(2 * jj, 0, False)
        scores(2 * jj + 2, 0)
        softmax_pv(2 * jj + 1, 1, False)
        return 0

    lax.fori_loop(0, i // 2, pair, 0)

    @pl.when(i % 2 == 0)
    def _():
        softmax_pv(i, 0, True)

    @pl.when(i % 2 == 1)
    def _():
        scores(i, 1)
        softmax_pv(i - 1, 0, False)
        softmax_pv(i, 1, True)

    lam = lam_ref[0:1, 0:1]
    acc = acc_scr[...]
    o = acc[0:HEAD_DIM] / acc[HEAD_DIM:HEAD_DIM + 1]
    o = o[:, 0:t] - lam * o[:, t:2 * t]
    ms = jnp.mean(o * o, axis=0, keepdims=True)
    o = o * lax.rsqrt(ms + RMS_EPS) * g_ref[...] * (1.0 - lam_init)
    o_ref[...] = o.T.astype(BF16)


def _attn_call(lam_tile, subln_g, q16, k16, v16, *, t, lam_init):
    b, s, aw = q16.shape
    nh = aw // HEAD_DIM
    vt = v16.reshape(b, s // t, t, nh, HEAD_DIM).transpose(0, 3, 1, 4, 2)
    vt = jnp.concatenate([vt, jnp.ones((b, nh, s // t, ONES_ROWS, t), BF16)], axis=3)
    key_chunk = jnp.arange(t, dtype=jnp.int32)[:, None] // CHUNK
    qry_chunk = (jnp.arange(2 * t, dtype=jnp.int32)[None, :] % t) // CHUNK
    bias = jnp.where(key_chunk <= qry_chunk, 0.0, NEG_INF).astype(F32)
    tile_spec = pl.BlockSpec((None, t, HEAD_DIM), lambda bi, hi, qi: (bi, qi, hi))
    const = lambda bi, hi, qi: (0, 0)
    return pl.pallas_call(
        functools.partial(_attn_kernel, t=t, lam_init=lam_init),
        grid=(b, nh, s // t),
        in_specs=[pl.BlockSpec((SUBLANES, LANES), const),
                  pl.BlockSpec((HEAD_DIM, 1), const),
                  pl.BlockSpec((t, 2 * t), const),
                  tile_spec,
                  pl.BlockSpec((None, s, HEAD_DIM), lambda bi, hi, qi: (bi, 0, hi)),
                  pl.BlockSpec((None, None, s // t, HEAD_DIM + ONES_ROWS, t),
                               lambda bi, hi, qi: (bi, hi, 0, 0, 0))],
        out_specs=tile_spec,
        out_shape=jax.ShapeDtypeStruct((b, s, aw), BF16),
        scratch_shapes=[pltpu.VMEM((2, t, 2 * t), F32),
                        pltpu.VMEM((1, 2 * t), F32),
                        pltpu.VMEM((HEAD_DIM + ONES_ROWS, 2 * t), F32)],
        compiler_params=pltpu.CompilerParams(
            dimension_semantics=("arbitrary", "arbitrary", "arbitrary"),
            vmem_limit_bytes=VMEM_LIMIT_BYTES),
        name="attn",
    )(lam_tile, subln_g.reshape(HEAD_DIM, 1), bias, q16, k16, vt)


def _dec_attn_kernel(lam_ref, g_ref, bias_ref, q_ref, kn_ref, vn_ref, ck_ref, cv_ref, o_ref,
                     *, t, nh, kc, lam_init):
    rows = 2 * nh * t
    p_len = ck_ref.shape[0]
    q = q_ref[...]
    lane = lax.broadcasted_iota(jnp.int32, (t, HEAD_DIM), 1)
    parts = []
    for c in range(2):
        keep = (lane < HALF_DIM) if c == 0 else (lane >= HALF_DIM)
        for h in range(nh):
            qh = q[:, h * HEAD_DIM:(h + 1) * HEAD_DIM]
            parts.append(jnp.where(keep, qh, jnp.zeros_like(qh)))
    qall = jnp.concatenate(parts, axis=0)

    def scores(kf):
        s = _dot_nt(qall, kf.astype(BF16))
        return s + bias_ref[:, 0:s.shape[1]]

    def update(carry, s, vf):
        m, l, acc = carry
        m_new = jnp.maximum(m, jnp.max(s, axis=-1, keepdims=True))
        alpha = jnp.exp2(m - m_new)
        p = jnp.exp2(s - m_new)
        l = alpha * l + jnp.sum(p, axis=-1, keepdims=True)
        acc = alpha * acc + _dot(p.astype(BF16), vf.astype(BF16))
        return m_new, l, acc

    carry = (jnp.full((rows, 1), NEG_INF, F32), jnp.zeros((rows, 1), F32),
             jnp.zeros((rows, HEAD_DIM), F32))
    for j in range(p_len // kc):
        kf = ck_ref[j * kc:(j + 1) * kc].reshape(kc * nh, HEAD_DIM)
        vf = cv_ref[j * kc:(j + 1) * kc].reshape(kc * nh, HEAD_DIM)
        carry = update(carry, scores(kf), vf)
    kf = kn_ref[...].reshape(t * nh, HEAD_DIM)
    vf = vn_ref[...].reshape(t * nh, HEAD_DIM)
    _, l, acc = update(carry, scores(kf), vf)

    o = acc / l
    half = rows // 2
    o = o[0:half] - lam_ref[0:1, 0:1] * o[half:rows]
    ms = jnp.mean(o * o, axis=-1, keepdims=True)
    o = o * lax.rsqrt(ms + RMS_EPS) * g_ref[...] * (1.0 - lam_init)
    for h in range(nh):
        o_ref[:, h * HEAD_DIM:(h + 1) * HEAD_DIM] = o[h * t:(h + 1) * t].astype(BF16)


def _dec_attn_call(lam_tile, subln_g, q16, kn32, vn32, cache_k, cache_v, *, layer, lam_init):
    bd, t, aw = q16.shape
    _, p, nh, hd = cache_k.shape
    new_spec = pl.BlockSpec((None, t, nh, hd), lambda b: (b, 0, 0, 0))
    cache_spec = pl.BlockSpec((None, p, nh, hd), lambda b: (layer * bd + b, 0, 0, 0))
    io_spec = pl.BlockSpec((None, t, aw), lambda b: (b, 0, 0))
    kc = _pick_tile(p, 256)
    ncol = max(kc, t) * nh
    row_head = (jnp.arange(2 * nh * t, dtype=jnp.int32)[:, None] // t) % nh
    col_head = jnp.arange(ncol, dtype=jnp.int32)[None, :] % nh
    bias = jnp.where(row_head == col_head, 0.0, NEG_INF).astype(F32)
    return pl.pallas_call(
        functools.partial(_dec_attn_kernel, t=t, nh=nh, kc=kc, lam_init=lam_init),
        grid=(bd,),
        in_specs=[pl.BlockSpec((SUBLANES, LANES), lambda b: (0, 0)),
                  pl.BlockSpec((1, HEAD_DIM), lambda b: (0, 0)),
                  pl.BlockSpec(bias.shape, lambda b: (0, 0)),
                  io_spec, new_spec, new_spec, cache_spec, cache_spec],
        out_specs=io_spec,
        out_shape=jax.ShapeDtypeStruct((bd, t, aw), BF16),
        compiler_params=pltpu.CompilerParams(dimension_semantics=("arbitrary",),
                                             vmem_limit_bytes=VMEM_LIMIT_BYTES),
        name="dec_attn",
    )(lam_tile, subln_g, bias, q16, kn32, vn32, cache_k, cache_v)


def _out_mlp_kernel(x_ref, oa_ref, ol_ref, wo_ref, g1_ref, b1_ref, wu_ref, wd_ref, g2_ref, b2_ref,
                    y_ref, x1_scr, *, alpha):
    j = pl.program_id(1)

    @pl.when(j == 0)
    def _():
        cat = jnp.concatenate([oa_ref[...], ol_ref[...]], axis=1)
        x1 = _layer_norm(alpha * x_ref[...] + _dot(cat, wo_ref[...]), g1_ref[...], b1_ref[...])
        x1_scr[...] = x1.astype(BF16)
        y_ref[...] = alpha * x1

    hid = jnp.square(jnp.maximum(_dot(x1_scr[...], wu_ref[...]), 0.0)).astype(BF16)
    y_ref[...] += _dot(hid, wd_ref[...])

    @pl.when(j == pl.num_programs(1) - 1)
    def _():
        y_ref[...] = _layer_norm(y_ref[...], g2_ref[...], b2_ref[...])


def _out_mlp_call(x2d, oa16, ol16, w_out16, g1, b1, w_up16, w_down16, g2, b2, *, tm, tf, alpha):
    rows, d = x2d.shape
    aw = oa16.shape[1]
    lw = ol16.shape[1]
    d_ff = w_up16.shape[1]
    row = lambda i, j: (i, 0)
    const = lambda i, j: (0, 0)
    return pl.pallas_call(
        functools.partial(_out_mlp_kernel, alpha=alpha),
        grid=(rows // tm, d_ff // tf),
        in_specs=[pl.BlockSpec((tm, d), row),
                  pl.BlockSpec((tm, aw), row),
                  pl.BlockSpec((tm, lw), row),
                  pl.BlockSpec(w_out16.shape, const, pipeline_mode=pl.Buffered(1)),
                  pl.BlockSpec((1, d), const), pl.BlockSpec((1, d), const),
                  pl.BlockSpec((d, tf), lambda i, j: (0, j)),
                  pl.BlockSpec((tf, d), lambda i, j: (j, 0)),
                  pl.BlockSpec((1, d), const), pl.BlockSpec((1, d), const)],
        out_specs=pl.BlockSpec((tm, d), row),
        out_shape=jax.ShapeDtypeStruct((rows, d), F32),
        scratch_shapes=[pltpu.VMEM((tm, d), BF16)],
        compiler_params=pltpu.CompilerParams(dimension_semantics=("arbitrary", "arbitrary"),
                                             vmem_limit_bytes=VMEM_LIMIT_BYTES),
        name="out_mlp",
    )(x2d, oa16, ol16, w_out16, g1, b1, w_up16, w_down16, g2, b2)


def _rope_tables(pos):
    inv = ROPE_THETA ** (-jnp.arange(0, HALF_DIM, 2, dtype=F32) / HALF_DIM)
    ang = pos.astype(F32)[:, None] * inv[None, :]
    cos = jnp.tile(jnp.cos(ang), (1, 4))
    sin = jnp.sin(ang)
    sin = jnp.tile(jnp.concatenate([-sin, sin], axis=-1), (1, 2))
    return cos, sin


def _gate_weights(w_a, w_i):
    per = MXU_DIM // w_a.shape[-1]

    def dense(w):
        nb, bd, _ = w.shape
        w4 = w.reshape(nb // per, per, bd, bd)
        eye = jnp.eye(per, dtype=w.dtype)
        return jnp.einsum('gpij,pq->gpiqj', w4, eye).reshape(nb // per, per * bd, per * bd)

    return jnp.concatenate([dense(w_a), dense(w_i)], axis=-1).astype(BF16)


def _pick_tile(n, pref):
    t = min(n, pref)
    while n % t:
        t //= 2
    return t


def kernel(x_prompt, x_sample, cache_k, cache_v, state_h, state_conv, w_in, lambda_q1, lambda_k1, lambda_q2, lambda_k2, subln_g, conv_w, conv_b, w_rg_a, b_rg_a, w_rg_i, b_rg_i, lru_lambda, w_out, ln1_g, ln1_b, w_up, w_down, ln2_g, ln2_b):
    depth = w_in.shape[0]
    b, s, d = x_prompt.shape
    bd, t, _ = x_sample.shape
    p = cache_k.shape[2]
    lw = conv_w.shape[-1]
    aw = (w_in.shape[-1] - 2 * lw) // 3
    nh = aw // HEAD_DIM
    alpha = (2.0 * depth) ** 0.25

    cos_p, sin_p = _rope_tables(jnp.arange(s, dtype=jnp.int32))
    cos_s, sin_s = _rope_tables(p + jnp.arange(t, dtype=jnp.int32))
    cos_s = jnp.tile(cos_s, (bd, 1))
    sin_s = jnp.tile(sin_s, (bd, 1))

    tm_a = _pick_tile(s, 256)
    tq = _pick_tile(s, 512)
    tm_c = _pick_tile(b * s, 512)
    tf = _pick_tile(w_up.shape[-1], 1024)

    yp = x_prompt.reshape(b * s, d)
    ys = x_sample.reshape(bd * t, d)
    outs = [[] for _ in range(8)]
    for l in range(depth):
        lam0 = _lambda_init(l)
        w_in16 = w_in[l].astype(BF16)
        w_out16 = w_out[l].astype(BF16)
        w_up16 = w_up[l].astype(BF16)
        w_down16 = w_down[l].astype(BF16)
        wg = _gate_weights(w_rg_a[l], w_rg_i[l])
        row = lambda v: v[l].reshape(1, -1)
        lam_tile = _lam_call(row(lambda_q1), row(lambda_k1), row(lambda_q2), row(lambda_k2), lam0)
        g_sub = row(subln_g)
        lru_args = (conv_w[l], row(conv_b), wg, row(b_rg_a), row(b_rg_i), row(lru_lambda))
        ln_args = (w_out16, row(ln1_g), row(ln1_b), w_up16, w_down16, row(ln2_g), row(ln2_b))

        q16, k32, k16, v32, v16, ol16, hp, cp = _in_lru_call(
            yp, w_in16, cos_p, sin_p, *lru_args,
            jnp.zeros((b, 1, lw), F32), jnp.zeros((b, CONV_WIDTH - 1, lw), F32),
            tm=tm_a, seq_rows=tm_a, rows_per_batch=s)
        oa16 = _attn_call(lam_tile, g_sub, q16.reshape(b, s, aw), k16.reshape(b, s, aw),
                          v16.reshape(b, s, aw), t=tq, lam_init=lam0)
        yp = _out_mlp_call(yp, oa16.reshape(b * s, aw), ol16, *ln_args, tm=tm_c, tf=tf, alpha=alpha)
        outs[0].append(k32.reshape(b, s, nh, HEAD_DIM))
        outs[1].append(v32.reshape(b, s, nh, HEAD_DIM))
        outs[2].append(hp.reshape(b, lw))
        outs[3].append(cp)

        q16, k32, k16, v32, v16, ol16, hn, cn = _in_lru_call(
            ys, w_in16, cos_s, sin_s, *lru_args,
            state_h[l].reshape(bd, 1, lw), state_conv[l],
            tm=bd * t, seq_rows=t, rows_per_batch=t)
        oa16 = _dec_attn_call(lam_tile, g_sub, q16.reshape(bd, t, aw),
                              k32.reshape(bd, t, nh, HEAD_DIM), v32.reshape(bd, t, nh, HEAD_DIM),
                              cache_k.reshape(depth * bd, p, nh, HEAD_DIM),
                              cache_v.reshape(depth * bd, p, nh, HEAD_DIM), layer=l, lam_init=lam0)
        ys = _out_mlp_call(ys, oa16.reshape(bd * t, aw), ol16, *ln_args,
                           tm=_pick_tile(bd * t, 512), tf=tf, alpha=alpha)
        outs[4].append(k32.reshape(bd, t, nh, HEAD_DIM))
        outs[5].append(v32.reshape(bd, t, nh, HEAD_DIM))
        outs[6].append(hn.reshape(bd, lw))
        outs[7].append(cn)

    stk = [jnp.stack(o) for o in outs]
    return (yp.reshape(b, s, d), ys.reshape(bd, t, d), stk[0], stk[1], stk[2], stk[3],
            stk[4], stk[5], stk[6], stk[7])
```

```python
import functools
import math

import jax
import jax.numpy as jnp
from jax import lax
from jax.experimental import pallas as pl
from jax.experimental.pallas import tpu as pltpu

CHUNK = 64
HEAD_DIM = 128
HALF_DIM = HEAD_DIM // 2
LRU_BLOCKS = 16
CONV_WIDTH = 4
LRU_C = 8.0
ROPE_THETA = 10000.0
LN_EPS = 1e-5
RMS_EPS = 1e-5
NEG_INF = -1e30

SUBLANES = 8
LANES = 128
MXU_DIM = 256
ONES_ROWS = 16
VMEM_LIMIT_BYTES = 56 * 1024 * 1024

Q_SCALE = HALF_DIM ** -0.5 * math.log2(math.e)

BF16 = jnp.bfloat16
F32 = jnp.float32


def _lambda_init(layer):
    return 0.8 - 0.6 * math.exp(-0.3 * layer)


def _dot(a, b):
    return jnp.dot(a, b, preferred_element_type=F32)


def _dot_nt(a, b):
    return lax.dot_general(a, b, (((1,), (1,)), ((), ())), preferred_element_type=F32)


def _layer_norm(z, g, b):
    mu = jnp.mean(z, axis=-1, keepdims=True)
    d = z - mu
    var = jnp.mean(d * d, axis=-1, keepdims=True)
    return d * lax.rsqrt(var + LN_EPS) * g + b


def _lam_kernel(q1_ref, k1_ref, q2_ref, k2_ref, o_ref, *, lam_init):
    s1 = jnp.sum(q1_ref[...] * k1_ref[...], axis=-1, keepdims=True)
    s2 = jnp.sum(q2_ref[...] * k2_ref[...], axis=-1, keepdims=True)
    lam = jnp.exp(s1) - jnp.exp(s2) + lam_init
    o_ref[...] = jnp.broadcast_to(lam, o_ref.shape)


def _lam_call(lq1, lk1, lq2, lk2, lam_init):
    return pl.pallas_call(
        functools.partial(_lam_kernel, lam_init=lam_init),
        out_shape=jax.ShapeDtypeStruct((SUBLANES, LANES), F32),
        name="lam",
    )(lq1, lk1, lq2, lk2)


def _group_scan(a, u, sub):
    for s in (1, 2, 4):
        keep = sub >= s
        a_sh = pltpu.roll(a, s, axis=0)
        u_sh = pltpu.roll(u, s, axis=0)
        u = u + jnp.where(keep, a * u_sh, 0.0)
        a = a * jnp.where(keep, a_sh, 1.0)
    return a, u


def _in_lru_kernel(x_ref, w_ref, cos_ref, sin_ref, cw_ref, cb_ref, wg_ref, ba_ref, bi_ref,
                   lam_ref, h0_ref, c0_ref,
                   q_ref, k32_ref, k16_ref, v32_ref, v16_ref, ol_ref, hl_ref, cl_ref,
                   xp_scr, xc32_scr, xc16_scr, gate_scr, g_scr, h_scr,
                   *, tm, seq_rows, tiles_per_seq, aw, lw):
    carried = seq_rows == tm
    nseq = tm // seq_rows
    gps = seq_rows // SUBLANES
    i = pl.program_id(0)

    xb16 = x_ref[...].astype(BF16)

    cos = cos_ref[...]
    sin = sin_ref[...]
    lane = lax.broadcasted_iota(jnp.int32, (tm, LANES), 1)
    first_half = (lane % HALF_DIM) < (HALF_DIM // 2)

    def rope(xs):
        rot = jnp.where(first_half, pltpu.roll(xs, LANES - HALF_DIM // 2, axis=1),
                        pltpu.roll(xs, HALF_DIM // 2, axis=1))
        return xs * cos + rot * sin

    pieces_per_seg = aw // MXU_DIM
    n_pieces = 3 * pieces_per_seg

    def qkv_piece(p):
        seg, c0 = p // pieces_per_seg, (p % pieces_per_seg) * MXU_DIM
        pre = _dot(xb16, w_ref[:, seg * aw + c0:seg * aw + c0 + MXU_DIM])
        for h in range(MXU_DIM // LANES):
            sl = slice(c0 + h * LANES, c0 + (h + 1) * LANES)
            ph = pre[:, h * LANES:(h + 1) * LANES]
            if seg == 0:
                q_ref[:, sl] = (rope(ph) * Q_SCALE).astype(BF16)
            elif seg == 1:
                kr = rope(ph)
                k32_ref[:, sl] = kr
                k16_ref[:, sl] = kr.astype(BF16)
            else:
                v32_ref[:, sl] = ph
                v16_ref[:, sl] = ph.astype(BF16)

    xbp = _dot(xb16, w_ref[:, 3 * aw:3 * aw + lw])
    g_scr[...] = _dot(xb16, w_ref[:, 3 * aw + lw:3 * aw + 2 * lw])

    if carried:
        @pl.when(i % tiles_per_seq == 0)
        def _():
            xp_scr[0, 0:SUBLANES, :] = jnp.zeros((SUBLANES, lw), F32)
            h_scr[...] = jnp.zeros_like(h_scr)
    else:
        for sq in range(nseq):
            xp_scr[sq, SUBLANES - (CONV_WIDTH - 1):SUBLANES, :] = c0_ref[sq]
    for sq in range(nseq):
        xp_scr[sq, SUBLANES:, :] = xbp[sq * seq_rows:(sq + 1) * seq_rows, :]

    sub = lax.broadcasted_iota(jnp.int32, (SUBLANES, lw), 0)
    cwb = [jnp.broadcast_to(cw_ref[j:j + 1, :], (SUBLANES, lw)) for j in range(CONV_WIDTH)]
    cbb = jnp.broadcast_to(cb_ref[...], (SUBLANES, lw))
    for sq in range(nseq):
        prev = xp_scr[sq, 0:SUBLANES, :]
        prev_rolled = [pltpu.roll(prev, d, axis=0) for d in range(1, CONV_WIDTH)]
        for gg in range(gps):
            r0 = SUBLANES + gg * SUBLANES
            cur = xp_scr[sq, r0:r0 + SUBLANES, :]
            cur_rolled = [pltpu.roll(cur, d, axis=0) for d in range(1, CONV_WIDTH)]
            xc = cbb + cwb[CONV_WIDTH - 1] * cur
            for d in range(1, CONV_WIDTH):
                shifted = jnp.where(sub >= d, cur_rolled[d - 1], prev_rolled[d - 1])
                xc = xc + cwb[CONV_WIDTH - 1 - d] * shifted
            prev_rolled = cur_rolled
            g0 = (sq * gps + gg) * SUBLANES
            xc32_scr[g0:g0 + SUBLANES, :] = xc
            xc16_scr[g0:g0 + SUBLANES, :] = xc.astype(BF16)

    for sq in range(nseq):
        cl_ref[sq] = xp_scr[sq, SUBLANES + seq_rows - (CONV_WIDTH - 1):SUBLANES + seq_rows, :]
    if carried:
        xp_scr[0, 0:SUBLANES, :] = xp_scr[0, seq_rows:seq_rows + SUBLANES, :]

    for blk in range(lw // MXU_DIM):
        sl = slice(blk * MXU_DIM, (blk + 1) * MXU_DIM)
        gate_scr[:, 2 * blk * MXU_DIM:2 * (blk + 1) * MXU_DIM] = _dot(xc16_scr[:, sl], wg_ref[blk])

    nl = -lam_ref[...]
    softplus = jnp.maximum(nl, 0.0) + jnp.log1p(jnp.exp(-jnp.abs(nl)))
    decay = jnp.broadcast_to(-LRU_C * softplus, (SUBLANES, lw))
    ba = jnp.broadcast_to(ba_ref[...], (SUBLANES, lw))
    bi = jnp.broadcast_to(bi_ref[...], (SUBLANES, lw))

    def sigmoid(z):
        return 0.5 * jnp.tanh(0.5 * z) + 0.5

    n_groups = nseq * gps
    pieces_done = 0
    after = None
    for sq in range(nseq):
        hin = h_scr[...] if carried else h0_ref[sq]
        for gg in range(gps):
            while pieces_done * n_groups < (sq * gps + gg) * n_pieces:
                qkv_piece(pieces_done)
                pieces_done += 1
            g0 = (sq * gps + gg) * SUBLANES
            rows = slice(g0, g0 + SUBLANES)
            pre = gate_scr[rows, :]
            ra = jnp.concatenate(
                [pre[:, 2 * b * MXU_DIM:(2 * b + 1) * MXU_DIM] for b in range(lw // MXU_DIM)], axis=1)
            ri = jnp.concatenate(
                [pre[:, (2 * b + 1) * MXU_DIM:(2 * b + 2) * MXU_DIM] for b in range(lw // MXU_DIM)], axis=1)
            if after is not None:
                ra = ra + after
                ri = ri + after
            r = sigmoid(ra + ba)
            ig = sigmoid(ri + bi)
            log_a = decay * r
            a = jnp.exp(log_a)
            y = (1.0 + a * a) * jnp.tanh(-log_a)
            root = jnp.where(y > 0.0, y * lax.rsqrt(y), 0.0)
            u = root * (ig * xc32_scr[rows, :])
            acum, ucum = _group_scan(a, u, sub)
            hs = acum * hin + ucum
            hin = hs[SUBLANES - 1:SUBLANES, :]
            out = hs * jax.nn.gelu(g_scr[rows, :])
            ol_ref[rows, :] = out.astype(BF16)
            after = pltpu.bitcast(
                lax.shift_right_logical(pltpu.bitcast(out, jnp.uint32), jnp.uint32(32)), F32)
        hl_ref[sq] = hin
        if carried:
            h_scr[...] = hin
    while pieces_done < n_pieces:
        qkv_piece(pieces_done)
        pieces_done += 1


def _in_lru_call(x2d, w_in16, cos, sin, conv_w, conv_b, wg, b_a, b_i, lru_lambda, h0, c0,
                 *, tm, seq_rows, rows_per_batch):
    rows, d_model = x2d.shape
    aw = (w_in16.shape[1] - 2 * conv_w.shape[1]) // 3
    lw = conv_w.shape[1]
    carried = seq_rows == tm
    n_tiles = rows // tm
    tiles_per_seq = rows_per_batch // tm if carried else 1
    nseq_tile = tm // seq_rows
    nseq_total = rows // rows_per_batch
    if carried:
        state_map = lambda i: (i // tiles_per_seq, 0, 0)
    else:
        state_map = lambda i: (i, 0, 0)
    const2 = lambda i: (0, 0)
    const3 = lambda i: (0, 0, 0)
    row_map = lambda i: (i, 0)
    resident = dict(pipeline_mode=pl.Buffered(1))
    in_specs = [
        pl.BlockSpec((tm, d_model), row_map),
        pl.BlockSpec(w_in16.shape, const2, **resident),
        pl.BlockSpec((tm, LANES), (lambda i: (i % tiles_per_seq, 0)) if carried else const2),
        pl.BlockSpec((tm, LANES), (lambda i: (i % tiles_per_seq, 0)) if carried else const2),
        pl.BlockSpec(conv_w.shape, const2),
        pl.BlockSpec(conv_b.shape, const2),
        pl.BlockSpec(wg.shape, const3, **resident),
        pl.BlockSpec(b_a.shape, const2),
        pl.BlockSpec(b_i.shape, const2),
        pl.BlockSpec(lru_lambda.shape, const2),
        pl.BlockSpec((nseq_tile, 1, lw), state_map),
        pl.BlockSpec((nseq_tile, CONV_WIDTH - 1, lw), state_map),
    ]
    out_shape = (
        jax.ShapeDtypeStruct((rows, aw), BF16),
        jax.ShapeDtypeStruct((rows, aw), F32),
        jax.ShapeDtypeStruct((rows, aw), BF16),
        jax.ShapeDtypeStruct((rows, aw), F32),
        jax.ShapeDtypeStruct((rows, aw), BF16),
        jax.ShapeDtypeStruct((rows, lw), BF16),
        jax.ShapeDtypeStruct((nseq_total, 1, lw), F32),
        jax.ShapeDtypeStruct((nseq_total, CONV_WIDTH - 1, lw), F32),
    )
    out_specs = (
        pl.BlockSpec((tm, aw), row_map),
        pl.BlockSpec((tm, aw), row_map),
        pl.BlockSpec((tm, aw), row_map),
        pl.BlockSpec((tm, aw), row_map),
        pl.BlockSpec((tm, aw), row_map),
        pl.BlockSpec((tm, lw), row_map),
        pl.BlockSpec((nseq_tile, 1, lw), state_map),
        pl.BlockSpec((nseq_tile, CONV_WIDTH - 1, lw), state_map),
    )
    scratch = [
        pltpu.VMEM((nseq_tile, SUBLANES + seq_rows, lw), F32),
        pltpu.VMEM((tm, lw), F32),
        pltpu.VMEM((tm, lw), BF16),
        pltpu.VMEM((tm, 2 * lw), F32),
        pltpu.VMEM((tm, lw), F32),
        pltpu.VMEM((1, lw), F32),
    ]
    kern = functools.partial(_in_lru_kernel, tm=tm, seq_rows=seq_rows, tiles_per_seq=tiles_per_seq,
                             aw=aw, lw=lw)
    return pl.pallas_call(
        kern, grid=(n_tiles,), in_specs=in_specs, out_specs=out_specs, out_shape=out_shape,
        scratch_shapes=scratch,
        compiler_params=pltpu.CompilerParams(dimension_semantics=("arbitrary",),
                                             vmem_limit_bytes=VMEM_LIMIT_BYTES),
        name="in_lru",
    )(x2d, w_in16, cos, sin, conv_w, conv_b, wg, b_a, b_i, lru_lambda, h0, c0)


def _attn_kernel(lam_ref, g_ref, bias_ref, q_ref, k_ref, vt_ref, o_ref, s_scr, m_scr, acc_scr,
                 *, t, n_q, lam_init):
    def scores(i, j, slot):
        q = q_ref[pl.ds(pl.multiple_of(i * t, t), t), :]
        lane = lax.broadcasted_iota(jnp.int32, (t, HEAD_DIM), 1)
        zero = jnp.zeros_like(q)
        qq = jnp.concatenate([jnp.where(lane < HALF_DIM, q, zero),
                              jnp.where(lane >= HALF_DIM, q, zero)], axis=0)
        kj = k_ref[pl.ds(pl.multiple_of(j * t, t), t), :]
        s_scr[slot] = _dot_nt(kj, qq)

    def softmax_pv(j, slot, masked):
        s = s_scr[slot]
        if masked:
            s = s + bias_ref[...]
        m = m_scr[...]
        m_new = jnp.maximum(m, jnp.max(s, axis=0, keepdims=True))
        alpha = jnp.exp2(m - m_new)
        p = jnp.exp2(s - m_new)
        m_scr[...] = m_new
        acc_scr[...] = alpha * acc_scr[...] + _dot(vt_ref[j], p.astype(BF16))

    def reset():
        m_scr[...] = jnp.full(m_scr.shape, NEG_INF, F32)
        acc_scr[...] = jnp.zeros(acc_scr.shape, F32)

    def finish(i):
        acc = acc_scr[...]
        o = acc[0:HEAD_DIM] / acc[HEAD_DIM:HEAD_DIM + 1]
        o = o[:, 0:t] - lam_ref[0:1, 0:1] * o[:, t:2 * t]
        ms = jnp.mean(o * o, axis=0, keepdims=True)
        o = o * lax.rsqrt(ms + RMS_EPS) * g_ref[...] * (1.0 - lam_init)
        o_ref[pl.ds(pl.multiple_of(i * t, t), t), :] = o.T.astype(BF16)
        reset()

    def following(i, j):
        edge = j == i
        return jnp.where(edge, i + 1, i), jnp.where(edge, 0, j + 1)

    def pair(_, ij):
        ia, ja = ij
        ib, jb = following(ia, ja)
        ic, jc = following(ib, jb)
        icc = jnp.minimum(ic, n_q - 1)
        edge_a = ja == ia
        edge_b = jb == ib

        def body(mask_a, mask_b):
            scores(ib, jb, 1)
            softmax_pv(ja, 0, mask_a)
            if mask_a:
                finish(ia)
            scores(icc, jc, 0)
            softmax_pv(jb, 1, mask_b)
            if mask_b:
                finish(ib)

        pl.when(edge_a)(lambda: body(True, False))
        pl.when(edge_b)(lambda: body(False, True))
        pl.when(jnp.logical_not(jnp.logical_or(edge_a, edge_b)))(lambda: body(False, False))
        return ic, jc

    reset()
    scores(0, 0, 0)
    n_items = n_q * (n_q + 1) // 2
    lax.fori_loop(0, n_items // 2, pair, (jnp.int32(0), jnp.int32(0)))
    if n_items % 2:
        softmax_pv(n_q - 1, 0, True)
        finish(n_q - 1)


def _attn_call(lam_tile, subln_g, q16, k16, v16, *, t, lam_init):
    b, s, aw = q16.shape
    nh = aw // HEAD_DIM
    vt = v16.reshape(b, s // t, t, nh, HEAD_DIM).transpose(0, 3, 1, 4, 2)
    vt = jnp.concatenate([vt, jnp.ones((b, nh, s // t, ONES_ROWS, t), BF16)], axis=3)
    key_chunk = jnp.arange(t, dtype=jnp.int32)[:, None] // CHUNK
    qry_chunk = (jnp.arange(2 * t, dtype=jnp.int32)[None, :] % t) // CHUNK
    bias = jnp.where(key_chunk <= qry_chunk, 0.0, NEG_INF).astype(F32)
    seq_spec = pl.BlockSpec((None, s, HEAD_DIM), lambda bi, hi: (bi, 0, hi))
    const = lambda bi, hi: (0, 0)
    return pl.pallas_call(
        functools.partial(_attn_kernel, t=t, n_q=s // t, lam_init=lam_init),
        grid=(b, nh),
        in_specs=[pl.BlockSpec((SUBLANES, LANES), const),
                  pl.BlockSpec((HEAD_DIM, 1), const),
                  pl.BlockSpec((t, 2 * t), const),
                  seq_spec, seq_spec,
                  pl.BlockSpec((None, None, s // t, HEAD_DIM + ONES_ROWS, t),
                               lambda bi, hi: (bi, hi, 0, 0, 0))],
        out_specs=seq_spec,
        out_shape=jax.ShapeDtypeStruct((b, s, aw), BF16),
        scratch_shapes=[pltpu.VMEM((2, t, 2 * t), F32),
                        pltpu.VMEM((1, 2 * t), F32),
                        pltpu.VMEM((HEAD_DIM + ONES_ROWS, 2 * t), F32)],
        compiler_params=pltpu.CompilerParams(
            dimension_semantics=("arbitrary", "arbitrary"),
            vmem_limit_bytes=VMEM_LIMIT_BYTES),
        name="attn",
    )(lam_tile, subln_g.reshape(HEAD_DIM, 1), bias, q16, k16, vt)


def _dec_attn_kernel(lam_ref, g_ref, bias_ref, q_ref, kn_ref, vn_ref, ck_ref, cv_ref, o_ref,
                     *, t, nh, kc, lam_init):
    rows = 2 * nh * t
    p_len = ck_ref.shape[0]
    q = q_ref[...]
    lane = lax.broadcasted_iota(jnp.int32, (t, HEAD_DIM), 1)
    parts = []
    for c in range(2):
        keep = (lane < HALF_DIM) if c == 0 else (lane >= HALF_DIM)
        for h in range(nh):
            qh = q[:, h * HEAD_DIM:(h + 1) * HEAD_DIM]
            parts.append(jnp.where(keep, qh, jnp.zeros_like(qh)))
    qall = jnp.concatenate(parts, axis=0)

    def scores(kf):
        s = _dot_nt(qall, kf.astype(BF16))
        return s + bias_ref[:, 0:s.shape[1]]

    def update(carry, s, vf):
        m, l, acc = carry
        m_new = jnp.maximum(m, jnp.max(s, axis=-1, keepdims=True))
        alpha = jnp.exp2(m - m_new)
        p = jnp.exp2(s - m_new)
        l = alpha * l + jnp.sum(p, axis=-1, keepdims=True)
        acc = alpha * acc + _dot(p.astype(BF16), vf.astype(BF16))
        return m_new, l, acc

    carry = (jnp.full((rows, 1), NEG_INF, F32), jnp.zeros((rows, 1), F32),
             jnp.zeros((rows, HEAD_DIM), F32))
    for j in range(p_len // kc):
        kf = ck_ref[j * kc:(j + 1) * kc].reshape(kc * nh, HEAD_DIM)
        vf = cv_ref[j * kc:(j + 1) * kc].reshape(kc * nh, HEAD_DIM)
        carry = update(carry, scores(kf), vf)
    kf = kn_ref[...].reshape(t * nh, HEAD_DIM)
    vf = vn_ref[...].reshape(t * nh, HEAD_DIM)
    _, l, acc = update(carry, scores(kf), vf)

    o = acc / l
    half = rows // 2
    o = o[0:half] - lam_ref[0:1, 0:1] * o[half:rows]
    ms = jnp.mean(o * o, axis=-1, keepdims=True)
    o = o * lax.rsqrt(ms + RMS_EPS) * g_ref[...] * (1.0 - lam_init)
    for h in range(nh):
        o_ref[:, h * HEAD_DIM:(h + 1) * HEAD_DIM] = o[h * t:(h + 1) * t].astype(BF16)


def _dec_attn_call(lam_tile, subln_g, q16, kn32, vn32, cache_k, cache_v, *, layer, lam_init):
    bd, t, aw = q16.shape
    _, p, nh, hd = cache_k.shape
    new_spec = pl.BlockSpec((None, t, nh, hd), lambda b: (b, 0, 0, 0))
    cache_spec = pl.BlockSpec((None, p, nh, hd), lambda b: (layer * bd + b, 0, 0, 0))
    io_spec = pl.BlockSpec((None, t, aw), lambda b: (b, 0, 0))
    kc = _pick_tile(p, 256)
    ncol = max(kc, t) * nh
    row_head = (jnp.arange(2 * nh * t, dtype=jnp.int32)[:, None] // t) % nh
    col_head = jnp.arange(ncol, dtype=jnp.int32)[None, :] % nh
    bias = jnp.where(row_head == col_head, 0.0, NEG_INF).astype(F32)
    return pl.pallas_call(
        functools.partial(_dec_attn_kernel, t=t, nh=nh, kc=kc, lam_init=lam_init),
        grid=(bd,),
        in_specs=[pl.BlockSpec((SUBLANES, LANES), lambda b: (0, 0)),
                  pl.BlockSpec((1, HEAD_DIM), lambda b: (0, 0)),
                  pl.BlockSpec(bias.shape, lambda b: (0, 0)),
                  io_spec, new_spec, new_spec, cache_spec, cache_spec],
        out_specs=io_spec,
        out_shape=jax.ShapeDtypeStruct((bd, t, aw), BF16),
        compiler_params=pltpu.CompilerParams(dimension_semantics=("arbitrary",),
                                             vmem_limit_bytes=VMEM_LIMIT_BYTES),
        name="dec_attn",
    )(lam_tile, subln_g, bias, q16, kn32, vn32, cache_k, cache_v)


def _out_mlp_kernel(x_ref, oa_ref, ol_ref, wo_ref, g1_ref, b1_ref, wu_ref, wd_ref, g2_ref, b2_ref,
                    y_ref, *rest, alpha, emit_weights):
    j = pl.program_id(1)
    if emit_weights:
        wu16_ref, wd16_ref, x1_scr = rest
    else:
        (x1_scr,) = rest

    @pl.when(j == 0)
    def _():
        cat = jnp.concatenate([oa_ref[...], ol_ref[...]], axis=1)
        x1 = _layer_norm(alpha * x_ref[...] + _dot(cat, wo_ref[...]), g1_ref[...], b1_ref[...])
        x1_scr[...] = x1.astype(BF16)
        y_ref[...] = alpha * x1

    wu = wu_ref[...]
    wd = wd_ref[...]
    if emit_weights:
        wu = wu.astype(BF16)
        wd = wd.astype(BF16)
        wu16_ref[...] = wu
        wd16_ref[...] = wd
    hid = jnp.square(jnp.maximum(_dot(x1_scr[...], wu), 0.0)).astype(BF16)
    y_ref[...] += _dot(hid, wd)

    @pl.when(j == pl.num_programs(1) - 1)
    def _():
        y_ref[...] = _layer_norm(y_ref[...], g2_ref[...], b2_ref[...])


def _out_mlp_call(x2d, oa16, ol16, w_out16, g1, b1, w_up, w_down, g2, b2, *, tm, tf, alpha):
    rows, d = x2d.shape
    aw = oa16.shape[1]
    lw = ol16.shape[1]
    d_ff = w_up.shape[1]
    emit_weights = w_up.dtype != BF16
    assert not emit_weights or rows == tm
    row = lambda i, j: (i, 0)
    const = lambda i, j: (0, 0)
    up_spec = pl.BlockSpec((d, tf), lambda i, j: (0, j))
    down_spec = pl.BlockSpec((tf, d), lambda i, j: (j, 0))
    out_specs = [pl.BlockSpec((tm, d), row)]
    out_shape = [jax.ShapeDtypeStruct((rows, d), F32)]
    if emit_weights:
        out_specs += [up_spec, down_spec]
        out_shape += [jax.ShapeDtypeStruct(w_up.shape, BF16), jax.ShapeDtypeStruct(w_down.shape, BF16)]
    outs = pl.pallas_call(
        functools.partial(_out_mlp_kernel, alpha=alpha, emit_weights=emit_weights),
        grid=(rows // tm, d_ff // tf),
        in_specs=[pl.BlockSpec((tm, d), row),
                  pl.BlockSpec((tm, aw), row),
                  pl.BlockSpec((tm, lw), row),
                  pl.BlockSpec(w_out16.shape, const, pipeline_mode=pl.Buffered(1)),
                  pl.BlockSpec((1, d), const), pl.BlockSpec((1, d), const),
                  up_spec, down_spec,
                  pl.BlockSpec((1, d), const), pl.BlockSpec((1, d), const)],
        out_specs=out_specs,
        out_shape=out_shape,
        scratch_shapes=[pltpu.VMEM((tm, d), BF16)],
        compiler_params=pltpu.CompilerParams(dimension_semantics=("arbitrary", "arbitrary"),
                                             vmem_limit_bytes=VMEM_LIMIT_BYTES),
        name="out_mlp",
    )(x2d, oa16, ol16, w_out16, g1, b1, w_up, w_down, g2, b2)
    return outs if emit_weights else outs[0]


def _rope_tables(pos):
    inv = ROPE_THETA ** (-jnp.arange(0, HALF_DIM, 2, dtype=F32) / HALF_DIM)
    ang = pos.astype(F32)[:, None] * inv[None, :]
    cos = jnp.tile(jnp.cos(ang), (1, 4))
    sin = jnp.sin(ang)
    sin = jnp.tile(jnp.concatenate([-sin, sin], axis=-1), (1, 2))
    return cos, sin


def _gate_weights(w_a, w_i):
    per = MXU_DIM // w_a.shape[-1]

    def dense(w):
        nb, bd, _ = w.shape
        w4 = w.reshape(nb // per, per, bd, bd)
        eye = jnp.eye(per, dtype=w.dtype)
        return jnp.einsum('gpij,pq->gpiqj', w4, eye).reshape(nb // per, per * bd, per * bd)

    return jnp.concatenate([dense(w_a), dense(w_i)], axis=-1).astype(BF16)


def _pick_tile(n, pref):
    t = min(n, pref)
    while n % t:
        t //= 2
    return t


def kernel(x_prompt, x_sample, cache_k, cache_v, state_h, state_conv, w_in, lambda_q1, lambda_k1, lambda_q2, lambda_k2, subln_g, conv_w, conv_b, w_rg_a, b_rg_a, w_rg_i, b_rg_i, lru_lambda, w_out, ln1_g, ln1_b, w_up, w_down, ln2_g, ln2_b):
    depth = w_in.shape[0]
    b, s, d = x_prompt.shape
    bd, t, _ = x_sample.shape
    p = cache_k.shape[2]
    lw = conv_w.shape[-1]
    aw = (w_in.shape[-1] - 2 * lw) // 3
    nh = aw // HEAD_DIM
    alpha = (2.0 * depth) ** 0.25

    cos_p, sin_p = _rope_tables(jnp.arange(s, dtype=jnp.int32))
    cos_s, sin_s = _rope_tables(p + jnp.arange(t, dtype=jnp.int32))
    cos_s = jnp.tile(cos_s, (bd, 1))
    sin_s = jnp.tile(sin_s, (bd, 1))

    tm_a = _pick_tile(s, 256)
    tq = _pick_tile(s, 512)
    tm_c = _pick_tile(b * s, 512)
    tf = _pick_tile(w_up.shape[-1], 1024)
    tf_s = _pick_tile(w_up.shape[-1], 512)

    yp = x_prompt.reshape(b * s, d)
    ys = x_sample.reshape(bd * t, d)
    outs = [[] for _ in range(8)]
    for l in range(depth):
        lam0 = _lambda_init(l)
        w_in16 = w_in[l].astype(BF16)
        w_out16 = w_out[l].astype(BF16)
        wg = _gate_weights(w_rg_a[l], w_rg_i[l])
        row = lambda v: v[l].reshape(1, -1)
        lam_tile = _lam_call(row(lambda_q1), row(lambda_k1), row(lambda_q2), row(lambda_k2), lam0)
        g_sub = row(subln_g)
        lru_args = (conv_w[l], row(conv_b), wg, row(b_rg_a), row(b_rg_i), row(lru_lambda))
        ln1_args = (w_out16, row(ln1_g), row(ln1_b))
        ln2_args = (row(ln2_g), row(ln2_b))

        q16, k32, k16, v32, v16, ol16, hn, cn = _in_lru_call(
            ys, w_in16, cos_s, sin_s, *lru_args,
            state_h[l].reshape(bd, 1, lw), state_conv[l],
            tm=bd * t, seq_rows=t, rows_per_batch=t)
        oa16 = _dec_attn_call(lam_tile, g_sub, q16.reshape(bd, t, aw),
                              k32.reshape(bd, t, nh, HEAD_DIM), v32.reshape(bd, t, nh, HEAD_DIM),
                              cache_k.reshape(depth * bd, p, nh, HEAD_DIM),
                              cache_v.reshape(depth * bd, p, nh, HEAD_DIM), layer=l, lam_init=lam0)
        ys, w_up16, w_down16 = _out_mlp_call(
            ys, oa16.reshape(bd * t, aw), ol16, *ln1_args, w_up[l], w_down[l], *ln2_args,
            tm=bd * t, tf=tf_s, alpha=alpha)
        outs[4].append(k32.reshape(bd, t, nh, HEAD_DIM))
        outs[5].append(v32.reshape(bd, t, nh, HEAD_DIM))
        outs[6].append(hn.reshape(bd, lw))
        outs[7].append(cn)

        q16, k32, k16, v32, v16, ol16, hp, cp = _in_lru_call(
            yp, w_in16, cos_p, sin_p, *lru_args,
            jnp.zeros((b, 1, lw), F32), jnp.zeros((b, CONV_WIDTH - 1, lw), F32),
            tm=tm_a, seq_rows=tm_a, rows_per_batch=s)
        oa16 = _attn_call(lam_tile, g_sub, q16.reshape(b, s, aw), k16.reshape(b, s, aw),
                          v16.reshape(b, s, aw), t=tq, lam_init=lam0)
        yp = _out_mlp_call(yp, oa16.reshape(b * s, aw), ol16, *ln1_args, w_up16, w_down16, *ln2_args,
                           tm=tm_c, tf=tf, alpha=alpha)
        outs[0].append(k32.reshape(b, s, nh, HEAD_DIM))
        outs[1].append(v32.reshape(b, s, nh, HEAD_DIM))
        outs[2].append(hp.reshape(b, lw))
        outs[3].append(cp)

    stk = [jnp.stack(o) for o in outs]
    return (yp.reshape(b, s, d), ys.reshape(bd, t, d), stk[0], stk[1], stk[2], stk[3],
            stk[4], stk[5], stk[6], stk[7])
```

```python
import functools
import math

import jax
import jax.numpy as jnp
from jax import lax
from jax.experimental import pallas as pl
from jax.experimental.pallas import tpu as pltpu

CHUNK = 64
HEAD_DIM = 128
HALF_DIM = HEAD_DIM // 2
LRU_BLOCKS = 16
CONV_WIDTH = 4
LRU_C = 8.0
ROPE_THETA = 10000.0
LN_EPS = 1e-5
RMS_EPS = 1e-5
NEG_INF = -1e30

SUBLANES = 8
LANES = 128
MXU_DIM = 256
ONES_ROWS = 16
VMEM_LIMIT_BYTES = 56 * 1024 * 1024

Q_SCALE = HALF_DIM ** -0.5 * math.log2(math.e)

BF16 = jnp.bfloat16
F32 = jnp.float32


def _lambda_init(layer):
    return 0.8 - 0.6 * math.exp(-0.3 * layer)


def _dot(a, b):
    return jnp.dot(a, b, preferred_element_type=F32)


def _dot_nt(a, b):
    return lax.dot_general(a, b, (((1,), (1,)), ((), ())), preferred_element_type=F32)


def _layer_norm(z, g, b):
    mu = jnp.mean(z, axis=-1, keepdims=True)
    d = z - mu
    var = jnp.mean(d * d, axis=-1, keepdims=True)
    return d * lax.rsqrt(var + LN_EPS) * g + b


def _lam_kernel(q1_ref, k1_ref, q2_ref, k2_ref, o_ref, *, lam_init):
    s1 = jnp.sum(q1_ref[...] * k1_ref[...], axis=-1, keepdims=True)
    s2 = jnp.sum(q2_ref[...] * k2_ref[...], axis=-1, keepdims=True)
    lam = jnp.exp(s1) - jnp.exp(s2) + lam_init
    o_ref[...] = jnp.broadcast_to(lam, o_ref.shape)


def _lam_call(lq1, lk1, lq2, lk2, lam_init):
    return pl.pallas_call(
        functools.partial(_lam_kernel, lam_init=lam_init),
        out_shape=jax.ShapeDtypeStruct((SUBLANES, LANES), F32),
        name="lam",
    )(lq1, lk1, lq2, lk2)


def _group_scan(a, u, sub):
    for s in (1, 2, 4):
        keep = sub >= s
        a_sh = pltpu.roll(a, s, axis=0)
        u_sh = pltpu.roll(u, s, axis=0)
        u = u + jnp.where(keep, a * u_sh, 0.0)
        a = a * jnp.where(keep, a_sh, 1.0)
    return a, u


def _in_lru_kernel(x_ref, w_ref, cos_ref, sin_ref, cw_ref, cb_ref, wg_ref, ba_ref, bi_ref,
                   lam_ref, h0_ref, c0_ref,
                   q_ref, qt_ref, k32_ref, k16_ref, v32_ref, vt_ref, ol_ref, hl_ref, cl_ref,
                   xp_scr, xc32_scr, xc16_scr, gate_scr, g_scr, h_scr,
                   *, tm, seq_rows, tiles_per_seq, aw, lw):
    carried = seq_rows == tm
    nseq = tm // seq_rows
    gps = seq_rows // SUBLANES
    i = pl.program_id(0)

    xb16 = x_ref[...].astype(BF16)

    cos = cos_ref[...]
    sin = sin_ref[...]
    lane = lax.broadcasted_iota(jnp.int32, (tm, LANES), 1)
    first_half = (lane % HALF_DIM) < (HALF_DIM // 2)

    def rope(xs):
        rot = jnp.where(first_half, pltpu.roll(xs, LANES - HALF_DIM // 2, axis=1),
                        pltpu.roll(xs, HALF_DIM // 2, axis=1))
        return xs * cos + rot * sin

    pieces_per_seg = aw // MXU_DIM
    n_pieces = 3 * pieces_per_seg

    def qkv_piece(p):
        seg, c0 = p // pieces_per_seg, (p % pieces_per_seg) * MXU_DIM
        pre = _dot(xb16, w_ref[:, seg * aw + c0:seg * aw + c0 + MXU_DIM])
        for h in range(MXU_DIM // LANES):
            sl = slice(c0 + h * LANES, c0 + (h + 1) * LANES)
            ph = pre[:, h * LANES:(h + 1) * LANES]
            if seg == 0:
                qs = rope(ph) * Q_SCALE
                q_ref[:, sl] = qs.astype(BF16)
                qt_ref[c0 // LANES + h] = qs.T.astype(BF16)
            elif seg == 1:
                kr = rope(ph)
                k32_ref[:, sl] = kr
                k16_ref[:, sl] = kr.astype(BF16)
            else:
                v32_ref[:, sl] = ph
                head = c0 // LANES + h
                vt_ref[head, 0:HEAD_DIM, :] = ph.T.astype(BF16)
                vt_ref[head, HEAD_DIM:HEAD_DIM + ONES_ROWS, :] = jnp.ones((ONES_ROWS, tm), BF16)

    xbp = _dot(xb16, w_ref[:, 3 * aw:3 * aw + lw])
    g_scr[...] = _dot(xb16, w_ref[:, 3 * aw + lw:3 * aw + 2 * lw])

    if carried:
        @pl.when(i % tiles_per_seq == 0)
        def _():
            xp_scr[0, 0:SUBLANES, :] = jnp.zeros((SUBLANES, lw), F32)
            h_scr[...] = jnp.zeros_like(h_scr)
    else:
        for sq in range(nseq):
            xp_scr[sq, SUBLANES - (CONV_WIDTH - 1):SUBLANES, :] = c0_ref[sq]
    for sq in range(nseq):
        xp_scr[sq, SUBLANES:, :] = xbp[sq * seq_rows:(sq + 1) * seq_rows, :]

    sub = lax.broadcasted_iota(jnp.int32, (SUBLANES, lw), 0)
    cwb = [jnp.broadcast_to(cw_ref[j:j + 1, :], (SUBLANES, lw)) for j in range(CONV_WIDTH)]
    cbb = jnp.broadcast_to(cb_ref[...], (SUBLANES, lw))
    for sq in range(nseq):
        prev = xp_scr[sq, 0:SUBLANES, :]
        prev_rolled = [pltpu.roll(prev, d, axis=0) for d in range(1, CONV_WIDTH)]
        for gg in range(gps):
            r0 = SUBLANES + gg * SUBLANES
            cur = xp_scr[sq, r0:r0 + SUBLANES, :]
            cur_rolled = [pltpu.roll(cur, d, axis=0) for d in range(1, CONV_WIDTH)]
            xc = cbb + cwb[CONV_WIDTH - 1] * cur
            for d in range(1, CONV_WIDTH):
                shifted = jnp.where(sub >= d, cur_rolled[d - 1], prev_rolled[d - 1])
                xc = xc + cwb[CONV_WIDTH - 1 - d] * shifted
            prev_rolled = cur_rolled
            g0 = (sq * gps + gg) * SUBLANES
            xc32_scr[g0:g0 + SUBLANES, :] = xc
            xc16_scr[g0:g0 + SUBLANES, :] = xc.astype(BF16)

    for sq in range(nseq):
        cl_ref[sq] = xp_scr[sq, SUBLANES + seq_rows - (CONV_WIDTH - 1):SUBLANES + seq_rows, :]
    if carried:
        xp_scr[0, 0:SUBLANES, :] = xp_scr[0, seq_rows:seq_rows + SUBLANES, :]

    for blk in range(lw // MXU_DIM):
        sl = slice(blk * MXU_DIM, (blk + 1) * MXU_DIM)
        gate_scr[:, 2 * blk * MXU_DIM:2 * (blk + 1) * MXU_DIM] = _dot(xc16_scr[:, sl], wg_ref[blk])

    nl = -lam_ref[...]
    softplus = jnp.maximum(nl, 0.0) + jnp.log1p(jnp.exp(-jnp.abs(nl)))
    decay = jnp.broadcast_to(-LRU_C * softplus, (SUBLANES, lw))
    ba = jnp.broadcast_to(ba_ref[...], (SUBLANES, lw))
    bi = jnp.broadcast_to(bi_ref[...], (SUBLANES, lw))

    def sigmoid(z):
        return 0.5 * jnp.tanh(0.5 * z) + 0.5

    n_groups = nseq * gps
    pieces_done = 0
    after = None
    for sq in range(nseq):
        hin = h_scr[...] if carried else h0_ref[sq]
        for gg in range(gps):
            while pieces_done * n_groups < (sq * gps + gg) * n_pieces:
                qkv_piece(pieces_done)
                pieces_done += 1
            g0 = (sq * gps + gg) * SUBLANES
            rows = slice(g0, g0 + SUBLANES)
            pre = gate_scr[rows, :]
            ra = jnp.concatenate(
                [pre[:, 2 * b * MXU_DIM:(2 * b + 1) * MXU_DIM] for b in range(lw // MXU_DIM)], axis=1)
            ri = jnp.concatenate(
                [pre[:, (2 * b + 1) * MXU_DIM:(2 * b + 2) * MXU_DIM] for b in range(lw // MXU_DIM)], axis=1)
            if after is not None:
                ra = ra + after
                ri = ri + after
            r = sigmoid(ra + ba)
            ig = sigmoid(ri + bi)
            log_a = decay * r
            a = jnp.exp(log_a)
            y = (1.0 + a * a) * jnp.tanh(-log_a)
            root = jnp.where(y > 0.0, y * lax.rsqrt(y), 0.0)
            u = root * (ig * xc32_scr[rows, :])
            acum, ucum = _group_scan(a, u, sub)
            hs = acum * hin + ucum
            hin = hs[SUBLANES - 1:SUBLANES, :]
            out = hs * jax.nn.gelu(g_scr[rows, :])
            ol_ref[rows, :] = out.astype(BF16)
            after = pltpu.bitcast(
                lax.shift_right_logical(pltpu.bitcast(out, jnp.uint32), jnp.uint32(32)), F32)
        hl_ref[sq] = hin
        if carried:
            h_scr[...] = hin
    while pieces_done < n_pieces:
        qkv_piece(pieces_done)
        pieces_done += 1


def _in_lru_call(x2d, w_in16, cos, sin, conv_w, conv_b, wg, b_a, b_i, lru_lambda, h0, c0,
                 *, tm, seq_rows, rows_per_batch, vt_tile):
    rows, d_model = x2d.shape
    aw = (w_in16.shape[1] - 2 * conv_w.shape[1]) // 3
    lw = conv_w.shape[1]
    nh = aw // HEAD_DIM
    carried = seq_rows == tm
    n_tiles = rows // tm
    tiles_per_seq = rows_per_batch // tm if carried else 1
    nseq_tile = tm // seq_rows
    nseq_total = rows // rows_per_batch
    if carried:
        state_map = lambda i: (i // tiles_per_seq, 0, 0)
        vt_shape = (nseq_total, nh, rows_per_batch // vt_tile, HEAD_DIM + ONES_ROWS, vt_tile)
        vt_map = lambda i: (i // tiles_per_seq, 0, ((i % tiles_per_seq) * tm) // vt_tile, 0,
                            (((i % tiles_per_seq) * tm) % vt_tile) // tm)
    else:
        state_map = lambda i: (i, 0, 0)
        vt_shape = (n_tiles, nh, 1, HEAD_DIM + ONES_ROWS, tm)
        vt_map = lambda i: (i, 0, 0, 0, 0)
    const2 = lambda i: (0, 0)
    const3 = lambda i: (0, 0, 0)
    row_map = lambda i: (i, 0)
    resident = dict(pipeline_mode=pl.Buffered(1))
    in_specs = [
        pl.BlockSpec((tm, d_model), row_map),
        pl.BlockSpec(w_in16.shape, const2, **resident),
        pl.BlockSpec((tm, LANES), (lambda i: (i % tiles_per_seq, 0)) if carried else const2),
        pl.BlockSpec((tm, LANES), (lambda i: (i % tiles_per_seq, 0)) if carried else const2),
        pl.BlockSpec(conv_w.shape, const2),
        pl.BlockSpec(conv_b.shape, const2),
        pl.BlockSpec(wg.shape, const3, **resident),
        pl.BlockSpec(b_a.shape, const2),
        pl.BlockSpec(b_i.shape, const2),
        pl.BlockSpec(lru_lambda.shape, const2),
        pl.BlockSpec((nseq_tile, 1, lw), state_map),
        pl.BlockSpec((nseq_tile, CONV_WIDTH - 1, lw), state_map),
    ]
    out_shape = (
        jax.ShapeDtypeStruct((rows, aw), BF16),
        jax.ShapeDtypeStruct(vt_shape[:3] + (HEAD_DIM, vt_shape[4]), BF16),
        jax.ShapeDtypeStruct((rows, aw), F32),
        jax.ShapeDtypeStruct((rows, aw), BF16),
        jax.ShapeDtypeStruct((rows, aw), F32),
        jax.ShapeDtypeStruct(vt_shape, BF16),
        jax.ShapeDtypeStruct((rows, lw), BF16),
        jax.ShapeDtypeStruct((nseq_total, 1, lw), F32),
        jax.ShapeDtypeStruct((nseq_total, CONV_WIDTH - 1, lw), F32),
    )
    out_specs = (
        pl.BlockSpec((tm, aw), row_map),
        pl.BlockSpec((None, nh, None, HEAD_DIM, tm), vt_map),
        pl.BlockSpec((tm, aw), row_map),
        pl.BlockSpec((tm, aw), row_map),
        pl.BlockSpec((tm, aw), row_map),
        pl.BlockSpec((None, nh, None, HEAD_DIM + ONES_ROWS, tm), vt_map),
        pl.BlockSpec((tm, lw), row_map),
        pl.BlockSpec((nseq_tile, 1, lw), state_map),
        pl.BlockSpec((nseq_tile, CONV_WIDTH - 1, lw), state_map),
    )
    scratch = [
        pltpu.VMEM((nseq_tile, SUBLANES + seq_rows, lw), F32),
        pltpu.VMEM((tm, lw), F32),
        pltpu.VMEM((tm, lw), BF16),
        pltpu.VMEM((tm, 2 * lw), F32),
        pltpu.VMEM((tm, lw), F32),
        pltpu.VMEM((1, lw), F32),
    ]
    kern = functools.partial(_in_lru_kernel, tm=tm, seq_rows=seq_rows, tiles_per_seq=tiles_per_seq,
                             aw=aw, lw=lw)
    return pl.pallas_call(
        kern, grid=(n_tiles,), in_specs=in_specs, out_specs=out_specs, out_shape=out_shape,
        scratch_shapes=scratch,
        compiler_params=pltpu.CompilerParams(dimension_semantics=("arbitrary",),
                                             vmem_limit_bytes=VMEM_LIMIT_BYTES),
        name="in_lru",
    )(x2d, w_in16, cos, sin, conv_w, conv_b, wg, b_a, b_i, lru_lambda, h0, c0)


def _attn_kernel(lam_ref, g_ref, bias_ref, qt_ref, k_ref, vt_ref, o_ref, s_scr, m_scr, acc_scr,
                 *, t, n_q, lam_init):
    def scores(i, j, slot):
        qt = qt_ref[i]
        dim = lax.broadcasted_iota(jnp.int32, (HEAD_DIM, t), 0)
        zero = jnp.zeros_like(qt)
        qq = jnp.concatenate([jnp.where(dim < HALF_DIM, qt, zero),
                              jnp.where(dim >= HALF_DIM, qt, zero)], axis=1)
        kj = k_ref[pl.ds(pl.multiple_of(j * t, t), t), :]
        s_scr[slot] = _dot(kj, qq)

    def softmax_pv(j, slot, masked):
        s = s_scr[slot]
        if masked:
            s = s + bias_ref[...]
        m = m_scr[...]
        m_new = jnp.maximum(m, jnp.max(s, axis=0, keepdims=True))
        alpha = jnp.exp2(m - m_new)
        p = jnp.exp2(s - m_new)
        m_scr[...] = m_new
        acc_scr[...] = alpha * acc_scr[...] + _dot(vt_ref[j], p.astype(BF16))

    def reset():
        m_scr[...] = jnp.full(m_scr.shape, NEG_INF, F32)
        acc_scr[...] = jnp.zeros(acc_scr.shape, F32)

    def finish(i):
        acc = acc_scr[...]
        o = acc[0:HEAD_DIM] / acc[HEAD_DIM:HEAD_DIM + 1]
        o = o[:, 0:t] - lam_ref[0:1, 0:1] * o[:, t:2 * t]
        ms = jnp.mean(o * o, axis=0, keepdims=True)
        o = o * lax.rsqrt(ms + RMS_EPS) * g_ref[...] * (1.0 - lam_init)
        o_ref[pl.ds(pl.multiple_of(i * t, t), t), :] = o.T.astype(BF16)
        reset()

    def following(i, j):
        edge = j == i
        return jnp.where(edge, i + 1, i), jnp.where(edge, 0, j + 1)

    def pair(_, ij):
        ia, ja = ij
        ib, jb = following(ia, ja)
        ic, jc = following(ib, jb)
        icc = jnp.minimum(ic, n_q - 1)
        edge_a = ja == ia
        edge_b = jb == ib

        def body(mask_a, mask_b):
            scores(ib, jb, 1)
            softmax_pv(ja, 0, mask_a)
            if mask_a:
                finish(ia)
            scores(icc, jc, 0)
            softmax_pv(jb, 1, mask_b)
            if mask_b:
                finish(ib)

        pl.when(edge_a)(lambda: body(True, False))
        pl.when(edge_b)(lambda: body(False, True))
        pl.when(jnp.logical_not(jnp.logical_or(edge_a, edge_b)))(lambda: body(False, False))
        return ic, jc

    reset()
    scores(0, 0, 0)
    n_items = n_q * (n_q + 1) // 2
    lax.fori_loop(0, n_items // 2, pair, (jnp.int32(0), jnp.int32(0)))
    if n_items % 2:
        softmax_pv(n_q - 1, 0, True)
        finish(n_q - 1)


def _attn_call(lam_tile, subln_g, qt, k16, vt, *, t, lam_init):
    b, s, aw = k16.shape
    nh = aw // HEAD_DIM
    key_chunk = jnp.arange(t, dtype=jnp.int32)[:, None] // CHUNK
    qry_chunk = (jnp.arange(2 * t, dtype=jnp.int32)[None, :] % t) // CHUNK
    bias = jnp.where(key_chunk <= qry_chunk, 0.0, NEG_INF).astype(F32)
    seq_spec = pl.BlockSpec((None, s, HEAD_DIM), lambda bi, hi: (bi, 0, hi))
    const = lambda bi, hi: (0, 0)
    return pl.pallas_call(
        functools.partial(_attn_kernel, t=t, n_q=s // t, lam_init=lam_init),
        grid=(b, nh),
        in_specs=[pl.BlockSpec((SUBLANES, LANES), const),
                  pl.BlockSpec((HEAD_DIM, 1), const),
                  pl.BlockSpec((t, 2 * t), const),
                  pl.BlockSpec((None, None, s // t, HEAD_DIM, t), lambda bi, hi: (bi, hi, 0, 0, 0)),
                  seq_spec,
                  pl.BlockSpec((None, None, s // t, HEAD_DIM + ONES_ROWS, t),
                               lambda bi, hi: (bi, hi, 0, 0, 0))],
        out_specs=seq_spec,
        out_shape=jax.ShapeDtypeStruct((b, s, aw), BF16),
        scratch_shapes=[pltpu.VMEM((2, t, 2 * t), F32),
                        pltpu.VMEM((1, 2 * t), F32),
                        pltpu.VMEM((HEAD_DIM + ONES_ROWS, 2 * t), F32)],
        compiler_params=pltpu.CompilerParams(
            dimension_semantics=("arbitrary", "arbitrary"),
            vmem_limit_bytes=VMEM_LIMIT_BYTES),
        name="attn",
    )(lam_tile, subln_g.reshape(HEAD_DIM, 1), bias, qt, k16, vt)


def _dec_attn_kernel(lam_ref, g_ref, bias_ref, q_ref, kn_ref, vn_ref, ck_ref, cv_ref, o_ref,
                     *, t, nh, kc, lam_init):
    rows = 2 * nh * t
    p_len = ck_ref.shape[0]
    q = q_ref[...]
    lane = lax.broadcasted_iota(jnp.int32, (t, HEAD_DIM), 1)
    parts = []
    for c in range(2):
        keep = (lane < HALF_DIM) if c == 0 else (lane >= HALF_DIM)
        for h in range(nh):
            qh = q[:, h * HEAD_DIM:(h + 1) * HEAD_DIM]
            parts.append(jnp.where(keep, qh, jnp.zeros_like(qh)))
    qall = jnp.concatenate(parts, axis=0)

    def scores(kf):
        s = _dot_nt(qall, kf.astype(BF16))
        return s + bias_ref[:, 0:s.shape[1]]

    def update(carry, s, vf):
        m, l, acc = carry
        m_new = jnp.maximum(m, jnp.max(s, axis=-1, keepdims=True))
        alpha = jnp.exp2(m - m_new)
        p = jnp.exp2(s - m_new)
        l = alpha * l + jnp.sum(p, axis=-1, keepdims=True)
        acc = alpha * acc + _dot(p.astype(BF16), vf.astype(BF16))
        return m_new, l, acc

    carry = (jnp.full((rows, 1), NEG_INF, F32), jnp.zeros((rows, 1), F32),
             jnp.zeros((rows, HEAD_DIM), F32))
    n_chunks = p_len // kc
    s_next = scores(ck_ref[0:kc].reshape(kc * nh, HEAD_DIM))
    for j in range(n_chunks):
        s_cur = s_next
        if j + 1 < n_chunks:
            s_next = scores(ck_ref[(j + 1) * kc:(j + 2) * kc].reshape(kc * nh, HEAD_DIM))
        else:
            s_next = scores(kn_ref[...].reshape(t * nh, HEAD_DIM))
        carry = update(carry, s_cur, cv_ref[j * kc:(j + 1) * kc].reshape(kc * nh, HEAD_DIM))
    _, l, acc = update(carry, s_next, vn_ref[...].reshape(t * nh, HEAD_DIM))

    o = acc / l
    half = rows // 2
    o = o[0:half] - lam_ref[0:1, 0:1] * o[half:rows]
    ms = jnp.mean(o * o, axis=-1, keepdims=True)
    o = o * lax.rsqrt(ms + RMS_EPS) * g_ref[...] * (1.0 - lam_init)
    for h in range(nh):
        o_ref[:, h * HEAD_DIM:(h + 1) * HEAD_DIM] = o[h * t:(h + 1) * t].astype(BF16)


def _dec_attn_call(lam_tile, subln_g, q16, kn32, vn32, cache_k, cache_v, *, layer, lam_init):
    bd, t, aw = q16.shape
    _, p, nh, hd = cache_k.shape
    new_spec = pl.BlockSpec((None, t, nh, hd), lambda b: (b, 0, 0, 0))
    cache_spec = pl.BlockSpec((None, p, nh, hd), lambda b: (layer * bd + b, 0, 0, 0))
    io_spec = pl.BlockSpec((None, t, aw), lambda b: (b, 0, 0))
    kc = _pick_tile(p, 256)
    ncol = max(kc, t) * nh
    row_head = (jnp.arange(2 * nh * t, dtype=jnp.int32)[:, None] // t) % nh
    col_head = jnp.arange(ncol, dtype=jnp.int32)[None, :] % nh
    bias = jnp.where(row_head == col_head, 0.0, NEG_INF).astype(F32)
    return pl.pallas_call(
        functools.partial(_dec_attn_kernel, t=t, nh=nh, kc=kc, lam_init=lam_init),
        grid=(bd,),
        in_specs=[pl.BlockSpec((SUBLANES, LANES), lambda b: (0, 0)),
                  pl.BlockSpec((1, HEAD_DIM), lambda b: (0, 0)),
                  pl.BlockSpec(bias.shape, lambda b: (0, 0)),
                  io_spec, new_spec, new_spec, cache_spec, cache_spec],
        out_specs=io_spec,
        out_shape=jax.ShapeDtypeStruct((bd, t, aw), BF16),
        compiler_params=pltpu.CompilerParams(dimension_semantics=("arbitrary",),
                                             vmem_limit_bytes=VMEM_LIMIT_BYTES),
        name="dec_attn",
    )(lam_tile, subln_g, bias, q16, kn32, vn32, cache_k, cache_v)


def _out_mlp_kernel(x_ref, oa_ref, ol_ref, wo_ref, g1_ref, b1_ref, wu_ref, wd_ref, g2_ref, b2_ref,
                    y_ref, *rest, alpha, emit_weights):
    j = pl.program_id(1)
    if emit_weights:
        wu16_ref, wd16_ref, x1_scr = rest
    else:
        (x1_scr,) = rest

    @pl.when(j == 0)
    def _():
        cat = jnp.concatenate([oa_ref[...], ol_ref[...]], axis=1)
        x1 = _layer_norm(alpha * x_ref[...] + _dot(cat, wo_ref[...]), g1_ref[...], b1_ref[...])
        x1_scr[...] = x1.astype(BF16)
        y_ref[...] = alpha * x1

    wu = wu_ref[...]
    wd = wd_ref[...]
    if emit_weights:
        wu = wu.astype(BF16)
        wd = wd.astype(BF16)
        wu16_ref[...] = wu
        wd16_ref[...] = wd
    hid = jnp.square(jnp.maximum(_dot(x1_scr[...], wu), 0.0)).astype(BF16)
    y_ref[...] += _dot(hid, wd)

    @pl.when(j == pl.num_programs(1) - 1)
    def _():
        y_ref[...] = _layer_norm(y_ref[...], g2_ref[...], b2_ref[...])


def _out_mlp_call(x2d, oa16, ol16, w_out16, g1, b1, w_up, w_down, g2, b2, *, tm, tf, alpha):
    rows, d = x2d.shape
    aw = oa16.shape[1]
    lw = ol16.shape[1]
    d_ff = w_up.shape[1]
    emit_weights = w_up.dtype != BF16
    assert not emit_weights or rows == tm
    row = lambda i, j: (i, 0)
    const = lambda i, j: (0, 0)
    up_spec = pl.BlockSpec((d, tf), lambda i, j: (0, j))
    down_spec = pl.BlockSpec((tf, d), lambda i, j: (j, 0))
    out_specs = [pl.BlockSpec((tm, d), row)]
    out_shape = [jax.ShapeDtypeStruct((rows, d), F32)]
    if emit_weights:
        out_specs += [up_spec, down_spec]
        out_shape += [jax.ShapeDtypeStruct(w_up.shape, BF16), jax.ShapeDtypeStruct(w_down.shape, BF16)]
    outs = pl.pallas_call(
        functools.partial(_out_mlp_kernel, alpha=alpha, emit_weights=emit_weights),
        grid=(rows // tm, d_ff // tf),
        in_specs=[pl.BlockSpec((tm, d), row),
                  pl.BlockSpec((tm, aw), row),
                  pl.BlockSpec((tm, lw), row),
                  pl.BlockSpec(w_out16.shape, const, pipeline_mode=pl.Buffered(1)),
                  pl.BlockSpec((1, d), const), pl.BlockSpec((1, d), const),
                  up_spec, down_spec,
                  pl.BlockSpec((1, d), const), pl.BlockSpec((1, d), const)],
        out_specs=out_specs,
        out_shape=out_shape,
        scratch_shapes=[pltpu.VMEM((tm, d), BF16)],
        compiler_params=pltpu.CompilerParams(dimension_semantics=("arbitrary", "arbitrary"),
                                             vmem_limit_bytes=VMEM_LIMIT_BYTES),
        name="out_mlp",
    )(x2d, oa16, ol16, w_out16, g1, b1, w_up, w_down, g2, b2)
    return outs if emit_weights else outs[0]


def _rope_tables(pos):
    inv = ROPE_THETA ** (-jnp.arange(0, HALF_DIM, 2, dtype=F32) / HALF_DIM)
    ang = pos.astype(F32)[:, None] * inv[None, :]
    cos = jnp.tile(jnp.cos(ang), (1, 4))
    sin = jnp.sin(ang)
    sin = jnp.tile(jnp.concatenate([-sin, sin], axis=-1), (1, 2))
    return cos, sin


def _gate_weights(w_a, w_i):
    per = MXU_DIM // w_a.shape[-1]

    def dense(w):
        nb, bd, _ = w.shape
        w4 = w.reshape(nb // per, per, bd, bd)
        eye = jnp.eye(per, dtype=w.dtype)
        return jnp.einsum('gpij,pq->gpiqj', w4, eye).reshape(nb // per, per * bd, per * bd)

    return jnp.concatenate([dense(w_a), dense(w_i)], axis=-1).astype(BF16)


def _pick_tile(n, pref):
    t = min(n, pref)
    while n % t:
        t //= 2
    return t


def kernel(x_prompt, x_sample, cache_k, cache_v, state_h, state_conv, w_in, lambda_q1, lambda_k1, lambda_q2, lambda_k2, subln_g, conv_w, conv_b, w_rg_a, b_rg_a, w_rg_i, b_rg_i, lru_lambda, w_out, ln1_g, ln1_b, w_up, w_down, ln2_g, ln2_b):
    depth = w_in.shape[0]
    b, s, d = x_prompt.shape
    bd, t, _ = x_sample.shape
    p = cache_k.shape[2]
    lw = conv_w.shape[-1]
    aw = (w_in.shape[-1] - 2 * lw) // 3
    nh = aw // HEAD_DIM
    alpha = (2.0 * depth) ** 0.25

    cos_p, sin_p = _rope_tables(jnp.arange(s, dtype=jnp.int32))
    cos_s, sin_s = _rope_tables(p + jnp.arange(t, dtype=jnp.int32))
    cos_s = jnp.tile(cos_s, (bd, 1))
    sin_s = jnp.tile(sin_s, (bd, 1))

    tm_a = _pick_tile(s, 256)
    tq = _pick_tile(s, 512)
    tm_c = _pick_tile(b * s, 512)
    tf = _pick_tile(w_up.shape[-1], 1024)
    tf_s = _pick_tile(w_up.shape[-1], 512)

    yp = x_prompt.reshape(b * s, d)
    ys = x_sample.reshape(bd * t, d)
    outs = [[] for _ in range(8)]
    for l in range(depth):
        lam0 = _lambda_init(l)
        w_in16 = w_in[l].astype(BF16)
        w_out16 = w_out[l].astype(BF16)
        wg = _gate_weights(w_rg_a[l], w_rg_i[l])
        row = lambda v: v[l].reshape(1, -1)
        lam_tile = _lam_call(row(lambda_q1), row(lambda_k1), row(lambda_q2), row(lambda_k2), lam0)
        g_sub = row(subln_g)
        lru_args = (conv_w[l], row(conv_b), wg, row(b_rg_a), row(b_rg_i), row(lru_lambda))
        ln1_args = (w_out16, row(ln1_g), row(ln1_b))
        ln2_args = (row(ln2_g), row(ln2_b))

        q16, _, k32, _, v32, _, ol16, hn, cn = _in_lru_call(
            ys, w_in16, cos_s, sin_s, *lru_args,
            state_h[l].reshape(bd, 1, lw), state_conv[l],
            tm=bd * t, seq_rows=t, rows_per_batch=t, vt_tile=bd * t)
        oa16 = _dec_attn_call(lam_tile, g_sub, q16.reshape(bd, t, aw),
                              k32.reshape(bd, t, nh, HEAD_DIM), v32.reshape(bd, t, nh, HEAD_DIM),
                              cache_k.reshape(depth * bd, p, nh, HEAD_DIM),
                              cache_v.reshape(depth * bd, p, nh, HEAD_DIM), layer=l, lam_init=lam0)
        ys, w_up16, w_down16 = _out_mlp_call(
            ys, oa16.reshape(bd * t, aw), ol16, *ln1_args, w_up[l], w_down[l], *ln2_args,
            tm=bd * t, tf=tf_s, alpha=alpha)
        outs[4].append(k32.reshape(bd, t, nh, HEAD_DIM))
        outs[5].append(v32.reshape(bd, t, nh, HEAD_DIM))
        outs[6].append(hn.reshape(bd, lw))
        outs[7].append(cn)

        _, qt16, k32, k16, v32, vt16, ol16, hp, cp = _in_lru_call(
            yp, w_in16, cos_p, sin_p, *lru_args,
            jnp.zeros((b, 1, lw), F32), jnp.zeros((b, CONV_WIDTH - 1, lw), F32),
            tm=tm_a, seq_rows=tm_a, rows_per_batch=s, vt_tile=tq)
        oa16 = _attn_call(lam_tile, g_sub, qt16, k16.reshape(b, s, aw), vt16, t=tq, lam_init=lam0)
        yp = _out_mlp_call(yp, oa16.reshape(b * s, aw), ol16, *ln1_args, w_up16, w_down16, *ln2_args,
                           tm=tm_c, tf=tf, alpha=alpha)
        outs[0].append(k32.reshape(b, s, nh, HEAD_DIM))
        outs[1].append(v32.reshape(b, s, nh, HEAD_DIM))
        outs[2].append(hp.reshape(b, lw))
        outs[3].append(cp)

    stk = [jnp.stack(o) for o in outs]
    return (yp.reshape(b, s, d), ys.reshape(bd, t, d), stk[0], stk[1], stk[2], stk[3],
            stk[4], stk[5], stk[6], stk[7])
```

```python
import functools
import math

import jax
import jax.numpy as jnp
from jax import lax
from jax.experimental import pallas as pl
from jax.experimental.pallas import tpu as pltpu

CHUNK = 64
HEAD_DIM = 128
HALF_DIM = HEAD_DIM // 2
LRU_BLOCKS = 16
CONV_WIDTH = 4
LRU_C = 8.0
ROPE_THETA = 10000.0
LN_EPS = 1e-5
RMS_EPS = 1e-5
NEG_INF = -1e30

SUBLANES = 8
LANES = 128
MXU_DIM = 256
ONES_ROWS = 16
VMEM_LIMIT_BYTES = 56 * 1024 * 1024

Q_SCALE = HALF_DIM ** -0.5 * math.log2(math.e)

BF16 = jnp.bfloat16
F32 = jnp.float32


def _lambda_init(layer):
    return 0.8 - 0.6 * math.exp(-0.3 * layer)


def _dot(a, b):
    return jnp.dot(a, b, preferred_element_type=F32)


def _dot_nt(a, b):
    return lax.dot_general(a, b, (((1,), (1,)), ((), ())), preferred_element_type=F32)


def _layer_norm(z, g, b):
    mu = jnp.mean(z, axis=-1, keepdims=True)
    d = z - mu
    var = jnp.mean(d * d, axis=-1, keepdims=True)
    return d * lax.rsqrt(var + LN_EPS) * g + b


def _lam_kernel(q1_ref, k1_ref, q2_ref, k2_ref, o_ref, *, lam_init):
    s1 = jnp.sum(q1_ref[...] * k1_ref[...], axis=-1, keepdims=True)
    s2 = jnp.sum(q2_ref[...] * k2_ref[...], axis=-1, keepdims=True)
    lam = jnp.exp(s1) - jnp.exp(s2) + lam_init
    o_ref[...] = jnp.broadcast_to(lam, o_ref.shape)


def _lam_call(lq1, lk1, lq2, lk2, lam_init):
    return pl.pallas_call(
        functools.partial(_lam_kernel, lam_init=lam_init),
        out_shape=jax.ShapeDtypeStruct((SUBLANES, LANES), F32),
        name="lam",
    )(lq1, lk1, lq2, lk2)


def _group_scan(a, u, sub):
    for s in (1, 2, 4):
        keep = sub >= s
        a_sh = pltpu.roll(a, s, axis=0)
        u_sh = pltpu.roll(u, s, axis=0)
        u = u + jnp.where(keep, a * u_sh, 0.0)
        a = a * jnp.where(keep, a_sh, 1.0)
    return a, u


def _in_lru_kernel(x_ref, w_ref, cos_ref, sin_ref, cw_ref, cb_ref, wg_ref, ba_ref, bi_ref,
                   lam_ref, h0_ref, c0_ref, *outs_and_scratch,
                   tm, seq_rows, tiles_per_seq, aw, lw):
    carried = seq_rows == tm
    nseq = tm // seq_rows
    gps = seq_rows // SUBLANES
    i = pl.program_id(0)
    xp_scr, xc32_scr, xc16_scr, gate_scr, g_scr, h_scr = outs_and_scratch[-6:]
    if carried:
        qt_ref, k32_ref, k16_ref, v32_ref, vt_ref, ol_ref, hl_ref, cl_ref = outs_and_scratch[:-6]
    else:
        q_ref, k32_ref, v32_ref, ol_ref, hl_ref, cl_ref = outs_and_scratch[:-6]

    xb16 = x_ref[...].astype(BF16)

    cos = cos_ref[...]
    sin = sin_ref[...]
    lane = lax.broadcasted_iota(jnp.int32, (tm, LANES), 1)
    first_half = (lane % HALF_DIM) < (HALF_DIM // 2)

    def rope(xs):
        rot = jnp.where(first_half, pltpu.roll(xs, LANES - HALF_DIM // 2, axis=1),
                        pltpu.roll(xs, HALF_DIM // 2, axis=1))
        return xs * cos + rot * sin

    pieces_per_seg = aw // MXU_DIM
    n_pieces = 3 * pieces_per_seg

    def qkv_piece(p):
        seg, c0 = p // pieces_per_seg, (p % pieces_per_seg) * MXU_DIM
        pre = _dot(xb16, w_ref[:, seg * aw + c0:seg * aw + c0 + MXU_DIM])
        for h in range(MXU_DIM // LANES):
            sl = slice(c0 + h * LANES, c0 + (h + 1) * LANES)
            ph = pre[:, h * LANES:(h + 1) * LANES]
            head = c0 // LANES + h
            if seg == 0:
                qs = rope(ph) * Q_SCALE
                if carried:
                    qt_ref[head] = qs.T.astype(BF16)
                else:
                    q_ref[:, sl] = qs.astype(BF16)
            elif seg == 1:
                kr = rope(ph)
                k32_ref[:, sl] = kr
                if carried:
                    k16_ref[:, sl] = kr.astype(BF16)
            else:
                v32_ref[:, sl] = ph
                if carried:
                    vt_ref[head, 0:HEAD_DIM, :] = ph.T.astype(BF16)
                    vt_ref[head, HEAD_DIM:HEAD_DIM + ONES_ROWS, :] = jnp.ones((ONES_ROWS, tm), BF16)

    xbp = _dot(xb16, w_ref[:, 3 * aw:3 * aw + lw])
    g_scr[...] = _dot(xb16, w_ref[:, 3 * aw + lw:3 * aw + 2 * lw])

    if carried:
        @pl.when(i % tiles_per_seq == 0)
        def _():
            xp_scr[0, 0:SUBLANES, :] = jnp.zeros((SUBLANES, lw), F32)
            h_scr[...] = jnp.zeros_like(h_scr)
    else:
        for sq in range(nseq):
            xp_scr[sq, SUBLANES - (CONV_WIDTH - 1):SUBLANES, :] = c0_ref[sq]
    for sq in range(nseq):
        xp_scr[sq, SUBLANES:, :] = xbp[sq * seq_rows:(sq + 1) * seq_rows, :]

    sub = lax.broadcasted_iota(jnp.int32, (SUBLANES, lw), 0)
    cwb = [jnp.broadcast_to(cw_ref[j:j + 1, :], (SUBLANES, lw)) for j in range(CONV_WIDTH)]
    cbb = jnp.broadcast_to(cb_ref[...], (SUBLANES, lw))
    for sq in range(nseq):
        prev = xp_scr[sq, 0:SUBLANES, :]
        prev_rolled = [pltpu.roll(prev, d, axis=0) for d in range(1, CONV_WIDTH)]
        for gg in range(gps):
            r0 = SUBLANES + gg * SUBLANES
            cur = xp_scr[sq, r0:r0 + SUBLANES, :]
            cur_rolled = [pltpu.roll(cur, d, axis=0) for d in range(1, CONV_WIDTH)]
            xc = cbb + cwb[CONV_WIDTH - 1] * cur
            for d in range(1, CONV_WIDTH):
                shifted = jnp.where(sub >= d, cur_rolled[d - 1], prev_rolled[d - 1])
                xc = xc + cwb[CONV_WIDTH - 1 - d] * shifted
            prev_rolled = cur_rolled
            g0 = (sq * gps + gg) * SUBLANES
            xc32_scr[g0:g0 + SUBLANES, :] = xc
            xc16_scr[g0:g0 + SUBLANES, :] = xc.astype(BF16)

    for sq in range(nseq):
        cl_ref[sq] = xp_scr[sq, SUBLANES + seq_rows - (CONV_WIDTH - 1):SUBLANES + seq_rows, :]
    if carried:
        xp_scr[0, 0:SUBLANES, :] = xp_scr[0, seq_rows:seq_rows + SUBLANES, :]

    for blk in range(lw // MXU_DIM):
        sl = slice(blk * MXU_DIM, (blk + 1) * MXU_DIM)
        gate_scr[:, 2 * blk * MXU_DIM:2 * (blk + 1) * MXU_DIM] = _dot(xc16_scr[:, sl], wg_ref[blk])

    nl = -lam_ref[...]
    softplus = jnp.maximum(nl, 0.0) + jnp.log1p(jnp.exp(-jnp.abs(nl)))
    decay = jnp.broadcast_to(-LRU_C * softplus, (SUBLANES, lw))
    ba = jnp.broadcast_to(ba_ref[...], (SUBLANES, lw))
    bi = jnp.broadcast_to(bi_ref[...], (SUBLANES, lw))

    def sigmoid(z):
        return 0.5 * jnp.tanh(0.5 * z) + 0.5

    n_groups = nseq * gps
    pieces_done = 0
    after = None
    for sq in range(nseq):
        hin = h_scr[...] if carried else h0_ref[sq]
        for gg in range(gps):
            while pieces_done * n_groups < (sq * gps + gg) * n_pieces:
                qkv_piece(pieces_done)
                pieces_done += 1
            g0 = (sq * gps + gg) * SUBLANES
            rows = slice(g0, g0 + SUBLANES)
            pre = gate_scr[rows, :]
            ra = jnp.concatenate(
                [pre[:, 2 * b * MXU_DIM:(2 * b + 1) * MXU_DIM] for b in range(lw // MXU_DIM)], axis=1)
            ri = jnp.concatenate(
                [pre[:, (2 * b + 1) * MXU_DIM:(2 * b + 2) * MXU_DIM] for b in range(lw // MXU_DIM)], axis=1)
            if after is not None:
                ra = ra + after
                ri = ri + after
            r = sigmoid(ra + ba)
            ig = sigmoid(ri + bi)
            log_a = decay * r
            a = jnp.exp(log_a)
            y = (1.0 + a * a) * jnp.tanh(-log_a)
            root = jnp.where(y > 0.0, y * lax.rsqrt(y), 0.0)
            u = root * (ig * xc32_scr[rows, :])
            acum, ucum = _group_scan(a, u, sub)
            hs = acum * hin + ucum
            hin = hs[SUBLANES - 1:SUBLANES, :]
            out = hs * jax.nn.gelu(g_scr[rows, :])
            ol_ref[rows, :] = out.astype(BF16)
            after = pltpu.bitcast(
                lax.shift_right_logical(pltpu.bitcast(out, jnp.uint32), jnp.uint32(32)), F32)
        hl_ref[sq] = hin
        if carried:
            h_scr[...] = hin
    while pieces_done < n_pieces:
        qkv_piece(pieces_done)
        pieces_done += 1


def _in_lru_call(x2d, w_in16, cos, sin, conv_w, conv_b, wg, b_a, b_i, lru_lambda, h0, c0,
                 *, tm, seq_rows, rows_per_batch, vt_tile):
    rows, d_model = x2d.shape
    aw = (w_in16.shape[1] - 2 * conv_w.shape[1]) // 3
    lw = conv_w.shape[1]
    nh = aw // HEAD_DIM
    carried = seq_rows == tm
    n_tiles = rows // tm
    tiles_per_seq = rows_per_batch // tm if carried else 1
    nseq_tile = tm // seq_rows
    nseq_total = rows // rows_per_batch
    if carried:
        state_map = lambda i: (i // tiles_per_seq, 0, 0)
    else:
        state_map = lambda i: (i, 0, 0)
    const2 = lambda i: (0, 0)
    const3 = lambda i: (0, 0, 0)
    row_map = lambda i: (i, 0)
    resident = dict(pipeline_mode=pl.Buffered(1))
    in_specs = [
        pl.BlockSpec((tm, d_model), row_map),
        pl.BlockSpec(w_in16.shape, const2, **resident),
        pl.BlockSpec((tm, LANES), (lambda i: (i % tiles_per_seq, 0)) if carried else const2),
        pl.BlockSpec((tm, LANES), (lambda i: (i % tiles_per_seq, 0)) if carried else const2),
        pl.BlockSpec(conv_w.shape, const2),
        pl.BlockSpec(conv_b.shape, const2),
        pl.BlockSpec(wg.shape, const3, **resident),
        pl.BlockSpec(b_a.shape, const2),
        pl.BlockSpec(b_i.shape, const2),
        pl.BlockSpec(lru_lambda.shape, const2),
        pl.BlockSpec((nseq_tile, 1, lw), state_map),
        pl.BlockSpec((nseq_tile, CONV_WIDTH - 1, lw), state_map),
    ]
    rows_aw = lambda dt: (jax.ShapeDtypeStruct((rows, aw), dt), pl.BlockSpec((tm, aw), row_map))
    k32_out, v32_out = rows_aw(F32), rows_aw(F32)
    tail_outs = [
        (jax.ShapeDtypeStruct((rows, lw), BF16), pl.BlockSpec((tm, lw), row_map)),
        (jax.ShapeDtypeStruct((nseq_total, 1, lw), F32), pl.BlockSpec((nseq_tile, 1, lw), state_map)),
        (jax.ShapeDtypeStruct((nseq_total, CONV_WIDTH - 1, lw), F32),
         pl.BlockSpec((nseq_tile, CONV_WIDTH - 1, lw), state_map)),
    ]
    if carried:
        t_lead = (nseq_total, nh, rows_per_batch // vt_tile)
        t_map = lambda i: (i // tiles_per_seq, 0, ((i % tiles_per_seq) * tm) // vt_tile, 0,
                           (((i % tiles_per_seq) * tm) % vt_tile) // tm)
        transposed = lambda r: (jax.ShapeDtypeStruct(t_lead + (r, vt_tile), BF16),
                                pl.BlockSpec((None, nh, None, r, tm), t_map))
        outs = [transposed(HEAD_DIM), k32_out, rows_aw(BF16), v32_out,
                transposed(HEAD_DIM + ONES_ROWS)] + tail_outs
    else:
        outs = [rows_aw(BF16), k32_out, v32_out] + tail_outs
    out_shape = tuple(o[0] for o in outs)
    out_specs = tuple(o[1] for o in outs)
    scratch = [
        pltpu.VMEM((nseq_tile, SUBLANES + seq_rows, lw), F32),
        pltpu.VMEM((tm, lw), F32),
        pltpu.VMEM((tm, lw), BF16),
        pltpu.VMEM((tm, 2 * lw), F32),
        pltpu.VMEM((tm, lw), F32),
        pltpu.VMEM((1, lw), F32),
    ]
    kern = functools.partial(_in_lru_kernel, tm=tm, seq_rows=seq_rows, tiles_per_seq=tiles_per_seq,
                             aw=aw, lw=lw)
    return pl.pallas_call(
        kern, grid=(n_tiles,), in_specs=in_specs, out_specs=out_specs, out_shape=out_shape,
        scratch_shapes=scratch,
        compiler_params=pltpu.CompilerParams(dimension_semantics=("arbitrary",),
                                             vmem_limit_bytes=VMEM_LIMIT_BYTES),
        name="in_lru",
    )(x2d, w_in16, cos, sin, conv_w, conv_b, wg, b_a, b_i, lru_lambda, h0, c0)


def _attn_kernel(lam_ref, g_ref, bias_ref, qt_ref, k_ref, vt_ref, *rest, t, n_q, lam_init):
    n_w = (len(rest) - 4) // 2
    w32_refs, o_ref, w16_refs = rest[:n_w], rest[n_w], rest[n_w + 1:2 * n_w + 1]
    s_scr, m_scr, acc_scr = rest[2 * n_w + 1:]
    for w32_ref, w16_ref in zip(w32_refs, w16_refs):
        w16_ref[...] = w32_ref[...].astype(BF16)

    def scores(i, j, slot):
        qt = qt_ref[i]
        dim = lax.broadcasted_iota(jnp.int32, (HEAD_DIM, t), 0)
        zero = jnp.zeros_like(qt)
        qq = jnp.concatenate([jnp.where(dim < HALF_DIM, qt, zero),
                              jnp.where(dim >= HALF_DIM, qt, zero)], axis=1)
        kj = k_ref[pl.ds(pl.multiple_of(j * t, t), t), :]
        s_scr[slot] = _dot(kj, qq)

    def softmax_pv(j, slot, masked):
        s = s_scr[slot]
        if masked:
            s = s + bias_ref[...]
        m = m_scr[...]
        m_new = jnp.maximum(m, jnp.max(s, axis=0, keepdims=True))
        alpha = jnp.exp2(m - m_new)
        p = jnp.exp2(s - m_new)
        m_scr[...] = m_new
        acc_scr[...] = alpha * acc_scr[...] + _dot(vt_ref[j], p.astype(BF16))

    def reset():
        m_scr[...] = jnp.full(m_scr.shape, NEG_INF, F32)
        acc_scr[...] = jnp.zeros(acc_scr.shape, F32)

    def finish(i):
        acc = acc_scr[...]
        o = acc[0:HEAD_DIM] / acc[HEAD_DIM:HEAD_DIM + 1]
        o = o[:, 0:t] - lam_ref[0:1, 0:1] * o[:, t:2 * t]
        ms = jnp.mean(o * o, axis=0, keepdims=True)
        o = o * lax.rsqrt(ms + RMS_EPS) * g_ref[...] * (1.0 - lam_init)
        o_ref[pl.ds(pl.multiple_of(i * t, t), t), :] = o.T.astype(BF16)
        reset()

    def following(i, j):
        edge = j == i
        return jnp.where(edge, i + 1, i), jnp.where(edge, 0, j + 1)

    def pair(_, ij):
        ia, ja = ij
        ib, jb = following(ia, ja)
        ic, jc = following(ib, jb)
        icc = jnp.minimum(ic, n_q - 1)
        edge_a = ja == ia
        edge_b = jb == ib

        def body(mask_a, mask_b):
            scores(ib, jb, 1)
            softmax_pv(ja, 0, mask_a)
            if mask_a:
                finish(ia)
            scores(icc, jc, 0)
            softmax_pv(jb, 1, mask_b)
            if mask_b:
                finish(ib)

        pl.when(edge_a)(lambda: body(True, False))
        pl.when(edge_b)(lambda: body(False, True))
        pl.when(jnp.logical_not(jnp.logical_or(edge_a, edge_b)))(lambda: body(False, False))
        return ic, jc

    reset()
    scores(0, 0, 0)
    n_items = n_q * (n_q + 1) // 2
    lax.fori_loop(0, n_items // 2, pair, (jnp.int32(0), jnp.int32(0)))
    if n_items % 2:
        softmax_pv(n_q - 1, 0, True)
        finish(n_q - 1)


def _attn_call(lam_tile, subln_g, qt, k16, vt, cast_rows, cast_cols, *, t, lam_init):
    b, s, aw = k16.shape
    nh = aw // HEAD_DIM
    steps = b * nh
    key_chunk = jnp.arange(t, dtype=jnp.int32)[:, None] // CHUNK
    qry_chunk = (jnp.arange(2 * t, dtype=jnp.int32)[None, :] % t) // CHUNK
    bias = jnp.where(key_chunk <= qry_chunk, 0.0, NEG_INF).astype(F32)
    seq_spec = pl.BlockSpec((None, s, HEAD_DIM), lambda bi, hi: (bi, 0, hi))
    const = lambda bi, hi: (0, 0)
    w_specs = [pl.BlockSpec((w.shape[0] // steps, w.shape[1]), lambda bi, hi: (bi * nh + hi, 0))
               for w in cast_rows]
    w_specs += [pl.BlockSpec((w.shape[0], w.shape[1] // steps), lambda bi, hi: (0, bi * nh + hi))
                for w in cast_cols]
    weights = list(cast_rows) + list(cast_cols)
    return pl.pallas_call(
        functools.partial(_attn_kernel, t=t, n_q=s // t, lam_init=lam_init),
        grid=(b, nh),
        in_specs=[pl.BlockSpec((SUBLANES, LANES), const),
                  pl.BlockSpec((HEAD_DIM, 1), const),
                  pl.BlockSpec((t, 2 * t), const),
                  pl.BlockSpec((None, None, s // t, HEAD_DIM, t), lambda bi, hi: (bi, hi, 0, 0, 0)),
                  seq_spec,
                  pl.BlockSpec((None, None, s // t, HEAD_DIM + ONES_ROWS, t),
                               lambda bi, hi: (bi, hi, 0, 0, 0))] + w_specs,
        out_specs=[seq_spec] + w_specs,
        out_shape=[jax.ShapeDtypeStruct((b, s, aw), BF16)]
        + [jax.ShapeDtypeStruct(w.shape, BF16) for w in weights],
        scratch_shapes=[pltpu.VMEM((2, t, 2 * t), F32),
                        pltpu.VMEM((1, 2 * t), F32),
                        pltpu.VMEM((HEAD_DIM + ONES_ROWS, 2 * t), F32)],
        compiler_params=pltpu.CompilerParams(
            dimension_semantics=("arbitrary", "arbitrary"),
            vmem_limit_bytes=VMEM_LIMIT_BYTES),
        name="attn",
    )(lam_tile, subln_g.reshape(HEAD_DIM, 1), bias, qt, k16, vt, *weights)


def _dec_attn_kernel(lam_ref, g_ref, bias_ref, q_ref, kn_ref, vn_ref, ck_ref, cv_ref, o_ref,
                     *, t, nh, kc, lam_init):
    rows = 2 * nh * t
    p_len = ck_ref.shape[0]
    q = q_ref[...]
    lane = lax.broadcasted_iota(jnp.int32, (t, HEAD_DIM), 1)
    parts = []
    for c in range(2):
        keep = (lane < HALF_DIM) if c == 0 else (lane >= HALF_DIM)
        for h in range(nh):
            qh = q[:, h * HEAD_DIM:(h + 1) * HEAD_DIM]
            parts.append(jnp.where(keep, qh, jnp.zeros_like(qh)))
    qall = jnp.concatenate(parts, axis=0)

    def scores(kf):
        s = _dot_nt(qall, kf.astype(BF16))
        return s + bias_ref[:, 0:s.shape[1]]

    def update(carry, s, vf):
        m, l, acc = carry
        m_new = jnp.maximum(m, jnp.max(s, axis=-1, keepdims=True))
        alpha = jnp.exp2(m - m_new)
        p = jnp.exp2(s - m_new)
        l = alpha * l + jnp.sum(p, axis=-1, keepdims=True)
        acc = alpha * acc + _dot(p.astype(BF16), vf.astype(BF16))
        return m_new, l, acc

    carry = (jnp.full((rows, 1), NEG_INF, F32), jnp.zeros((rows, 1), F32),
             jnp.zeros((rows, HEAD_DIM), F32))
    n_chunks = p_len // kc
    s_next = scores(ck_ref[0:kc].reshape(kc * nh, HEAD_DIM))
    for j in range(n_chunks):
        s_cur = s_next
        if j + 1 < n_chunks:
            s_next = scores(ck_ref[(j + 1) * kc:(j + 2) * kc].reshape(kc * nh, HEAD_DIM))
        else:
            s_next = scores(kn_ref[...].reshape(t * nh, HEAD_DIM))
        carry = update(carry, s_cur, cv_ref[j * kc:(j + 1) * kc].reshape(kc * nh, HEAD_DIM))
    _, l, acc = update(carry, s_next, vn_ref[...].reshape(t * nh, HEAD_DIM))

    o = acc / l
    half = rows // 2
    o = o[0:half] - lam_ref[0:1, 0:1] * o[half:rows]
    ms = jnp.mean(o * o, axis=-1, keepdims=True)
    o = o * lax.rsqrt(ms + RMS_EPS) * g_ref[...] * (1.0 - lam_init)
    for h in range(nh):
        o_ref[:, h * HEAD_DIM:(h + 1) * HEAD_DIM] = o[h * t:(h + 1) * t].astype(BF16)


def _dec_attn_call(lam_tile, subln_g, q16, kn32, vn32, cache_k, cache_v, *, layer, lam_init):
    bd, t, aw = q16.shape
    _, p, nh, hd = cache_k.shape
    new_spec = pl.BlockSpec((None, t, nh, hd), lambda b: (b, 0, 0, 0))
    cache_spec = pl.BlockSpec((None, p, nh, hd), lambda b: (layer * bd + b, 0, 0, 0))
    io_spec = pl.BlockSpec((None, t, aw), lambda b: (b, 0, 0))
    kc = _pick_tile(p, 256)
    ncol = max(kc, t) * nh
    row_head = (jnp.arange(2 * nh * t, dtype=jnp.int32)[:, None] // t) % nh
    col_head = jnp.arange(ncol, dtype=jnp.int32)[None, :] % nh
    bias = jnp.where(row_head == col_head, 0.0, NEG_INF).astype(F32)
    return pl.pallas_call(
        functools.partial(_dec_attn_kernel, t=t, nh=nh, kc=kc, lam_init=lam_init),
        grid=(bd,),
        in_specs=[pl.BlockSpec((SUBLANES, LANES), lambda b: (0, 0)),
                  pl.BlockSpec((1, HEAD_DIM), lambda b: (0, 0)),
                  pl.BlockSpec(bias.shape, lambda b: (0, 0)),
                  io_spec, new_spec, new_spec, cache_spec, cache_spec],
        out_specs=io_spec,
        out_shape=jax.ShapeDtypeStruct((bd, t, aw), BF16),
        compiler_params=pltpu.CompilerParams(dimension_semantics=("arbitrary",),
                                             vmem_limit_bytes=VMEM_LIMIT_BYTES),
        name="dec_attn",
    )(lam_tile, subln_g, bias, q16, kn32, vn32, cache_k, cache_v)


def _out_mlp_kernel(x_ref, oa_ref, ol_ref, wo_ref, g1_ref, b1_ref, wu_ref, wd_ref, g2_ref, b2_ref,
                    y_ref, x1_scr, *, alpha):
    j = pl.program_id(1)

    @pl.when(j == 0)
    def _():
        cat = jnp.concatenate([oa_ref[...], ol_ref[...]], axis=1)
        x1 = _layer_norm(alpha * x_ref[...] + _dot(cat, wo_ref[...]), g1_ref[...], b1_ref[...])
        x1_scr[...] = x1.astype(BF16)
        y_ref[...] = alpha * x1

    hid = jnp.square(jnp.maximum(_dot(x1_scr[...], wu_ref[...]), 0.0)).astype(BF16)
    y_ref[...] += _dot(hid, wd_ref[...])

    @pl.when(j == pl.num_programs(1) - 1)
    def _():
        y_ref[...] = _layer_norm(y_ref[...], g2_ref[...], b2_ref[...])


def _out_mlp_call(x2d, oa16, ol16, w_out16, g1, b1, w_up16, w_down16, g2, b2, *, tm, tf, alpha):
    rows, d = x2d.shape
    aw = oa16.shape[1]
    lw = ol16.shape[1]
    d_ff = w_up16.shape[1]
    row = lambda i, j: (i, 0)
    const = lambda i, j: (0, 0)
    return pl.pallas_call(
        functools.partial(_out_mlp_kernel, alpha=alpha),
        grid=(rows // tm, d_ff // tf),
        in_specs=[pl.BlockSpec((tm, d), row),
                  pl.BlockSpec((tm, aw), row),
                  pl.BlockSpec((tm, lw), row),
                  pl.BlockSpec(w_out16.shape, const, pipeline_mode=pl.Buffered(1)),
                  pl.BlockSpec((1, d), const), pl.BlockSpec((1, d), const),
                  pl.BlockSpec((d, tf), lambda i, j: (0, j)),
                  pl.BlockSpec((tf, d), lambda i, j: (j, 0)),
                  pl.BlockSpec((1, d), const), pl.BlockSpec((1, d), const)],
        out_specs=pl.BlockSpec((tm, d), row),
        out_shape=jax.ShapeDtypeStruct((rows, d), F32),
        scratch_shapes=[pltpu.VMEM((tm, d), BF16)],
        compiler_params=pltpu.CompilerParams(dimension_semantics=("arbitrary", "arbitrary"),
                                             vmem_limit_bytes=VMEM_LIMIT_BYTES),
        name="out_mlp",
    )(x2d, oa16, ol16, w_out16, g1, b1, w_up16, w_down16, g2, b2)


def _rope_tables(pos):
    inv = ROPE_THETA ** (-jnp.arange(0, HALF_DIM, 2, dtype=F32) / HALF_DIM)
    ang = pos.astype(F32)[:, None] * inv[None, :]
    cos = jnp.tile(jnp.cos(ang), (1, 4))
    sin = jnp.sin(ang)
    sin = jnp.tile(jnp.concatenate([-sin, sin], axis=-1), (1, 2))
    return cos, sin


def _gate_weights(w_a, w_i):
    per = MXU_DIM // w_a.shape[-1]

    def dense(w):
        nb, bd, _ = w.shape
        w4 = w.reshape(nb // per, per, bd, bd)
        eye = jnp.eye(per, dtype=w.dtype)
        return jnp.einsum('gpij,pq->gpiqj', w4, eye).reshape(nb // per, per * bd, per * bd)

    return jnp.concatenate([dense(w_a), dense(w_i)], axis=-1).astype(BF16)


def _pick_tile(n, pref):
    t = min(n, pref)
    while n % t:
        t //= 2
    return t


def kernel(x_prompt, x_sample, cache_k, cache_v, state_h, state_conv, w_in, lambda_q1, lambda_k1, lambda_q2, lambda_k2, subln_g, conv_w, conv_b, w_rg_a, b_rg_a, w_rg_i, b_rg_i, lru_lambda, w_out, ln1_g, ln1_b, w_up, w_down, ln2_g, ln2_b):
    depth = w_in.shape[0]
    b, s, d = x_prompt.shape
    bd, t, _ = x_sample.shape
    p = cache_k.shape[2]
    lw = conv_w.shape[-1]
    aw = (w_in.shape[-1] - 2 * lw) // 3
    nh = aw // HEAD_DIM
    alpha = (2.0 * depth) ** 0.25

    cos_p, sin_p = _rope_tables(jnp.arange(s, dtype=jnp.int32))
    cos_s, sin_s = _rope_tables(p + jnp.arange(t, dtype=jnp.int32))
    cos_s = jnp.tile(cos_s, (bd, 1))
    sin_s = jnp.tile(sin_s, (bd, 1))

    tm_a = _pick_tile(s, 256)
    tq = _pick_tile(s, 512)
    tm_c = _pick_tile(b * s, 512)
    tf = _pick_tile(w_up.shape[-1], 1024)

    yp = x_prompt.reshape(b * s, d)
    ys = x_sample.reshape(bd * t, d)
    outs = [[] for _ in range(8)]
    for l in range(depth):
        lam0 = _lambda_init(l)
        w_in16 = w_in[l].astype(BF16)
        wg = _gate_weights(w_rg_a[l], w_rg_i[l])
        row = lambda v: v[l].reshape(1, -1)
        lam_tile = _lam_call(row(lambda_q1), row(lambda_k1), row(lambda_q2), row(lambda_k2), lam0)
        g_sub = row(subln_g)
        lru_args = (conv_w[l], row(conv_b), wg, row(b_rg_a), row(b_rg_i), row(lru_lambda))

        qt16, k32, k16, v32, vt16, ol16, hp, cp = _in_lru_call(
            yp, w_in16, cos_p, sin_p, *lru_args,
            jnp.zeros((b, 1, lw), F32), jnp.zeros((b, CONV_WIDTH - 1, lw), F32),
            tm=tm_a, seq_rows=tm_a, rows_per_batch=s, vt_tile=tq)
        oa16, w_out16, w_down16, w_up16 = _attn_call(
            lam_tile, g_sub, qt16, k16.reshape(b, s, aw), vt16, (w_out[l], w_down[l]), (w_up[l],),
            t=tq, lam_init=lam0)
        ln_args = (w_out16, row(ln1_g), row(ln1_b), w_up16, w_down16, row(ln2_g), row(ln2_b))
        yp = _out_mlp_call(yp, oa16.reshape(b * s, aw), ol16, *ln_args, tm=tm_c, tf=tf, alpha=alpha)
        outs[0].append(k32.reshape(b, s, nh, HEAD_DIM))
        outs[1].append(v32.reshape(b, s, nh, HEAD_DIM))
        outs[2].append(hp.reshape(b, lw))
        outs[3].append(cp)

        q16, k32, v32, ol16, hn, cn = _in_lru_call(
            ys, w_in16, cos_s, sin_s, *lru_args,
            state_h[l].reshape(bd, 1, lw), state_conv[l],
            tm=bd * t, seq_rows=t, rows_per_batch=t, vt_tile=bd * t)
        oa16 = _dec_attn_call(lam_tile, g_sub, q16.reshape(bd, t, aw),
                              k32.reshape(bd, t, nh, HEAD_DIM), v32.reshape(bd, t, nh, HEAD_DIM),
                              cache_k.reshape(depth * bd, p, nh, HEAD_DIM),
                              cache_v.reshape(depth * bd, p, nh, HEAD_DIM), layer=l, lam_init=lam0)
        ys = _out_mlp_call(ys, oa16.reshape(bd * t, aw), ol16, *ln_args,
                           tm=_pick_tile(bd * t, 512), tf=tf, alpha=alpha)
        outs[4].append(k32.reshape(bd, t, nh, HEAD_DIM))
        outs[5].append(v32.reshape(bd, t, nh, HEAD_DIM))
        outs[6].append(hn.reshape(bd, lw))
        outs[7].append(cn)

    stk = [jnp.stack(o) for o in outs]
    return (yp.reshape(b, s, d), ys.reshape(bd, t, d), stk[0], stk[1], stk[2], stk[3],
            stk[4], stk[5], stk[6], stk[7])
```

```python
import functools
import math

import jax
import jax.numpy as jnp
from jax import lax
from jax.experimental import pallas as pl
from jax.experimental.pallas import tpu as pltpu

CHUNK = 64
HEAD_DIM = 128
HALF_DIM = HEAD_DIM // 2
LRU_BLOCKS = 16
CONV_WIDTH = 4
LRU_C = 8.0
ROPE_THETA = 10000.0
LN_EPS = 1e-5
RMS_EPS = 1e-5
NEG_INF = -1e30

SUBLANES = 8
LANES = 128
MXU_DIM = 256
ONES_ROWS = 16
VMEM_LIMIT_BYTES = 56 * 1024 * 1024

Q_SCALE = HALF_DIM ** -0.5 * math.log2(math.e)

BF16 = jnp.bfloat16
F32 = jnp.float32


def _lambda_init(layer):
    return 0.8 - 0.6 * math.exp(-0.3 * layer)


def _dot(a, b):
    return jnp.dot(a, b, preferred_element_type=F32)


def _dot_nt(a, b):
    return lax.dot_general(a, b, (((1,), (1,)), ((), ())), preferred_element_type=F32)


def _layer_norm(z, g, b):
    mu = jnp.mean(z, axis=-1, keepdims=True)
    d = z - mu
    var = jnp.mean(d * d, axis=-1, keepdims=True)
    return d * lax.rsqrt(var + LN_EPS) * g + b


def _lam_kernel(q1_ref, k1_ref, q2_ref, k2_ref, o_ref, *, lam_init):
    s1 = jnp.sum(q1_ref[...] * k1_ref[...], axis=-1, keepdims=True)
    s2 = jnp.sum(q2_ref[...] * k2_ref[...], axis=-1, keepdims=True)
    lam = jnp.exp(s1) - jnp.exp(s2) + lam_init
    o_ref[...] = jnp.broadcast_to(lam, o_ref.shape)


def _lam_call(lq1, lk1, lq2, lk2, lam_init):
    return pl.pallas_call(
        functools.partial(_lam_kernel, lam_init=lam_init),
        out_shape=jax.ShapeDtypeStruct((SUBLANES, LANES), F32),
        name="lam",
    )(lq1, lk1, lq2, lk2)


def _group_scan(a, u, sub):
    for s in (1, 2, 4):
        keep = sub >= s
        a_sh = pltpu.roll(a, s, axis=0)
        u_sh = pltpu.roll(u, s, axis=0)
        u = u + jnp.where(keep, a * u_sh, 0.0)
        a = a * jnp.where(keep, a_sh, 1.0)
    return a, u


def _in_lru_kernel(x_ref, w_ref, cos_ref, sin_ref, cw_ref, cb_ref, wg_ref, ba_ref, bi_ref,
                   lam_ref, h0_ref, c0_ref, *outs_and_scratch,
                   tm, seq_rows, tiles_per_seq, aw, lw):
    carried = seq_rows == tm
    nseq = tm // seq_rows
    gps = seq_rows // SUBLANES
    i = pl.program_id(0)
    xp_scr, xc32_scr, xc16_scr, gate_scr, g_scr, h_scr = outs_and_scratch[-6:]
    if carried:
        qt_ref, k32_ref, k16_ref, v32_ref, vt_ref, ol_ref, hl_ref, cl_ref = outs_and_scratch[:-6]
    else:
        q_ref, k32_ref, v32_ref, ol_ref, hl_ref, cl_ref = outs_and_scratch[:-6]

    if carried:
        xb16 = x_ref[...].astype(BF16)
        proj = lambda c0, width: _dot(xb16, w_ref[:, c0:c0 + width])
    else:
        proj = lambda c0, width: w_ref[:, c0:c0 + width]

    cos = cos_ref[...]
    sin = sin_ref[...]
    lane = lax.broadcasted_iota(jnp.int32, (tm, LANES), 1)
    first_half = (lane % HALF_DIM) < (HALF_DIM // 2)

    def rope(xs):
        rot = jnp.where(first_half, pltpu.roll(xs, LANES - HALF_DIM // 2, axis=1),
                        pltpu.roll(xs, HALF_DIM // 2, axis=1))
        return xs * cos + rot * sin

    pieces_per_seg = aw // MXU_DIM
    n_pieces = 3 * pieces_per_seg

    def qkv_piece(p):
        seg, c0 = p // pieces_per_seg, (p % pieces_per_seg) * MXU_DIM
        pre = proj(seg * aw + c0, MXU_DIM)
        for h in range(MXU_DIM // LANES):
            sl = slice(c0 + h * LANES, c0 + (h + 1) * LANES)
            ph = pre[:, h * LANES:(h + 1) * LANES]
            head = c0 // LANES + h
            if seg == 0:
                qs = rope(ph) * Q_SCALE
                if carried:
                    qt_ref[head] = qs.T.astype(BF16)
                else:
                    q_ref[:, sl] = qs.astype(BF16)
            elif seg == 1:
                kr = rope(ph)
                k32_ref[:, sl] = kr
                if carried:
                    k16_ref[:, sl] = kr.astype(BF16)
            else:
                v32_ref[:, sl] = ph
                if carried:
                    vt_ref[head, 0:HEAD_DIM, :] = ph.T.astype(BF16)
                    vt_ref[head, HEAD_DIM:HEAD_DIM + ONES_ROWS, :] = jnp.ones((ONES_ROWS, tm), BF16)

    xbp = proj(3 * aw, lw)
    g_scr[...] = proj(3 * aw + lw, lw)

    if carried:
        @pl.when(i % tiles_per_seq == 0)
        def _():
            xp_scr[0, 0:SUBLANES, :] = jnp.zeros((SUBLANES, lw), F32)
            h_scr[...] = jnp.zeros_like(h_scr)
    else:
        for sq in range(nseq):
            xp_scr[sq, SUBLANES - (CONV_WIDTH - 1):SUBLANES, :] = c0_ref[sq]
    for sq in range(nseq):
        xp_scr[sq, SUBLANES:, :] = xbp[sq * seq_rows:(sq + 1) * seq_rows, :]

    sub = lax.broadcasted_iota(jnp.int32, (SUBLANES, lw), 0)
    cwb = [jnp.broadcast_to(cw_ref[j:j + 1, :], (SUBLANES, lw)) for j in range(CONV_WIDTH)]
    cbb = jnp.broadcast_to(cb_ref[...], (SUBLANES, lw))
    for sq in range(nseq):
        prev = xp_scr[sq, 0:SUBLANES, :]
        prev_rolled = [pltpu.roll(prev, d, axis=0) for d in range(1, CONV_WIDTH)]
        for gg in range(gps):
            r0 = SUBLANES + gg * SUBLANES
            cur = xp_scr[sq, r0:r0 + SUBLANES, :]
            cur_rolled = [pltpu.roll(cur, d, axis=0) for d in range(1, CONV_WIDTH)]
            xc = cbb + cwb[CONV_WIDTH - 1] * cur
            for d in range(1, CONV_WIDTH):
                shifted = jnp.where(sub >= d, cur_rolled[d - 1], prev_rolled[d - 1])
                xc = xc + cwb[CONV_WIDTH - 1 - d] * shifted
            prev_rolled = cur_rolled
            g0 = (sq * gps + gg) * SUBLANES
            xc32_scr[g0:g0 + SUBLANES, :] = xc
            xc16_scr[g0:g0 + SUBLANES, :] = xc.astype(BF16)

    for sq in range(nseq):
        cl_ref[sq] = xp_scr[sq, SUBLANES + seq_rows - (CONV_WIDTH - 1):SUBLANES + seq_rows, :]
    if carried:
        xp_scr[0, 0:SUBLANES, :] = xp_scr[0, seq_rows:seq_rows + SUBLANES, :]

    for blk in range(lw // MXU_DIM):
        sl = slice(blk * MXU_DIM, (blk + 1) * MXU_DIM)
        gate_scr[:, 2 * blk * MXU_DIM:2 * (blk + 1) * MXU_DIM] = _dot(xc16_scr[:, sl], wg_ref[blk])

    nl = -lam_ref[...]
    softplus = jnp.maximum(nl, 0.0) + jnp.log1p(jnp.exp(-jnp.abs(nl)))
    decay = jnp.broadcast_to(-LRU_C * softplus, (SUBLANES, lw))
    ba = jnp.broadcast_to(ba_ref[...], (SUBLANES, lw))
    bi = jnp.broadcast_to(bi_ref[...], (SUBLANES, lw))

    def sigmoid(z):
        return 0.5 * jnp.tanh(0.5 * z) + 0.5

    n_groups = nseq * gps
    pieces_done = 0
    after = None
    for sq in range(nseq):
        hin = h_scr[...] if carried else h0_ref[sq]
        for gg in range(gps):
            while pieces_done * n_groups < (sq * gps + gg) * n_pieces:
                qkv_piece(pieces_done)
                pieces_done += 1
            g0 = (sq * gps + gg) * SUBLANES
            rows = slice(g0, g0 + SUBLANES)
            pre = gate_scr[rows, :]
            ra = jnp.concatenate(
                [pre[:, 2 * b * MXU_DIM:(2 * b + 1) * MXU_DIM] for b in range(lw // MXU_DIM)], axis=1)
            ri = jnp.concatenate(
                [pre[:, (2 * b + 1) * MXU_DIM:(2 * b + 2) * MXU_DIM] for b in range(lw // MXU_DIM)], axis=1)
            if after is not None:
                ra = ra + after
                ri = ri + after
            r = sigmoid(ra + ba)
            ig = sigmoid(ri + bi)
            log_a = decay * r
            a = jnp.exp(log_a)
            y = (1.0 + a * a) * jnp.tanh(-log_a)
            root = jnp.where(y > 0.0, y * lax.rsqrt(y), 0.0)
            u = root * (ig * xc32_scr[rows, :])
            acum, ucum = _group_scan(a, u, sub)
            hs = acum * hin + ucum
            hin = hs[SUBLANES - 1:SUBLANES, :]
            out = hs * jax.nn.gelu(g_scr[rows, :])
            ol_ref[rows, :] = out.astype(BF16)
            after = pltpu.bitcast(
                lax.shift_right_logical(pltpu.bitcast(out, jnp.uint32), jnp.uint32(32)), F32)
        hl_ref[sq] = hin
        if carried:
            h_scr[...] = hin
    while pieces_done < n_pieces:
        qkv_piece(pieces_done)
        pieces_done += 1


def _cast_proj_kernel(x_ref, w_ref, w16_ref, p_ref):
    w16 = w_ref[...].astype(BF16)
    w16_ref[...] = w16
    p_ref[...] = _dot(x_ref[...].astype(BF16), w16)


def _cast_proj_call(x2d, w_in):
    rows, d_model = x2d.shape
    n = w_in.shape[1]
    col = pl.BlockSpec((d_model, MXU_DIM), lambda p: (0, p))
    return pl.pallas_call(
        _cast_proj_kernel,
        grid=(n // MXU_DIM,),
        in_specs=[pl.BlockSpec((rows, d_model), lambda p: (0, 0)), col],
        out_specs=[col, pl.BlockSpec((rows, MXU_DIM), lambda p: (0, p))],
        out_shape=[jax.ShapeDtypeStruct(w_in.shape, BF16), jax.ShapeDtypeStruct((rows, n), F32)],
        compiler_params=pltpu.CompilerParams(dimension_semantics=("arbitrary",),
                                             vmem_limit_bytes=VMEM_LIMIT_BYTES),
        name="cast_proj",
    )(x2d, w_in)


def _in_lru_call(x2d, w_in16, cos, sin, conv_w, conv_b, wg, b_a, b_i, lru_lambda, h0, c0,
                 *, tm, seq_rows, rows_per_batch, vt_tile):
    rows, d_model = x2d.shape
    aw = (w_in16.shape[1] - 2 * conv_w.shape[1]) // 3
    lw = conv_w.shape[1]
    nh = aw // HEAD_DIM
    carried = seq_rows == tm
    n_tiles = rows // tm
    tiles_per_seq = rows_per_batch // tm if carried else 1
    nseq_tile = tm // seq_rows
    nseq_total = rows // rows_per_batch
    if carried:
        state_map = lambda i: (i // tiles_per_seq, 0, 0)
    else:
        state_map = lambda i: (i, 0, 0)
    const2 = lambda i: (0, 0)
    const3 = lambda i: (0, 0, 0)
    row_map = lambda i: (i, 0)
    resident = dict(pipeline_mode=pl.Buffered(1))
    in_specs = [
        pl.BlockSpec((tm, d_model), row_map),
        pl.BlockSpec(w_in16.shape, const2, **resident) if carried
        else pl.BlockSpec((tm, w_in16.shape[1]), row_map),
        pl.BlockSpec((tm, LANES), (lambda i: (i % tiles_per_seq, 0)) if carried else const2),
        pl.BlockSpec((tm, LANES), (lambda i: (i % tiles_per_seq, 0)) if carried else const2),
        pl.BlockSpec(conv_w.shape, const2),
        pl.BlockSpec(conv_b.shape, const2),
        pl.BlockSpec(wg.shape, const3, **resident),
        pl.BlockSpec(b_a.shape, const2),
        pl.BlockSpec(b_i.shape, const2),
        pl.BlockSpec(lru_lambda.shape, const2),
        pl.BlockSpec((nseq_tile, 1, lw), state_map),
        pl.BlockSpec((nseq_tile, CONV_WIDTH - 1, lw), state_map),
    ]
    rows_aw = lambda dt: (jax.ShapeDtypeStruct((rows, aw), dt), pl.BlockSpec((tm, aw), row_map))
    k32_out, v32_out = rows_aw(F32), rows_aw(F32)
    tail_outs = [
        (jax.ShapeDtypeStruct((rows, lw), BF16), pl.BlockSpec((tm, lw), row_map)),
        (jax.ShapeDtypeStruct((nseq_total, 1, lw), F32), pl.BlockSpec((nseq_tile, 1, lw), state_map)),
        (jax.ShapeDtypeStruct((nseq_total, CONV_WIDTH - 1, lw), F32),
         pl.BlockSpec((nseq_tile, CONV_WIDTH - 1, lw), state_map)),
    ]
    if carried:
        t_lead = (nseq_total, nh, rows_per_batch // vt_tile)
        t_map = lambda i: (i // tiles_per_seq, 0, ((i % tiles_per_seq) * tm) // vt_tile, 0,
                           (((i % tiles_per_seq) * tm) % vt_tile) // tm)
        transposed = lambda r: (jax.ShapeDtypeStruct(t_lead + (r, vt_tile), BF16),
                                pl.BlockSpec((None, nh, None, r, tm), t_map))
        outs = [transposed(HEAD_DIM), k32_out, rows_aw(BF16), v32_out,
                transposed(HEAD_DIM + ONES_ROWS)] + tail_outs
    else:
        outs = [rows_aw(BF16), k32_out, v32_out] + tail_outs
    out_shape = tuple(o[0] for o in outs)
    out_specs = tuple(o[1] for o in outs)
    scratch = [
        pltpu.VMEM((nseq_tile, SUBLANES + seq_rows, lw), F32),
        pltpu.VMEM((tm, lw), F32),
        pltpu.VMEM((tm, lw), BF16),
        pltpu.VMEM((tm, 2 * lw), F32),
        pltpu.VMEM((tm, lw), F32),
        pltpu.VMEM((1, lw), F32),
    ]
    kern = functools.partial(_in_lru_kernel, tm=tm, seq_rows=seq_rows, tiles_per_seq=tiles_per_seq,
                             aw=aw, lw=lw)
    return pl.pallas_call(
        kern, grid=(n_tiles,), in_specs=in_specs, out_specs=out_specs, out_shape=out_shape,
        scratch_shapes=scratch,
        compiler_params=pltpu.CompilerParams(dimension_semantics=("arbitrary",),
                                             vmem_limit_bytes=VMEM_LIMIT_BYTES),
        name="in_lru",
    )(x2d, w_in16, cos, sin, conv_w, conv_b, wg, b_a, b_i, lru_lambda, h0, c0)


def _attn_kernel(lam_ref, g_ref, bias_ref, qt_ref, k_ref, vt_ref, *rest, t, n_q, lam_init):
    n_w = (len(rest) - 4) // 2
    w32_refs, o_ref, w16_refs = rest[:n_w], rest[n_w], rest[n_w + 1:2 * n_w + 1]
    s_scr, m_scr, acc_scr = rest[2 * n_w + 1:]

    def scores(i, j, slot):
        qt = qt_ref[i]
        dim = lax.broadcasted_iota(jnp.int32, (HEAD_DIM, t), 0)
        zero = jnp.zeros_like(qt)
        qq = jnp.concatenate([jnp.where(dim < HALF_DIM, qt, zero),
                              jnp.where(dim >= HALF_DIM, qt, zero)], axis=1)
        kj = k_ref[pl.ds(pl.multiple_of(j * t, t), t), :]
        s_scr[slot] = _dot(kj, qq)

    def softmax_pv(j, slot, masked):
        s = s_scr[slot]
        if masked:
            s = s + bias_ref[...]
        m = m_scr[...]
        m_new = jnp.maximum(m, jnp.max(s, axis=0, keepdims=True))
        alpha = jnp.exp2(m - m_new)
        p = jnp.exp2(s - m_new)
        m_scr[...] = m_new
        acc_scr[...] = alpha * acc_scr[...] + _dot(vt_ref[j], p.astype(BF16))

    def reset():
        m_scr[...] = jnp.full(m_scr.shape, NEG_INF, F32)
        acc_scr[...] = jnp.zeros(acc_scr.shape, F32)

    def finish(i):
        acc = acc_scr[...]
        o = acc[0:HEAD_DIM] / acc[HEAD_DIM:HEAD_DIM + 1]
        o = o[:, 0:t] - lam_ref[0:1, 0:1] * o[:, t:2 * t]
        ms = jnp.mean(o * o, axis=0, keepdims=True)
        o = o * lax.rsqrt(ms + RMS_EPS) * g_ref[...] * (1.0 - lam_init)
        o_ref[pl.ds(pl.multiple_of(i * t, t), t), :] = o.T.astype(BF16)
        reset()

    def following(i, j):
        edge = j == i
        return jnp.where(edge, i + 1, i), jnp.where(edge, 0, j + 1)

    def pair(_, ij):
        ia, ja = ij
        ib, jb = following(ia, ja)
        ic, jc = following(ib, jb)
        icc = jnp.minimum(ic, n_q - 1)
        edge_a = ja == ia
        edge_b = jb == ib

        def body(mask_a, mask_b):
            scores(ib, jb, 1)
            softmax_pv(ja, 0, mask_a)
            if mask_a:
                finish(ia)
            scores(icc, jc, 0)
            softmax_pv(jb, 1, mask_b)
            if mask_b:
                finish(ib)

        pl.when(edge_a)(lambda: body(True, False))
        pl.when(edge_b)(lambda: body(False, True))
        pl.when(jnp.logical_not(jnp.logical_or(edge_a, edge_b)))(lambda: body(False, False))
        return ic, jc

    reset()
    scores(0, 0, 0)
    for w32_ref, w16_ref in zip(w32_refs, w16_refs):
        w16_ref[...] = w32_ref[...].astype(BF16)
    n_items = n_q * (n_q + 1) // 2
    lax.fori_loop(0, n_items // 2, pair, (jnp.int32(0), jnp.int32(0)))
    if n_items % 2:
        softmax_pv(n_q - 1, 0, True)
        finish(n_q - 1)


def _attn_call(lam_tile, subln_g, qt, k16, vt, cast_rows, cast_cols, *, t, lam_init):
    b, s, aw = k16.shape
    nh = aw // HEAD_DIM
    steps = b * nh
    key_chunk = jnp.arange(t, dtype=jnp.int32)[:, None] // CHUNK
    qry_chunk = (jnp.arange(2 * t, dtype=jnp.int32)[None, :] % t) // CHUNK
    bias = jnp.where(key_chunk <= qry_chunk, 0.0, NEG_INF).astype(F32)
    seq_spec = pl.BlockSpec((None, s, HEAD_DIM), lambda bi, hi: (bi, 0, hi))
    const = lambda bi, hi: (0, 0)
    w_specs = [pl.BlockSpec((w.shape[0] // steps, w.shape[1]), lambda bi, hi: (bi * nh + hi, 0))
               for w in cast_rows]
    w_specs += [pl.BlockSpec((w.shape[0], w.shape[1] // steps), lambda bi, hi: (0, bi * nh + hi))
                for w in cast_cols]
    weights = list(cast_rows) + list(cast_cols)
    return pl.pallas_call(
        functools.partial(_attn_kernel, t=t, n_q=s // t, lam_init=lam_init),
        grid=(b, nh),
        in_specs=[pl.BlockSpec((SUBLANES, LANES), const),
                  pl.BlockSpec((HEAD_DIM, 1), const),
                  pl.BlockSpec((t, 2 * t), const),
                  pl.BlockSpec((None, None, s // t, HEAD_DIM, t), lambda bi, hi: (bi, hi, 0, 0, 0)),
                  seq_spec,
                  pl.BlockSpec((None, None, s // t, HEAD_DIM + ONES_ROWS, t),
                               lambda bi, hi: (bi, hi, 0, 0, 0))] + w_specs,
        out_specs=[seq_spec] + w_specs,
        out_shape=[jax.ShapeDtypeStruct((b, s, aw), BF16)]
        + [jax.ShapeDtypeStruct(w.shape, BF16) for w in weights],
        scratch_shapes=[pltpu.VMEM((2, t, 2 * t), F32),
                        pltpu.VMEM((1, 2 * t), F32),
                        pltpu.VMEM((HEAD_DIM + ONES_ROWS, 2 * t), F32)],
        compiler_params=pltpu.CompilerParams(
            dimension_semantics=("arbitrary", "arbitrary"),
            vmem_limit_bytes=VMEM_LIMIT_BYTES),
        name="attn",
    )(lam_tile, subln_g.reshape(HEAD_DIM, 1), bias, qt, k16, vt, *weights)


def _dec_attn_kernel(lam_ref, g_ref, bias_ref, q_ref, kn_ref, vn_ref, ck_ref, cv_ref, o_ref,
                     *, t, nh, kc, lam_init):
    rows = 2 * nh * t
    p_len = ck_ref.shape[0]
    q = q_ref[...]
    lane = lax.broadcasted_iota(jnp.int32, (t, HEAD_DIM), 1)
    parts = []
    for c in range(2):
        keep = (lane < HALF_DIM) if c == 0 else (lane >= HALF_DIM)
        for h in range(nh):
            qh = q[:, h * HEAD_DIM:(h + 1) * HEAD_DIM]
            parts.append(jnp.where(keep, qh, jnp.zeros_like(qh)))
    qall = jnp.concatenate(parts, axis=0)

    def scores(kf):
        s = _dot_nt(qall, kf.astype(BF16))
        return s + bias_ref[:, 0:s.shape[1]]

    def update(carry, s, vf):
        m, l, acc = carry
        m_new = jnp.maximum(m, jnp.max(s, axis=-1, keepdims=True))
        alpha = jnp.exp2(m - m_new)
        p = jnp.exp2(s - m_new)
        l = alpha * l + jnp.sum(p, axis=-1, keepdims=True)
        acc = alpha * acc + _dot(p.astype(BF16), vf.astype(BF16))
        return m_new, l, acc

    carry = (jnp.full((rows, 1), NEG_INF, F32), jnp.zeros((rows, 1), F32),
             jnp.zeros((rows, HEAD_DIM), F32))
    n_chunks = p_len // kc
    s_next = scores(ck_ref[0:kc].reshape(kc * nh, HEAD_DIM))
    for j in range(n_chunks):
        s_cur = s_next
        if j + 1 < n_chunks:
            s_next = scores(ck_ref[(j + 1) * kc:(j + 2) * kc].reshape(kc * nh, HEAD_DIM))
        else:
            s_next = scores(kn_ref[...].reshape(t * nh, HEAD_DIM))
        carry = update(carry, s_cur, cv_ref[j * kc:(j + 1) * kc].reshape(kc * nh, HEAD_DIM))
    _, l, acc = update(carry, s_next, vn_ref[...].reshape(t * nh, HEAD_DIM))

    o = acc / l
    half = rows // 2
    o = o[0:half] - lam_ref[0:1, 0:1] * o[half:rows]
    ms = jnp.mean(o * o, axis=-1, keepdims=True)
    o = o * lax.rsqrt(ms + RMS_EPS) * g_ref[...] * (1.0 - lam_init)
    for h in range(nh):
        o_ref[:, h * HEAD_DIM:(h + 1) * HEAD_DIM] = o[h * t:(h + 1) * t].astype(BF16)


def _dec_attn_call(lam_tile, subln_g, q16, kn32, vn32, cache_k, cache_v, *, layer, lam_init):
    bd, t, aw = q16.shape
    _, p, nh, hd = cache_k.shape
    new_spec = pl.BlockSpec((None, t, nh, hd), lambda b: (b, 0, 0, 0))
    cache_spec = pl.BlockSpec((None, p, nh, hd), lambda b: (layer * bd + b, 0, 0, 0))
    io_spec = pl.BlockSpec((None, t, aw), lambda b: (b, 0, 0))
    kc = _pick_tile(p, 256)
    ncol = max(kc, t) * nh
    row_head = (jnp.arange(2 * nh * t, dtype=jnp.int32)[:, None] // t) % nh
    col_head = jnp.arange(ncol, dtype=jnp.int32)[None, :] % nh
    bias = jnp.where(row_head == col_head, 0.0, NEG_INF).astype(F32)
    return pl.pallas_call(
        functools.partial(_dec_attn_kernel, t=t, nh=nh, kc=kc, lam_init=lam_init),
        grid=(bd,),
        in_specs=[pl.BlockSpec((SUBLANES, LANES), lambda b: (0, 0)),
                  pl.BlockSpec((1, HEAD_DIM), lambda b: (0, 0)),
                  pl.BlockSpec(bias.shape, lambda b: (0, 0)),
                  io_spec, new_spec, new_spec, cache_spec, cache_spec],
        out_specs=io_spec,
        out_shape=jax.ShapeDtypeStruct((bd, t, aw), BF16),
        compiler_params=pltpu.CompilerParams(dimension_semantics=("arbitrary",),
                                             vmem_limit_bytes=VMEM_LIMIT_BYTES),
        name="dec_attn",
    )(lam_tile, subln_g, bias, q16, kn32, vn32, cache_k, cache_v)


def _out_mlp_kernel(x_ref, oa_ref, ol_ref, wo_ref, g1_ref, b1_ref, wu_ref, wd_ref, g2_ref, b2_ref,
                    y_ref, x1_scr, *, alpha):
    j = pl.program_id(1)

    @pl.when(j == 0)
    def _():
        half = x_ref.shape[0] // 2
        for rows in (slice(0, half), slice(half, 2 * half)):
            cat = jnp.concatenate([oa_ref[rows, :], ol_ref[rows, :]], axis=1)
            x1 = _layer_norm(alpha * x_ref[rows, :] + _dot(cat, wo_ref[...]), g1_ref[...], b1_ref[...])
            x1_scr[rows, :] = x1.astype(BF16)
            y_ref[rows, :] = alpha * x1

    hid = jnp.square(jnp.maximum(_dot(x1_scr[...], wu_ref[...]), 0.0)).astype(BF16)
    y_ref[...] += _dot(hid, wd_ref[...])

    @pl.when(j == pl.num_programs(1) - 1)
    def _():
        y_ref[...] = _layer_norm(y_ref[...], g2_ref[...], b2_ref[...])


def _out_mlp_call(x2d, oa16, ol16, w_out16, g1, b1, w_up16, w_down16, g2, b2, *, tm, tf, alpha):
    rows, d = x2d.shape
    aw = oa16.shape[1]
    lw = ol16.shape[1]
    d_ff = w_up16.shape[1]
    row = lambda i, j: (i, 0)
    const = lambda i, j: (0, 0)
    return pl.pallas_call(
        functools.partial(_out_mlp_kernel, alpha=alpha),
        grid=(rows // tm, d_ff // tf),
        in_specs=[pl.BlockSpec((tm, d), row),
                  pl.BlockSpec((tm, aw), row),
                  pl.BlockSpec((tm, lw), row),
                  pl.BlockSpec(w_out16.shape, const, pipeline_mode=pl.Buffered(1)),
                  pl.BlockSpec((1, d), const), pl.BlockSpec((1, d), const),
                  pl.BlockSpec((d, tf), lambda i, j: (0, j)),
                  pl.BlockSpec((tf, d), lambda i, j: (j, 0)),
                  pl.BlockSpec((1, d), const), pl.BlockSpec((1, d), const)],
        out_specs=pl.BlockSpec((tm, d), row),
        out_shape=jax.ShapeDtypeStruct((rows, d), F32),
        scratch_shapes=[pltpu.VMEM((tm, d), BF16)],
        compiler_params=pltpu.CompilerParams(dimension_semantics=("arbitrary", "arbitrary"),
                                             vmem_limit_bytes=VMEM_LIMIT_BYTES),
        name="out_mlp",
    )(x2d, oa16, ol16, w_out16, g1, b1, w_up16, w_down16, g2, b2)


def _rope_tables(pos):
    inv = ROPE_THETA ** (-jnp.arange(0, HALF_DIM, 2, dtype=F32) / HALF_DIM)
    ang = pos.astype(F32)[:, None] * inv[None, :]
    cos = jnp.tile(jnp.cos(ang), (1, 4))
    sin = jnp.sin(ang)
    sin = jnp.tile(jnp.concatenate([-sin, sin], axis=-1), (1, 2))
    return cos, sin


def _gate_weights(w_a, w_i):
    per = MXU_DIM // w_a.shape[-1]

    def dense(w):
        nb, bd, _ = w.shape
        w4 = w.reshape(nb // per, per, bd, bd)
        eye = jnp.eye(per, dtype=w.dtype)
        return jnp.einsum('gpij,pq->gpiqj', w4, eye).reshape(nb // per, per * bd, per * bd)

    return jnp.concatenate([dense(w_a), dense(w_i)], axis=-1).astype(BF16)


def _pick_tile(n, pref):
    t = min(n, pref)
    while n % t:
        t //= 2
    return t


def kernel(x_prompt, x_sample, cache_k, cache_v, state_h, state_conv, w_in, lambda_q1, lambda_k1, lambda_q2, lambda_k2, subln_g, conv_w, conv_b, w_rg_a, b_rg_a, w_rg_i, b_rg_i, lru_lambda, w_out, ln1_g, ln1_b, w_up, w_down, ln2_g, ln2_b):
    depth = w_in.shape[0]
    b, s, d = x_prompt.shape
    bd, t, _ = x_sample.shape
    p = cache_k.shape[2]
    lw = conv_w.shape[-1]
    aw = (w_in.shape[-1] - 2 * lw) // 3
    nh = aw // HEAD_DIM
    alpha = (2.0 * depth) ** 0.25

    cos_p, sin_p = _rope_tables(jnp.arange(s, dtype=jnp.int32))
    cos_s, sin_s = _rope_tables(p + jnp.arange(t, dtype=jnp.int32))
    cos_s = jnp.tile(cos_s, (bd, 1))
    sin_s = jnp.tile(sin_s, (bd, 1))

    tm_a = _pick_tile(s, 256)
    tq = _pick_tile(s, 512)
    tm_c = _pick_tile(b * s, 512)
    tf = _pick_tile(w_up.shape[-1], 1024)

    yp = x_prompt.reshape(b * s, d)
    ys = x_sample.reshape(bd * t, d)
    outs = [[] for _ in range(8)]
    for l in range(depth):
        lam0 = _lambda_init(l)
        w_in16, proj_s = _cast_proj_call(ys, w_in[l])
        wg = _gate_weights(w_rg_a[l], w_rg_i[l])
        row = lambda v: v[l].reshape(1, -1)
        lam_tile = _lam_call(row(lambda_q1), row(lambda_k1), row(lambda_q2), row(lambda_k2), lam0)
        g_sub = row(subln_g)
        lru_args = (conv_w[l], row(conv_b), wg, row(b_rg_a), row(b_rg_i), row(lru_lambda))

        qt16, k32, k16, v32, vt16, ol16, hp, cp = _in_lru_call(
            yp, w_in16, cos_p, sin_p, *lru_args,
            jnp.zeros((b, 1, lw), F32), jnp.zeros((b, CONV_WIDTH - 1, lw), F32),
            tm=tm_a, seq_rows=tm_a, rows_per_batch=s, vt_tile=tq)
        oa16, w_out16, w_down16, w_up16 = _attn_call(
            lam_tile, g_sub, qt16, k16.reshape(b, s, aw), vt16, (w_out[l], w_down[l]), (w_up[l],),
            t=tq, lam_init=lam0)
        ln_args = (w_out16, row(ln1_g), row(ln1_b), w_up16, w_down16, row(ln2_g), row(ln2_b))
        yp = _out_mlp_call(yp, oa16.reshape(b * s, aw), ol16, *ln_args, tm=tm_c, tf=tf, alpha=alpha)
        outs[0].append(k32.reshape(b, s, nh, HEAD_DIM))
        outs[1].append(v32.reshape(b, s, nh, HEAD_DIM))
        outs[2].append(hp.reshape(b, lw))
        outs[3].append(cp)

        q16, k32, v32, ol16, hn, cn = _in_lru_call(
            ys, proj_s, cos_s, sin_s, *lru_args,
            state_h[l].reshape(bd, 1, lw), state_conv[l],
            tm=bd * t, seq_rows=t, rows_per_batch=t, vt_tile=bd * t)
        oa16 = _dec_attn_call(lam_tile, g_sub, q16.reshape(bd, t, aw),
                              k32.reshape(bd, t, nh, HEAD_DIM), v32.reshape(bd, t, nh, HEAD_DIM),
                              cache_k.reshape(depth * bd, p, nh, HEAD_DIM),
                              cache_v.reshape(depth * bd, p, nh, HEAD_DIM), layer=l, lam_init=lam0)
        ys = _out_mlp_call(ys, oa16.reshape(bd * t, aw), ol16, *ln_args,
                           tm=_pick_tile(bd * t, 512), tf=tf, alpha=alpha)
        outs[4].append(k32.reshape(bd, t, nh, HEAD_DIM))
        outs[5].append(v32.reshape(bd, t, nh, HEAD_DIM))
        outs[6].append(hn.reshape(bd, lw))
        outs[7].append(cn)

    stk = [jnp.stack(o) for o in outs]
    return (yp.reshape(b, s, d), ys.reshape(bd, t, d), stk[0], stk[1], stk[2], stk[3],
            stk[4], stk[5], stk[6], stk[7])
```

```python
import functools
import math

import jax
import jax.numpy as jnp
from jax import lax
from jax.experimental import pallas as pl
from jax.experimental.pallas import tpu as pltpu

CHUNK = 64
HEAD_DIM = 128
HALF_DIM = HEAD_DIM // 2
LRU_BLOCKS = 16
CONV_WIDTH = 4
LRU_C = 8.0
ROPE_THETA = 10000.0
LN_EPS = 1e-5
RMS_EPS = 1e-5
NEG_INF = -1e30

SUBLANES = 8
LANES = 128
MXU_DIM = 256
ONES_ROWS = 16
VMEM_LIMIT_BYTES = 60 * 1024 * 1024

IN_ROWS = 512
ATTN_TILE = 512
DEC_KEY_CHUNK = 256
MLP_ROWS = 512
MLP_HIDDEN = 1024

Q_SCALE = HALF_DIM ** -0.5 * math.log2(math.e)

BF16 = jnp.bfloat16
F32 = jnp.float32


def _lambda_init(layer):
    return 0.8 - 0.6 * math.exp(-0.3 * layer)


def _dot(a, b):
    return jnp.dot(a, b, preferred_element_type=F32)


def _dot_nt(a, b):
    return lax.dot_general(a, b, (((1,), (1,)), ((), ())), preferred_element_type=F32)


def _layer_norm(z, g, b):
    mu = jnp.mean(z, axis=-1, keepdims=True)
    d = z - mu
    var = jnp.mean(d * d, axis=-1, keepdims=True)
    return d * lax.rsqrt(var + LN_EPS) * g + b


def _lam_kernel(q1_ref, k1_ref, q2_ref, k2_ref, o_ref, *, lam_init):
    s1 = jnp.sum(q1_ref[...] * k1_ref[...], axis=-1, keepdims=True)
    s2 = jnp.sum(q2_ref[...] * k2_ref[...], axis=-1, keepdims=True)
    lam = jnp.exp(s1) - jnp.exp(s2) + lam_init
    o_ref[...] = jnp.broadcast_to(lam, o_ref.shape)


def _lam_call(lq1, lk1, lq2, lk2, lam_init):
    return pl.pallas_call(
        functools.partial(_lam_kernel, lam_init=lam_init),
        out_shape=jax.ShapeDtypeStruct((SUBLANES, LANES), F32),
        name="lam",
    )(lq1, lk1, lq2, lk2)


def _group_scan(a, u, sub):
    for s in (1, 2, 4):
        keep = sub >= s
        a_sh = pltpu.roll(a, s, axis=0)
        u_sh = pltpu.roll(u, s, axis=0)
        u = u + jnp.where(keep, a * u_sh, 0.0)
        a = a * jnp.where(keep, a_sh, 1.0)
    return a, u


def _in_lru_kernel(x_ref, w_ref, cos_ref, sin_ref, cw_ref, cb_ref, wg_ref, ba_ref, bi_ref,
                   lam_ref, h0_ref, c0_ref, *outs_and_scratch,
                   tm, seq_rows, tiles_per_seq, aw, lw):
    carried = seq_rows == tm
    nseq = tm // seq_rows
    gps = seq_rows // SUBLANES
    i = pl.program_id(0)
    xp_scr, xc32_scr, xc16_scr, gate_scr, g_scr, h_scr = outs_and_scratch[-6:]
    if carried:
        qt_ref, k32_ref, k16_ref, v32_ref, vt_ref, ol_ref, hl_ref, cl_ref = outs_and_scratch[:-6]
    else:
        q_ref, k32_ref, v32_ref, ol_ref, hl_ref, cl_ref = outs_and_scratch[:-6]

    if carried:
        xb16 = x_ref[...].astype(BF16)
        proj = lambda c0, width: _dot(xb16, w_ref[:, c0:c0 + width])
    else:
        proj = lambda c0, width: w_ref[:, c0:c0 + width]

    cos = cos_ref[...]
    sin = sin_ref[...]
    lane = lax.broadcasted_iota(jnp.int32, (tm, LANES), 1)
    first_half = (lane % HALF_DIM) < (HALF_DIM // 2)

    def rope(xs):
        rot = jnp.where(first_half, pltpu.roll(xs, LANES - HALF_DIM // 2, axis=1),
                        pltpu.roll(xs, HALF_DIM // 2, axis=1))
        return xs * cos + rot * sin

    pieces_per_seg = aw // MXU_DIM
    n_pieces = 3 * pieces_per_seg

    def qkv_piece(p):
        seg, c0 = p // pieces_per_seg, (p % pieces_per_seg) * MXU_DIM
        pre = proj(seg * aw + c0, MXU_DIM)
        for h in range(MXU_DIM // LANES):
            sl = slice(c0 + h * LANES, c0 + (h + 1) * LANES)
            ph = pre[:, h * LANES:(h + 1) * LANES]
            head = c0 // LANES + h
            if seg == 0:
                qs = rope(ph) * Q_SCALE
                if carried:
                    qt_ref[head] = qs.T.astype(BF16)
                else:
                    q_ref[:, sl] = qs.astype(BF16)
            elif seg == 1:
                kr = rope(ph)
                k32_ref[:, sl] = kr
                if carried:
                    k16_ref[:, sl] = kr.astype(BF16)
            else:
                v32_ref[:, sl] = ph
                if carried:
                    vt_ref[head, 0:HEAD_DIM, :] = ph.T.astype(BF16)
                    vt_ref[head, HEAD_DIM:HEAD_DIM + ONES_ROWS, :] = jnp.ones((ONES_ROWS, tm), BF16)

    xbp = proj(3 * aw, lw)
    g_scr[...] = proj(3 * aw + lw, lw)

    if carried:
        @pl.when(i % tiles_per_seq == 0)
        def _():
            xp_scr[0, 0:SUBLANES, :] = jnp.zeros((SUBLANES, lw), F32)
            h_scr[...] = jnp.zeros_like(h_scr)
    else:
        for sq in range(nseq):
            xp_scr[sq, SUBLANES - (CONV_WIDTH - 1):SUBLANES, :] = c0_ref[sq]
    for sq in range(nseq):
        xp_scr[sq, SUBLANES:, :] = xbp[sq * seq_rows:(sq + 1) * seq_rows, :]

    sub = lax.broadcasted_iota(jnp.int32, (SUBLANES, lw), 0)
    cwb = [jnp.broadcast_to(cw_ref[j:j + 1, :], (SUBLANES, lw)) for j in range(CONV_WIDTH)]
    cbb = jnp.broadcast_to(cb_ref[...], (SUBLANES, lw))
    for sq in range(nseq):
        prev = xp_scr[sq, 0:SUBLANES, :]
        prev_rolled = [pltpu.roll(prev, d, axis=0) for d in range(1, CONV_WIDTH)]
        for gg in range(gps):
            r0 = SUBLANES + gg * SUBLANES
            cur = xp_scr[sq, r0:r0 + SUBLANES, :]
            cur_rolled = [pltpu.roll(cur, d, axis=0) for d in range(1, CONV_WIDTH)]
            xc = cbb + cwb[CONV_WIDTH - 1] * cur
            for d in range(1, CONV_WIDTH):
                shifted = jnp.where(sub >= d, cur_rolled[d - 1], prev_rolled[d - 1])
                xc = xc + cwb[CONV_WIDTH - 1 - d] * shifted
            prev_rolled = cur_rolled
            g0 = (sq * gps + gg) * SUBLANES
            xc32_scr[g0:g0 + SUBLANES, :] = xc
            xc16_scr[g0:g0 + SUBLANES, :] = xc.astype(BF16)

    for sq in range(nseq):
        cl_ref[sq] = xp_scr[sq, SUBLANES + seq_rows - (CONV_WIDTH - 1):SUBLANES + seq_rows, :]
    if carried:
        xp_scr[0, 0:SUBLANES, :] = xp_scr[0, seq_rows:seq_rows + SUBLANES, :]

    for blk in range(lw // MXU_DIM):
        sl = slice(blk * MXU_DIM, (blk + 1) * MXU_DIM)
        gate_scr[:, 2 * blk * MXU_DIM:2 * (blk + 1) * MXU_DIM] = _dot(xc16_scr[:, sl], wg_ref[blk])

    nl = -lam_ref[...]
    softplus = jnp.maximum(nl, 0.0) + jnp.log1p(jnp.exp(-jnp.abs(nl)))
    decay = jnp.broadcast_to(-LRU_C * softplus, (SUBLANES, lw))
    ba = jnp.broadcast_to(ba_ref[...], (SUBLANES, lw))
    bi = jnp.broadcast_to(bi_ref[...], (SUBLANES, lw))

    def sigmoid(z):
        return 0.5 * jnp.tanh(0.5 * z) + 0.5

    n_groups = nseq * gps
    pieces_done = 0
    after = None
    for sq in range(nseq):
        hin = h_scr[...] if carried else h0_ref[sq]
        for gg in range(gps):
            while pieces_done * n_groups < (sq * gps + gg) * n_pieces:
                qkv_piece(pieces_done)
                pieces_done += 1
            g0 = (sq * gps + gg) * SUBLANES
            rows = slice(g0, g0 + SUBLANES)
            pre = gate_scr[rows, :]
            ra = jnp.concatenate(
                [pre[:, 2 * b * MXU_DIM:(2 * b + 1) * MXU_DIM] for b in range(lw // MXU_DIM)], axis=1)
            ri = jnp.concatenate(
                [pre[:, (2 * b + 1) * MXU_DIM:(2 * b + 2) * MXU_DIM] for b in range(lw // MXU_DIM)], axis=1)
            if after is not None:
                ra = ra + after
                ri = ri + after
            r = sigmoid(ra + ba)
            ig = sigmoid(ri + bi)
            log_a = decay * r
            a = jnp.exp(log_a)
            y = (1.0 + a * a) * jnp.tanh(-log_a)
            root = jnp.where(y > 0.0, y * lax.rsqrt(y), 0.0)
            u = root * (ig * xc32_scr[rows, :])
            acum, ucum = _group_scan(a, u, sub)
            hs = acum * hin + ucum
            hin = hs[SUBLANES - 1:SUBLANES, :]
            out = hs * jax.nn.gelu(g_scr[rows, :])
            ol_ref[rows, :] = out.astype(BF16)
            after = pltpu.bitcast(
                lax.shift_right_logical(pltpu.bitcast(out, jnp.uint32), jnp.uint32(32)), F32)
        hl_ref[sq] = hin
        if carried:
            h_scr[...] = hin
    while pieces_done < n_pieces:
        qkv_piece(pieces_done)
        pieces_done += 1


def _cast_proj_kernel(x_ref, w_ref, w16_ref, p_ref):
    w16 = w_ref[...].astype(BF16)
    w16_ref[...] = w16
    p_ref[...] = _dot(x_ref[...].astype(BF16), w16)


def _cast_proj_call(x2d, w_in):
    rows, d_model = x2d.shape
    n = w_in.shape[1]
    col = pl.BlockSpec((d_model, MXU_DIM), lambda p: (0, p))
    return pl.pallas_call(
        _cast_proj_kernel,
        grid=(n // MXU_DIM,),
        in_specs=[pl.BlockSpec((rows, d_model), lambda p: (0, 0)), col],
        out_specs=[col, pl.BlockSpec((rows, MXU_DIM), lambda p: (0, p))],
        out_shape=[jax.ShapeDtypeStruct(w_in.shape, BF16), jax.ShapeDtypeStruct((rows, n), F32)],
        compiler_params=pltpu.CompilerParams(dimension_semantics=("arbitrary",),
                                             vmem_limit_bytes=VMEM_LIMIT_BYTES),
        name="cast_proj",
    )(x2d, w_in)


def _in_lru_call(x2d, w_in16, cos, sin, conv_w, conv_b, wg, b_a, b_i, lru_lambda, h0, c0,
                 *, tm, seq_rows, rows_per_batch, vt_tile):
    rows, d_model = x2d.shape
    aw = (w_in16.shape[1] - 2 * conv_w.shape[1]) // 3
    lw = conv_w.shape[1]
    nh = aw // HEAD_DIM
    carried = seq_rows == tm
    n_tiles = rows // tm
    tiles_per_seq = rows_per_batch // tm if carried else 1
    nseq_tile = tm // seq_rows
    nseq_total = rows // rows_per_batch
    if carried:
        state_map = lambda i: (i // tiles_per_seq, 0, 0)
    else:
        state_map = lambda i: (i, 0, 0)
    const2 = lambda i: (0, 0)
    const3 = lambda i: (0, 0, 0)
    row_map = lambda i: (i, 0)
    resident = dict(pipeline_mode=pl.Buffered(1))
    in_specs = [
        pl.BlockSpec((tm, d_model), row_map),
        pl.BlockSpec(w_in16.shape, const2, **resident) if carried
        else pl.BlockSpec((tm, w_in16.shape[1]), row_map),
        pl.BlockSpec((tm, LANES), (lambda i: (i % tiles_per_seq, 0)) if carried else const2),
        pl.BlockSpec((tm, LANES), (lambda i: (i % tiles_per_seq, 0)) if carried else const2),
        pl.BlockSpec(conv_w.shape, const2),
        pl.BlockSpec(conv_b.shape, const2),
        pl.BlockSpec(wg.shape, const3, **resident),
        pl.BlockSpec(b_a.shape, const2),
        pl.BlockSpec(b_i.shape, const2),
        pl.BlockSpec(lru_lambda.shape, const2),
        pl.BlockSpec((nseq_tile, 1, lw), state_map),
        pl.BlockSpec((nseq_tile, CONV_WIDTH - 1, lw), state_map),
    ]
    rows_aw = lambda dt: (jax.ShapeDtypeStruct((rows, aw), dt), pl.BlockSpec((tm, aw), row_map))
    k32_out, v32_out = rows_aw(F32), rows_aw(F32)
    tail_outs = [
        (jax.ShapeDtypeStruct((rows, lw), BF16), pl.BlockSpec((tm, lw), row_map)),
        (jax.ShapeDtypeStruct((nseq_total, 1, lw), F32), pl.BlockSpec((nseq_tile, 1, lw), state_map)),
        (jax.ShapeDtypeStruct((nseq_total, CONV_WIDTH - 1, lw), F32),
         pl.BlockSpec((nseq_tile, CONV_WIDTH - 1, lw), state_map)),
    ]
    if carried:
        t_lead = (nseq_total, nh, rows_per_batch // vt_tile)
        t_map = lambda i: (i // tiles_per_seq, 0, ((i % tiles_per_seq) * tm) // vt_tile, 0,
                           (((i % tiles_per_seq) * tm) % vt_tile) // tm)
        transposed = lambda r: (jax.ShapeDtypeStruct(t_lead + (r, vt_tile), BF16),
                                pl.BlockSpec((None, nh, None, r, tm), t_map))
        outs = [transposed(HEAD_DIM), k32_out, rows_aw(BF16), v32_out,
                transposed(HEAD_DIM + ONES_ROWS)] + tail_outs
    else:
        outs = [rows_aw(BF16), k32_out, v32_out] + tail_outs
    out_shape = tuple(o[0] for o in outs)
    out_specs = tuple(o[1] for o in outs)
    scratch = [
        pltpu.VMEM((nseq_tile, SUBLANES + seq_rows, lw), F32),
        pltpu.VMEM((tm, lw), F32),
        pltpu.VMEM((tm, lw), BF16),
        pltpu.VMEM((tm, 2 * lw), F32),
        pltpu.VMEM((tm, lw), F32),
        pltpu.VMEM((1, lw), F32),
    ]
    kern = functools.partial(_in_lru_kernel, tm=tm, seq_rows=seq_rows, tiles_per_seq=tiles_per_seq,
                             aw=aw, lw=lw)
    return pl.pallas_call(
        kern, grid=(n_tiles,), in_specs=in_specs, out_specs=out_specs, out_shape=out_shape,
        scratch_shapes=scratch,
        compiler_params=pltpu.CompilerParams(dimension_semantics=("arbitrary",),
                                             vmem_limit_bytes=VMEM_LIMIT_BYTES),
        name="in_lru",
    )(x2d, w_in16, cos, sin, conv_w, conv_b, wg, b_a, b_i, lru_lambda, h0, c0)


def _attn_kernel(lam_ref, g_ref, bias_ref, qt_ref, k_ref, vt_ref, *rest, t, n_q, lam_init):
    n_w = (len(rest) - 4) // 2
    w32_refs, o_ref, w16_refs = rest[:n_w], rest[n_w], rest[n_w + 1:2 * n_w + 1]
    s_scr, m_scr, acc_scr = rest[2 * n_w + 1:]

    def scores(i, j, slot):
        qt = qt_ref[i]
        dim = lax.broadcasted_iota(jnp.int32, (HEAD_DIM, t), 0)
        zero = jnp.zeros_like(qt)
        qq = jnp.concatenate([jnp.where(dim < HALF_DIM, qt, zero),
                              jnp.where(dim >= HALF_DIM, qt, zero)], axis=1)
        kj = k_ref[pl.ds(pl.multiple_of(j * t, t), t), :]
        s_scr[slot] = _dot(kj, qq)

    def softmax_pv(j, slot, masked):
        s = s_scr[slot]
        if masked:
            s = s + bias_ref[...]
        m = m_scr[...]
        m_new = jnp.maximum(m, jnp.max(s, axis=0, keepdims=True))
        alpha = jnp.exp2(m - m_new)
        p = jnp.exp2(s - m_new)
        m_scr[...] = m_new
        acc_scr[...] = alpha * acc_scr[...] + _dot(vt_ref[j], p.astype(BF16))

    def reset():
        m_scr[...] = jnp.full(m_scr.shape, NEG_INF, F32)
        acc_scr[...] = jnp.zeros(acc_scr.shape, F32)

    def finish(i):
        acc = acc_scr[...]
        o = acc[0:HEAD_DIM] / acc[HEAD_DIM:HEAD_DIM + 1]
        o = o[:, 0:t] - lam_ref[0:1, 0:1] * o[:, t:2 * t]
        ms = jnp.mean(o * o, axis=0, keepdims=True)
        o = o * lax.rsqrt(ms + RMS_EPS) * g_ref[...] * (1.0 - lam_init)
        o_ref[pl.ds(pl.multiple_of(i * t, t), t), :] = o.T.astype(BF16)
        reset()

    def following(i, j):
        edge = j == i
        return jnp.where(edge, i + 1, i), jnp.where(edge, 0, j + 1)

    def pair(_, ij):
        ia, ja = ij
        ib, jb = following(ia, ja)
        ic, jc = following(ib, jb)
        icc = jnp.minimum(ic, n_q - 1)
        edge_a = ja == ia
        edge_b = jb == ib

        def body(mask_a, mask_b):
            scores(ib, jb, 1)
            softmax_pv(ja, 0, mask_a)
            if mask_a:
                finish(ia)
            scores(icc, jc, 0)
            softmax_pv(jb, 1, mask_b)
            if mask_b:
                finish(ib)

        pl.when(edge_a)(lambda: body(True, False))
        pl.when(edge_b)(lambda: body(False, True))
        pl.when(jnp.logical_not(jnp.logical_or(edge_a, edge_b)))(lambda: body(False, False))
        return ic, jc

    reset()
    scores(0, 0, 0)
    for w32_ref, w16_ref in zip(w32_refs, w16_refs):
        w16_ref[...] = w32_ref[...].astype(BF16)
    n_items = n_q * (n_q + 1) // 2
    lax.fori_loop(0, n_items // 2, pair, (jnp.int32(0), jnp.int32(0)))
    if n_items % 2:
        softmax_pv(n_q - 1, 0, True)
        finish(n_q - 1)


def _attn_call(lam_tile, subln_g, qt, k16, vt, cast_rows, cast_cols, *, t, lam_init):
    b, s, aw = k16.shape
    nh = aw // HEAD_DIM
    steps = b * nh
    key_chunk = jnp.arange(t, dtype=jnp.int32)[:, None] // CHUNK
    qry_chunk = (jnp.arange(2 * t, dtype=jnp.int32)[None, :] % t) // CHUNK
    bias = jnp.where(key_chunk <= qry_chunk, 0.0, NEG_INF).astype(F32)
    seq_spec = pl.BlockSpec((None, s, HEAD_DIM), lambda bi, hi: (bi, 0, hi))
    const = lambda bi, hi: (0, 0)
    w_specs = [pl.BlockSpec((w.shape[0] // steps, w.shape[1]), lambda bi, hi: (bi * nh + hi, 0))
               for w in cast_rows]
    w_specs += [pl.BlockSpec((w.shape[0], w.shape[1] // steps), lambda bi, hi: (0, bi * nh + hi))
                for w in cast_cols]
    weights = list(cast_rows) + list(cast_cols)
    return pl.pallas_call(
        functools.partial(_attn_kernel, t=t, n_q=s // t, lam_init=lam_init),
        grid=(b, nh),
        in_specs=[pl.BlockSpec((SUBLANES, LANES), const),
                  pl.BlockSpec((HEAD_DIM, 1), const),
                  pl.BlockSpec((t, 2 * t), const),
                  pl.BlockSpec((None, None, s // t, HEAD_DIM, t), lambda bi, hi: (bi, hi, 0, 0, 0)),
                  seq_spec,
                  pl.BlockSpec((None, None, s // t, HEAD_DIM + ONES_ROWS, t),
                               lambda bi, hi: (bi, hi, 0, 0, 0))] + w_specs,
        out_specs=[seq_spec] + w_specs,
        out_shape=[jax.ShapeDtypeStruct((b, s, aw), BF16)]
        + [jax.ShapeDtypeStruct(w.shape, BF16) for w in weights],
        scratch_shapes=[pltpu.VMEM((2, t, 2 * t), F32),
                        pltpu.VMEM((1, 2 * t), F32),
                        pltpu.VMEM((HEAD_DIM + ONES_ROWS, 2 * t), F32)],
        compiler_params=pltpu.CompilerParams(
            dimension_semantics=("arbitrary", "arbitrary"),
            vmem_limit_bytes=VMEM_LIMIT_BYTES),
        name="attn",
    )(lam_tile, subln_g.reshape(HEAD_DIM, 1), bias, qt, k16, vt, *weights)


def _dec_attn_kernel(lam_ref, g_ref, bias_ref, q_ref, kn_ref, vn_ref, ck_ref, cv_ref, o_ref,
                     *, t, nh, kc, lam_init):
    rows = 2 * nh * t
    p_len = ck_ref.shape[0]
    q = q_ref[...]
    lane = lax.broadcasted_iota(jnp.int32, (t, HEAD_DIM), 1)
    parts = []
    for c in range(2):
        keep = (lane < HALF_DIM) if c == 0 else (lane >= HALF_DIM)
        for h in range(nh):
            qh = q[:, h * HEAD_DIM:(h + 1) * HEAD_DIM]
            parts.append(jnp.where(keep, qh, jnp.zeros_like(qh)))
    qall = jnp.concatenate(parts, axis=0)

    def scores(kf):
        s = _dot_nt(qall, kf.astype(BF16))
        return s + bias_ref[:, 0:s.shape[1]]

    def update(carry, s, vf):
        m, l, acc = carry
        m_new = jnp.maximum(m, jnp.max(s, axis=-1, keepdims=True))
        alpha = jnp.exp2(m - m_new)
        p = jnp.exp2(s - m_new)
        l = alpha * l + jnp.sum(p, axis=-1, keepdims=True)
        acc = alpha * acc + _dot(p.astype(BF16), vf.astype(BF16))
        return m_new, l, acc

    carry = (jnp.full((rows, 1), NEG_INF, F32), jnp.zeros((rows, 1), F32),
             jnp.zeros((rows, HEAD_DIM), F32))
    n_chunks = p_len // kc
    s_next = scores(ck_ref[0:kc].reshape(kc * nh, HEAD_DIM))
    for j in range(n_chunks):
        s_cur = s_next
        if j + 1 < n_chunks:
            s_next = scores(ck_ref[(j + 1) * kc:(j + 2) * kc].reshape(kc * nh, HEAD_DIM))
        else:
            s_next = scores(kn_ref[...].reshape(t * nh, HEAD_DIM))
        carry = update(carry, s_cur, cv_ref[j * kc:(j + 1) * kc].reshape(kc * nh, HEAD_DIM))
    _, l, acc = update(carry, s_next, vn_ref[...].reshape(t * nh, HEAD_DIM))

    o = acc / l
    half = rows // 2
    o = o[0:half] - lam_ref[0:1, 0:1] * o[half:rows]
    ms = jnp.mean(o * o, axis=-1, keepdims=True)
    o = o * lax.rsqrt(ms + RMS_EPS) * g_ref[...] * (1.0 - lam_init)
    for h in range(nh):
        o_ref[:, h * HEAD_DIM:(h + 1) * HEAD_DIM] = o[h * t:(h + 1) * t].astype(BF16)


def _dec_attn_call(lam_tile, subln_g, q16, kn32, vn32, cache_k, cache_v, *, layer, lam_init):
    bd, t, aw = q16.shape
    _, p, nh, hd = cache_k.shape
    new_spec = pl.BlockSpec((None, t, nh, hd), lambda b: (b, 0, 0, 0))
    cache_spec = pl.BlockSpec((None, p, nh, hd), lambda b: (layer * bd + b, 0, 0, 0))
    io_spec = pl.BlockSpec((None, t, aw), lambda b: (b, 0, 0))
    kc = _pick_tile(p, DEC_KEY_CHUNK)
    ncol = max(kc, t) * nh
    row_head = (jnp.arange(2 * nh * t, dtype=jnp.int32)[:, None] // t) % nh
    col_head = jnp.arange(ncol, dtype=jnp.int32)[None, :] % nh
    bias = jnp.where(row_head == col_head, 0.0, NEG_INF).astype(F32)
    return pl.pallas_call(
        functools.partial(_dec_attn_kernel, t=t, nh=nh, kc=kc, lam_init=lam_init),
        grid=(bd,),
        in_specs=[pl.BlockSpec((SUBLANES, LANES), lambda b: (0, 0)),
                  pl.BlockSpec((1, HEAD_DIM), lambda b: (0, 0)),
                  pl.BlockSpec(bias.shape, lambda b: (0, 0)),
                  io_spec, new_spec, new_spec, cache_spec, cache_spec],
        out_specs=io_spec,
        out_shape=jax.ShapeDtypeStruct((bd, t, aw), BF16),
        compiler_params=pltpu.CompilerParams(dimension_semantics=("arbitrary",),
                                             vmem_limit_bytes=VMEM_LIMIT_BYTES),
        name="dec_attn",
    )(lam_tile, subln_g, bias, q16, kn32, vn32, cache_k, cache_v)


def _out_mlp_kernel(x_ref, oa_ref, ol_ref, wo_ref, g1_ref, b1_ref, wu_ref, wd_ref, g2_ref, b2_ref,
                    y_ref, x1_scr, *, alpha):
    j = pl.program_id(1)

    @pl.when(j == 0)
    def _():
        cat = jnp.concatenate([oa_ref[...], ol_ref[...]], axis=1)
        x1 = _layer_norm(alpha * x_ref[...] + _dot(cat, wo_ref[...]), g1_ref[...], b1_ref[...])
        x1_scr[...] = x1.astype(BF16)
        y_ref[...] = alpha * x1

    hid = jnp.square(jnp.maximum(_dot(x1_scr[...], wu_ref[...]), 0.0)).astype(BF16)
    y_ref[...] += _dot(hid, wd_ref[...])

    @pl.when(j == pl.num_programs(1) - 1)
    def _():
        y_ref[...] = _layer_norm(y_ref[...], g2_ref[...], b2_ref[...])


def _out_mlp_call(x2d, oa16, ol16, w_out16, g1, b1, w_up16, w_down16, g2, b2, *, tm, tf, alpha):
    rows, d = x2d.shape
    aw = oa16.shape[1]
    lw = ol16.shape[1]
    d_ff = w_up16.shape[1]
    row = lambda i, j: (i, 0)
    const = lambda i, j: (0, 0)
    return pl.pallas_call(
        functools.partial(_out_mlp_kernel, alpha=alpha),
        grid=(rows // tm, d_ff // tf),
        in_specs=[pl.BlockSpec((tm, d), row),
                  pl.BlockSpec((tm, aw), row),
                  pl.BlockSpec((tm, lw), row),
                  pl.BlockSpec(w_out16.shape, const, pipeline_mode=pl.Buffered(1)),
                  pl.BlockSpec((1, d), const), pl.BlockSpec((1, d), const),
                  pl.BlockSpec((d, tf), lambda i, j: (0, j)),
                  pl.BlockSpec((tf, d), lambda i, j: (j, 0)),
                  pl.BlockSpec((1, d), const), pl.BlockSpec((1, d), const)],
        out_specs=pl.BlockSpec((tm, d), row),
        out_shape=jax.ShapeDtypeStruct((rows, d), F32),
        scratch_shapes=[pltpu.VMEM((tm, d), BF16)],
        compiler_params=pltpu.CompilerParams(dimension_semantics=("arbitrary", "arbitrary"),
                                             vmem_limit_bytes=VMEM_LIMIT_BYTES),
        name="out_mlp",
    )(x2d, oa16, ol16, w_out16, g1, b1, w_up16, w_down16, g2, b2)


def _rope_tables(pos):
    inv = ROPE_THETA ** (-jnp.arange(0, HALF_DIM, 2, dtype=F32) / HALF_DIM)
    ang = pos.astype(F32)[:, None] * inv[None, :]
    cos = jnp.tile(jnp.cos(ang), (1, 4))
    sin = jnp.sin(ang)
    sin = jnp.tile(jnp.concatenate([-sin, sin], axis=-1), (1, 2))
    return cos, sin


def _gate_weights(w_a, w_i):
    per = MXU_DIM // w_a.shape[-1]

    def dense(w):
        nb, bd, _ = w.shape
        w4 = w.reshape(nb // per, per, bd, bd)
        eye = jnp.eye(per, dtype=w.dtype)
        return jnp.einsum('gpij,pq->gpiqj', w4, eye).reshape(nb // per, per * bd, per * bd)

    return jnp.concatenate([dense(w_a), dense(w_i)], axis=-1).astype(BF16)


def _pick_tile(n, pref):
    t = min(n, pref)
    while n % t:
        t //= 2
    return t


def kernel(x_prompt, x_sample, cache_k, cache_v, state_h, state_conv, w_in, lambda_q1, lambda_k1, lambda_q2, lambda_k2, subln_g, conv_w, conv_b, w_rg_a, b_rg_a, w_rg_i, b_rg_i, lru_lambda, w_out, ln1_g, ln1_b, w_up, w_down, ln2_g, ln2_b):
    depth = w_in.shape[0]
    b, s, d = x_prompt.shape
    bd, t, _ = x_sample.shape
    p = cache_k.shape[2]
    lw = conv_w.shape[-1]
    aw = (w_in.shape[-1] - 2 * lw) // 3
    nh = aw // HEAD_DIM
    alpha = (2.0 * depth) ** 0.25

    cos_p, sin_p = _rope_tables(jnp.arange(s, dtype=jnp.int32))
    cos_s, sin_s = _rope_tables(p + jnp.arange(t, dtype=jnp.int32))
    cos_s = jnp.tile(cos_s, (bd, 1))
    sin_s = jnp.tile(sin_s, (bd, 1))

    tm_a = _pick_tile(s, IN_ROWS)
    tq = _pick_tile(s, ATTN_TILE)
    tm_c = _pick_tile(b * s, MLP_ROWS)
    tf = _pick_tile(w_up.shape[-1], MLP_HIDDEN)

    yp = x_prompt.reshape(b * s, d)
    ys = x_sample.reshape(bd * t, d)
    outs = [[] for _ in range(8)]
    for l in range(depth):
        lam0 = _lambda_init(l)
        w_in16, proj_s = _cast_proj_call(ys, w_in[l])
        wg = _gate_weights(w_rg_a[l], w_rg_i[l])
        row = lambda v: v[l].reshape(1, -1)
        lam_tile = _lam_call(row(lambda_q1), row(lambda_k1), row(lambda_q2), row(lambda_k2), lam0)
        g_sub = row(subln_g)
        lru_args = (conv_w[l], row(conv_b), wg, row(b_rg_a), row(b_rg_i), row(lru_lambda))

        qt16, k32, k16, v32, vt16, ol16, hp, cp = _in_lru_call(
            yp, w_in16, cos_p, sin_p, *lru_args,
            jnp.zeros((b, 1, lw), F32), jnp.zeros((b, CONV_WIDTH - 1, lw), F32),
            tm=tm_a, seq_rows=tm_a, rows_per_batch=s, vt_tile=tq)
        oa16, w_out16, w_down16, w_up16 = _attn_call(
            lam_tile, g_sub, qt16, k16.reshape(b, s, aw), vt16, (w_out[l], w_down[l]), (w_up[l],),
            t=tq, lam_init=lam0)
        ln_args = (w_out16, row(ln1_g), row(ln1_b), w_up16, w_down16, row(ln2_g), row(ln2_b))
        yp = _out_mlp_call(yp, oa16.reshape(b * s, aw), ol16, *ln_args, tm=tm_c, tf=tf, alpha=alpha)
        outs[0].append(k32.reshape(b, s, nh, HEAD_DIM))
        outs[1].append(v32.reshape(b, s, nh, HEAD_DIM))
        outs[2].append(hp.reshape(b, lw))
        outs[3].append(cp)

        q16, k32, v32, ol16, hn, cn = _in_lru_call(
            ys, proj_s, cos_s, sin_s, *lru_args,
            state_h[l].reshape(bd, 1, lw), state_conv[l],
            tm=bd * t, seq_rows=t, rows_per_batch=t, vt_tile=bd * t)
        oa16 = _dec_attn_call(lam_tile, g_sub, q16.reshape(bd, t, aw),
                              k32.reshape(bd, t, nh, HEAD_DIM), v32.reshape(bd, t, nh, HEAD_DIM),
                              cache_k.reshape(depth * bd, p, nh, HEAD_DIM),
                              cache_v.reshape(depth * bd, p, nh, HEAD_DIM), layer=l, lam_init=lam0)
        ys = _out_mlp_call(ys, oa16.reshape(bd * t, aw), ol16, *ln_args,
                           tm=_pick_tile(bd * t, MLP_ROWS), tf=tf, alpha=alpha)
        outs[4].append(k32.reshape(bd, t, nh, HEAD_DIM))
        outs[5].append(v32.reshape(bd, t, nh, HEAD_DIM))
        outs[6].append(hn.reshape(bd, lw))
        outs[7].append(cn)

    stk = [jnp.stack(o) for o in outs]
    return (yp.reshape(b, s, d), ys.reshape(bd, t, d), stk[0], stk[1], stk[2], stk[3],
            stk[4], stk[5], stk[6], stk[7])
```

```python
import functools
import math

import jax
import jax.numpy as jnp
from jax import lax
from jax.experimental import pallas as pl
from jax.experimental.pallas import tpu as pltpu

CHUNK = 64
HEAD_DIM = 128
HALF_DIM = HEAD_DIM // 2
LRU_BLOCKS = 16
CONV_WIDTH = 4
LRU_C = 8.0
ROPE_THETA = 10000.0
LN_EPS = 1e-5
RMS_EPS = 1e-5
NEG_INF = -1e30

SUBLANES = 8
LANES = 128
MXU_DIM = 256
ONES_ROWS = 16
VMEM_LIMIT_BYTES = 60 * 1024 * 1024

IN_ROWS = 512
ATTN_TILE = 512
DEC_KEY_CHUNK = 256
MLP_ROWS = 512
MLP_HIDDEN = 1024

Q_SCALE = HALF_DIM ** -0.5 * math.log2(math.e)

BF16 = jnp.bfloat16
F32 = jnp.float32


def _lambda_init(layer):
    return 0.8 - 0.6 * math.exp(-0.3 * layer)


def _dot(a, b):
    return jnp.dot(a, b, preferred_element_type=F32)


def _dot_nt(a, b):
    return lax.dot_general(a, b, (((1,), (1,)), ((), ())), preferred_element_type=F32)


def _layer_norm(z, g, b):
    mu = jnp.mean(z, axis=-1, keepdims=True)
    d = z - mu
    var = jnp.mean(d * d, axis=-1, keepdims=True)
    return d * lax.rsqrt(var + LN_EPS) * g + b


def _lam_kernel(q1_ref, k1_ref, q2_ref, k2_ref, o_ref, *, lam_init):
    s1 = jnp.sum(q1_ref[...] * k1_ref[...], axis=-1, keepdims=True)
    s2 = jnp.sum(q2_ref[...] * k2_ref[...], axis=-1, keepdims=True)
    lam = jnp.exp(s1) - jnp.exp(s2) + lam_init
    o_ref[...] = jnp.broadcast_to(lam, o_ref.shape)


def _lam_call(lq1, lk1, lq2, lk2, lam_init):
    return pl.pallas_call(
        functools.partial(_lam_kernel, lam_init=lam_init),
        out_shape=jax.ShapeDtypeStruct((SUBLANES, LANES), F32),
        name="lam",
    )(lq1, lk1, lq2, lk2)


def _group_scan(a, u, sub):
    for s in (1, 2, 4):
        keep = sub >= s
        a_sh = pltpu.roll(a, s, axis=0)
        u_sh = pltpu.roll(u, s, axis=0)
        u = u + jnp.where(keep, a * u_sh, 0.0)
        a = a * jnp.where(keep, a_sh, 1.0)
    return a, u


def _in_lru_kernel(x_ref, w_ref, cos_ref, sin_ref, cw_ref, cb_ref, wg_ref, ba_ref, bi_ref,
                   lam_ref, h0_ref, c0_ref, *outs_and_scratch,
                   tm, seq_rows, tiles_per_seq, aw, lw):
    carried = seq_rows == tm
    nseq = tm // seq_rows
    gps = seq_rows // SUBLANES
    i = pl.program_id(0)
    xp_scr, xc32_scr, xc16_scr, gate_scr, g_scr, h_scr = outs_and_scratch[-6:]
    if carried:
        qt_ref, k32_ref, k16_ref, v32_ref, vt_ref, ol_ref, hl_ref, cl_ref = outs_and_scratch[:-6]
    else:
        q_ref, k32_ref, v32_ref, ol_ref, hl_ref, cl_ref = outs_and_scratch[:-6]

    if carried:
        xb16 = x_ref[...].astype(BF16)
        proj = lambda c0, width: _dot(xb16, w_ref[:, c0:c0 + width])
    else:
        proj = lambda c0, width: w_ref[:, c0:c0 + width]

    cos = cos_ref[...]
    sin = sin_ref[...]
    lane = lax.broadcasted_iota(jnp.int32, (tm, LANES), 1)
    first_half = (lane % HALF_DIM) < (HALF_DIM // 2)

    def rope(xs):
        rot = jnp.where(first_half, pltpu.roll(xs, LANES - HALF_DIM // 2, axis=1),
                        pltpu.roll(xs, HALF_DIM // 2, axis=1))
        return xs * cos + rot * sin

    pieces_per_seg = aw // MXU_DIM
    n_pieces = 3 * pieces_per_seg

    def qkv_piece(p):
        seg, c0 = p // pieces_per_seg, (p % pieces_per_seg) * MXU_DIM
        pre = proj(seg * aw + c0, MXU_DIM)
        for h in range(MXU_DIM // LANES):
            sl = slice(c0 + h * LANES, c0 + (h + 1) * LANES)
            ph = pre[:, h * LANES:(h + 1) * LANES]
            head = c0 // LANES + h
            if seg == 0:
                qs = rope(ph) * Q_SCALE
                if carried:
                    qt_ref[head] = qs.T.astype(BF16)
                else:
                    q_ref[:, sl] = qs.astype(BF16)
            elif seg == 1:
                kr = rope(ph)
                k32_ref[:, sl] = kr
                if carried:
                    k16_ref[:, sl] = kr.astype(BF16)
            else:
                v32_ref[:, sl] = ph
                if carried:
                    vt_ref[head, 0:HEAD_DIM, :] = ph.T.astype(BF16)
                    vt_ref[head, HEAD_DIM:HEAD_DIM + ONES_ROWS, :] = jnp.ones((ONES_ROWS, tm), BF16)

    xbp = proj(3 * aw, lw)
    g_scr[...] = proj(3 * aw + lw, lw)

    if carried:
        @pl.when(i % tiles_per_seq == 0)
        def _():
            xp_scr[0, 0:SUBLANES, :] = jnp.zeros((SUBLANES, lw), F32)
            h_scr[...] = jnp.zeros_like(h_scr)
    else:
        for sq in range(nseq):
            xp_scr[sq, SUBLANES - (CONV_WIDTH - 1):SUBLANES, :] = c0_ref[sq]
    for sq in range(nseq):
        xp_scr[sq, SUBLANES:, :] = xbp[sq * seq_rows:(sq + 1) * seq_rows, :]

    sub = lax.broadcasted_iota(jnp.int32, (SUBLANES, lw), 0)
    cwb = [jnp.broadcast_to(cw_ref[j:j + 1, :], (SUBLANES, lw)) for j in range(CONV_WIDTH)]
    cbb = jnp.broadcast_to(cb_ref[...], (SUBLANES, lw))
    for sq in range(nseq):
        prev = xp_scr[sq, 0:SUBLANES, :]
        prev_rolled = [pltpu.roll(prev, d, axis=0) for d in range(1, CONV_WIDTH)]
        for gg in range(gps):
            r0 = SUBLANES + gg * SUBLANES
            cur = xp_scr[sq, r0:r0 + SUBLANES, :]
            cur_rolled = [pltpu.roll(cur, d, axis=0) for d in range(1, CONV_WIDTH)]
            xc = cbb + cwb[CONV_WIDTH - 1] * cur
            for d in range(1, CONV_WIDTH):
                shifted = jnp.where(sub >= d, cur_rolled[d - 1], prev_rolled[d - 1])
                xc = xc + cwb[CONV_WIDTH - 1 - d] * shifted
            prev_rolled = cur_rolled
            g0 = (sq * gps + gg) * SUBLANES
            xc32_scr[g0:g0 + SUBLANES, :] = xc
            xc16_scr[g0:g0 + SUBLANES, :] = xc.astype(BF16)

    for sq in range(nseq):
        cl_ref[sq] = xp_scr[sq, SUBLANES + seq_rows - (CONV_WIDTH - 1):SUBLANES + seq_rows, :]
    if carried:
        xp_scr[0, 0:SUBLANES, :] = xp_scr[0, seq_rows:seq_rows + SUBLANES, :]

    for blk in range(lw // MXU_DIM):
        sl = slice(blk * MXU_DIM, (blk + 1) * MXU_DIM)
        gate_scr[:, 2 * blk * MXU_DIM:2 * (blk + 1) * MXU_DIM] = _dot(xc16_scr[:, sl], wg_ref[blk])

    nl = -lam_ref[...]
    softplus = jnp.maximum(nl, 0.0) + jnp.log1p(jnp.exp(-jnp.abs(nl)))
    decay = jnp.broadcast_to(-LRU_C * softplus, (SUBLANES, lw))
    ba = jnp.broadcast_to(ba_ref[...], (SUBLANES, lw))
    bi = jnp.broadcast_to(bi_ref[...], (SUBLANES, lw))

    def sigmoid(z):
        return 0.5 * jnp.tanh(0.5 * z) + 0.5

    n_groups = nseq * gps
    pieces_done = 0
    after = None
    for sq in range(nseq):
        hin = h_scr[...] if carried else h0_ref[sq]
        for gg in range(gps):
            while pieces_done * n_groups < (sq * gps + gg) * n_pieces:
                qkv_piece(pieces_done)
                pieces_done += 1
            g0 = (sq * gps + gg) * SUBLANES
            rows = slice(g0, g0 + SUBLANES)
            pre = gate_scr[rows, :]
            ra = jnp.concatenate(
                [pre[:, 2 * b * MXU_DIM:(2 * b + 1) * MXU_DIM] for b in range(lw // MXU_DIM)], axis=1)
            ri = jnp.concatenate(
                [pre[:, (2 * b + 1) * MXU_DIM:(2 * b + 2) * MXU_DIM] for b in range(lw // MXU_DIM)], axis=1)
            if after is not None:
                ra = ra + after
                ri = ri + after
            r = sigmoid(ra + ba)
            ig = sigmoid(ri + bi)
            log_a = decay * r
            a = jnp.exp(log_a)
            y = (1.0 + a * a) * jnp.tanh(-log_a)
            root = jnp.where(y > 0.0, y * lax.rsqrt(y), 0.0)
            u = root * (ig * xc32_scr[rows, :])
            acum, ucum = _group_scan(a, u, sub)
            hs = acum * hin + ucum
            hin = hs[SUBLANES - 1:SUBLANES, :]
            out = hs * jax.nn.gelu(g_scr[rows, :])
            ol_ref[rows, :] = out.astype(BF16)
            after = pltpu.bitcast(
                lax.shift_right_logical(pltpu.bitcast(out, jnp.uint32), jnp.uint32(32)), F32)
        hl_ref[sq] = hin
        if carried:
            h_scr[...] = hin
    while pieces_done < n_pieces:
        qkv_piece(pieces_done)
        pieces_done += 1


def _cast_proj_kernel(x_ref, w_ref, w16_ref, p_ref):
    w16 = w_ref[...].astype(BF16)
    w16_ref[...] = w16
    p_ref[...] = _dot(x_ref[...].astype(BF16), w16)


def _cast_proj_call(x2d, w_in):
    rows, d_model = x2d.shape
    n = w_in.shape[1]
    col = pl.BlockSpec((d_model, MXU_DIM), lambda p: (0, p))
    return pl.pallas_call(
        _cast_proj_kernel,
        grid=(n // MXU_DIM,),
        in_specs=[pl.BlockSpec((rows, d_model), lambda p: (0, 0)), col],
        out_specs=[col, pl.BlockSpec((rows, MXU_DIM), lambda p: (0, p))],
        out_shape=[jax.ShapeDtypeStruct(w_in.shape, BF16), jax.ShapeDtypeStruct((rows, n), F32)],
        compiler_params=pltpu.CompilerParams(dimension_semantics=("arbitrary",),
                                             vmem_limit_bytes=VMEM_LIMIT_BYTES),
        name="cast_proj",
    )(x2d, w_in)


def _in_lru_call(x2d, w_in16, cos, sin, conv_w, conv_b, wg, b_a, b_i, lru_lambda, h0, c0,
                 *, tm, seq_rows, rows_per_batch, vt_tile):
    rows, d_model = x2d.shape
    aw = (w_in16.shape[1] - 2 * conv_w.shape[1]) // 3
    lw = conv_w.shape[1]
    nh = aw // HEAD_DIM
    carried = seq_rows == tm
    n_tiles = rows // tm
    tiles_per_seq = rows_per_batch // tm if carried else 1
    nseq_tile = tm // seq_rows
    nseq_total = rows // rows_per_batch
    if carried:
        state_map = lambda i: (i // tiles_per_seq, 0, 0)
    else:
        state_map = lambda i: (i, 0, 0)
    const2 = lambda i: (0, 0)
    const3 = lambda i: (0, 0, 0)
    row_map = lambda i: (i, 0)
    resident = dict(pipeline_mode=pl.Buffered(1))
    in_specs = [
        pl.BlockSpec((tm, d_model), row_map),
        pl.BlockSpec(w_in16.shape, const2, **resident) if carried
        else pl.BlockSpec((tm, w_in16.shape[1]), row_map),
        pl.BlockSpec((tm, LANES), (lambda i: (i % tiles_per_seq, 0)) if carried else const2),
        pl.BlockSpec((tm, LANES), (lambda i: (i % tiles_per_seq, 0)) if carried else const2),
        pl.BlockSpec(conv_w.shape, const2),
        pl.BlockSpec(conv_b.shape, const2),
        pl.BlockSpec(wg.shape, const3, **resident),
        pl.BlockSpec(b_a.shape, const2),
        pl.BlockSpec(b_i.shape, const2),
        pl.BlockSpec(lru_lambda.shape, const2),
        pl.BlockSpec((nseq_tile, 1, lw), state_map),
        pl.BlockSpec((nseq_tile, CONV_WIDTH - 1, lw), state_map),
    ]
    rows_aw = lambda dt: (jax.ShapeDtypeStruct((rows, aw), dt), pl.BlockSpec((tm, aw), row_map))
    k32_out, v32_out = rows_aw(F32), rows_aw(F32)
    tail_outs = [
        (jax.ShapeDtypeStruct((rows, lw), BF16), pl.BlockSpec((tm, lw), row_map)),
        (jax.ShapeDtypeStruct((nseq_total, 1, lw), F32), pl.BlockSpec((nseq_tile, 1, lw), state_map)),
        (jax.ShapeDtypeStruct((nseq_total, CONV_WIDTH - 1, lw), F32),
         pl.BlockSpec((nseq_tile, CONV_WIDTH - 1, lw), state_map)),
    ]
    if carried:
        t_lead = (nseq_total, nh, rows_per_batch // vt_tile)
        t_map = lambda i: (i // tiles_per_seq, 0, ((i % tiles_per_seq) * tm) // vt_tile, 0,
                           (((i % tiles_per_seq) * tm) % vt_tile) // tm)
        transposed = lambda r: (jax.ShapeDtypeStruct(t_lead + (r, vt_tile), BF16),
                                pl.BlockSpec((None, nh, None, r, tm), t_map))
        outs = [transposed(HEAD_DIM), k32_out, rows_aw(BF16), v32_out,
                transposed(HEAD_DIM + ONES_ROWS)] + tail_outs
    else:
        outs = [rows_aw(BF16), k32_out, v32_out] + tail_outs
    out_shape = tuple(o[0] for o in outs)
    out_specs = tuple(o[1] for o in outs)
    scratch = [
        pltpu.VMEM((nseq_tile, SUBLANES + seq_rows, lw), F32),
        pltpu.VMEM((tm, lw), F32),
        pltpu.VMEM((tm, lw), BF16),
        pltpu.VMEM((tm, 2 * lw), F32),
        pltpu.VMEM((tm, lw), F32),
        pltpu.VMEM((1, lw), F32),
    ]
    kern = functools.partial(_in_lru_kernel, tm=tm, seq_rows=seq_rows, tiles_per_seq=tiles_per_seq,
                             aw=aw, lw=lw)
    return pl.pallas_call(
        kern, grid=(n_tiles,), in_specs=in_specs, out_specs=out_specs, out_shape=out_shape,
        scratch_shapes=scratch,
        compiler_params=pltpu.CompilerParams(dimension_semantics=("arbitrary",),
                                             vmem_limit_bytes=VMEM_LIMIT_BYTES),
        name="in_lru",
    )(x2d, w_in16, cos, sin, conv_w, conv_b, wg, b_a, b_i, lru_lambda, h0, c0)


def _attn_kernel(lam_ref, g_ref, bias_ref, qt_ref, k_ref, vt_ref, *rest, t, n_q, lam_init):
    n_w = (len(rest) - 4) // 2
    w32_refs, o_ref, w16_refs = rest[:n_w], rest[n_w], rest[n_w + 1:2 * n_w + 1]
    s_scr, m_scr, acc_scr = rest[2 * n_w + 1:]

    def scores(i, j, slot):
        qt = qt_ref[i]
        dim = lax.broadcasted_iota(jnp.int32, (HEAD_DIM, t), 0)
        zero = jnp.zeros_like(qt)
        qq = jnp.concatenate([jnp.where(dim < HALF_DIM, qt, zero),
                              jnp.where(dim >= HALF_DIM, qt, zero)], axis=1)
        kj = k_ref[pl.ds(pl.multiple_of(j * t, t), t), :]
        s_scr[slot] = _dot(kj, qq)

    def softmax_pv(j, slot, masked):
        s = s_scr[slot]
        if masked:
            s = s + bias_ref[...]
        m = m_scr[...]
        m_new = jnp.maximum(m, jnp.max(s, axis=0, keepdims=True))
        alpha = jnp.exp2(m - m_new)
        p = jnp.exp2(s - m_new)
        m_scr[...] = m_new
        acc_scr[...] = alpha * acc_scr[...] + _dot(vt_ref[j], p.astype(BF16))

    def reset():
        m_scr[...] = jnp.full(m_scr.shape, NEG_INF, F32)
        acc_scr[...] = jnp.zeros(acc_scr.shape, F32)

    def finish(i):
        acc = acc_scr[...]
        o = acc[0:HEAD_DIM] / acc[HEAD_DIM:HEAD_DIM + 1]
        o = o[:, 0:t] - lam_ref[0:1, 0:1] * o[:, t:2 * t]
        ms = jnp.mean(o * o, axis=0, keepdims=True)
        o = o * lax.rsqrt(ms + RMS_EPS) * g_ref[...] * (1.0 - lam_init)
        o_ref[pl.ds(pl.multiple_of(i * t, t), t), :] = o.T.astype(BF16)
        reset()

    def following(i, j):
        edge = j == i
        return jnp.where(edge, i + 1, i), jnp.where(edge, 0, j + 1)

    def pair(_, ij):
        ia, ja = ij
        ib, jb = following(ia, ja)
        ic, jc = following(ib, jb)
        icc = jnp.minimum(ic, n_q - 1)
        edge_a = ja == ia
        edge_b = jb == ib

        def body(mask_a, mask_b):
            scores(ib, jb, 1)
            softmax_pv(ja, 0, mask_a)
            if mask_a:
                finish(ia)
            scores(icc, jc, 0)
            softmax_pv(jb, 1, mask_b)
            if mask_b:
                finish(ib)

        pl.when(edge_a)(lambda: body(True, False))
        pl.when(edge_b)(lambda: body(False, True))
        pl.when(jnp.logical_not(jnp.logical_or(edge_a, edge_b)))(lambda: body(False, False))
        return ic, jc

    reset()
    scores(0, 0, 0)
    for w32_ref, w16_ref in zip(w32_refs, w16_refs):
        w16_ref[...] = w32_ref[...].astype(BF16)
    n_items = n_q * (n_q + 1) // 2
    lax.fori_loop(0, n_items // 2, pair, (jnp.int32(0), jnp.int32(0)))
    if n_items % 2:
        softmax_pv(n_q - 1, 0, True)
        finish(n_q - 1)


def _attn_call(lam_tile, subln_g, qt, k16, vt, cast_rows, cast_cols, *, t, lam_init):
    b, s, aw = k16.shape
    nh = aw // HEAD_DIM
    steps = b * nh
    key_chunk = jnp.arange(t, dtype=jnp.int32)[:, None] // CHUNK
    qry_chunk = (jnp.arange(2 * t, dtype=jnp.int32)[None, :] % t) // CHUNK
    bias = jnp.where(key_chunk <= qry_chunk, 0.0, NEG_INF).astype(F32)
    seq_spec = pl.BlockSpec((None, s, HEAD_DIM), lambda bi, hi: (bi, 0, hi))
    const = lambda bi, hi: (0, 0)
    w_specs = [pl.BlockSpec((w.shape[0] // steps, w.shape[1]), lambda bi, hi: (bi * nh + hi, 0))
               for w in cast_rows]
    w_specs += [pl.BlockSpec((w.shape[0], w.shape[1] // steps), lambda bi, hi: (0, bi * nh + hi))
                for w in cast_cols]
    weights = list(cast_rows) + list(cast_cols)
    return pl.pallas_call(
        functools.partial(_attn_kernel, t=t, n_q=s // t, lam_init=lam_init),
        grid=(b, nh),
        in_specs=[pl.BlockSpec((SUBLANES, LANES), const),
                  pl.BlockSpec((HEAD_DIM, 1), const),
                  pl.BlockSpec((t, 2 * t), const),
                  pl.BlockSpec((None, None, s // t, HEAD_DIM, t), lambda bi, hi: (bi, hi, 0, 0, 0)),
                  seq_spec,
                  pl.BlockSpec((None, None, s // t, HEAD_DIM + ONES_ROWS, t),
                               lambda bi, hi: (bi, hi, 0, 0, 0))] + w_specs,
        out_specs=[seq_spec] + w_specs,
        out_shape=[jax.ShapeDtypeStruct((b, s, aw), BF16)]
        + [jax.ShapeDtypeStruct(w.shape, BF16) for w in weights],
        scratch_shapes=[pltpu.VMEM((2, t, 2 * t), F32),
                        pltpu.VMEM((1, 2 * t), F32),
                        pltpu.VMEM((HEAD_DIM + ONES_ROWS, 2 * t), F32)],
        compiler_params=pltpu.CompilerParams(
            dimension_semantics=("arbitrary", "arbitrary"),
            vmem_limit_bytes=VMEM_LIMIT_BYTES),
        name="attn",
    )(lam_tile, subln_g.reshape(HEAD_DIM, 1), bias, qt, k16, vt, *weights)


def _dec_attn_kernel(lam_ref, g_ref, bias_ref, q_ref, kn_ref, vn_ref, ck_ref, cv_ref, o_ref,
                     *, t, nh, kc, lam_init):
    rows = 2 * nh * t
    p_len = ck_ref.shape[0]
    q = q_ref[...]
    lane = lax.broadcasted_iota(jnp.int32, (t, HEAD_DIM), 1)
    parts = []
    for c in range(2):
        keep = (lane < HALF_DIM) if c == 0 else (lane >= HALF_DIM)
        for h in range(nh):
            qh = q[:, h * HEAD_DIM:(h + 1) * HEAD_DIM]
            parts.append(jnp.where(keep, qh, jnp.zeros_like(qh)))
    qall = jnp.concatenate(parts, axis=0)

    def scores(kf):
        s = _dot_nt(qall, kf.astype(BF16))
        return s + bias_ref[:, 0:s.shape[1]]

    def update(carry, s, vf):
        m, l, acc = carry
        m_new = jnp.maximum(m, jnp.max(s, axis=-1, keepdims=True))
        alpha = jnp.exp2(m - m_new)
        p = jnp.exp2(s - m_new)
        l = alpha * l + jnp.sum(p, axis=-1, keepdims=True)
        acc = alpha * acc + _dot(p.astype(BF16), vf.astype(BF16))
        return m_new, l, acc

    carry = (jnp.full((rows, 1), NEG_INF, F32), jnp.zeros((rows, 1), F32),
             jnp.zeros((rows, HEAD_DIM), F32))
    n_chunks = p_len // kc
    s_next = scores(ck_ref[0:kc].reshape(kc * nh, HEAD_DIM))
    for j in range(n_chunks):
        s_cur = s_next
        if j + 1 < n_chunks:
            s_next = scores(ck_ref[(j + 1) * kc:(j + 2) * kc].reshape(kc * nh, HEAD_DIM))
        else:
            s_next = scores(kn_ref[...].reshape(t * nh, HEAD_DIM))
        carry = update(carry, s_cur, cv_ref[j * kc:(j + 1) * kc].reshape(kc * nh, HEAD_DIM))
    _, l, acc = update(carry, s_next, vn_ref[...].reshape(t * nh, HEAD_DIM))

    o = acc / l
    half = rows // 2
    o = o[0:half] - lam_ref[0:1, 0:1] * o[half:rows]
    ms = jnp.mean(o * o, axis=-1, keepdims=True)
    o = o * lax.rsqrt(ms + RMS_EPS) * g_ref[...] * (1.0 - lam_init)
    for h in range(nh):
        o_ref[:, h * HEAD_DIM:(h + 1) * HEAD_DIM] = o[h * t:(h + 1) * t].astype(BF16)


def _dec_attn_call(lam_tile, subln_g, q16, kn32, vn32, cache_k, cache_v, *, layer, lam_init):
    bd, t, aw = q16.shape
    _, p, nh, hd = cache_k.shape
    new_spec = pl.BlockSpec((None, t, nh, hd), lambda b: (b, 0, 0, 0))
    cache_spec = pl.BlockSpec((None, p, nh, hd), lambda b: (layer * bd + b, 0, 0, 0))
    io_spec = pl.BlockSpec((None, t, aw), lambda b: (b, 0, 0))
    kc = _pick_tile(p, DEC_KEY_CHUNK)
    ncol = max(kc, t) * nh
    row_head = (jnp.arange(2 * nh * t, dtype=jnp.int32)[:, None] // t) % nh
    col_head = jnp.arange(ncol, dtype=jnp.int32)[None, :] % nh
    bias = jnp.where(row_head == col_head, 0.0, NEG_INF).astype(F32)
    return pl.pallas_call(
        functools.partial(_dec_attn_kernel, t=t, nh=nh, kc=kc, lam_init=lam_init),
        grid=(bd,),
        in_specs=[pl.BlockSpec((SUBLANES, LANES), lambda b: (0, 0)),
                  pl.BlockSpec((1, HEAD_DIM), lambda b: (0, 0)),
                  pl.BlockSpec(bias.shape, lambda b: (0, 0)),
                  io_spec, new_spec, new_spec, cache_spec, cache_spec],
        out_specs=io_spec,
        out_shape=jax.ShapeDtypeStruct((bd, t, aw), BF16),
        compiler_params=pltpu.CompilerParams(dimension_semantics=("arbitrary",),
                                             vmem_limit_bytes=VMEM_LIMIT_BYTES),
        name="dec_attn",
    )(lam_tile, subln_g, bias, q16, kn32, vn32, cache_k, cache_v)


def _out_mlp_kernel(x_ref, oa_ref, ol_ref, wo_ref, g1_ref, b1_ref, wu_ref, wd_ref, g2_ref, b2_ref,
                    y_ref, x1_scr, *, alpha):
    j = pl.program_id(1)

    @pl.when(j == 0)
    def _():
        cat = jnp.concatenate([oa_ref[...], ol_ref[...]], axis=1)
        x1 = _layer_norm(alpha * x_ref[...] + _dot(cat, wo_ref[...]), g1_ref[...], b1_ref[...])
        x1_scr[...] = x1.astype(BF16)
        y_ref[...] = alpha * x1

    hid = jnp.square(jnp.maximum(_dot(x1_scr[...], wu_ref[...]), 0.0)).astype(BF16)
    y_ref[...] += _dot(hid, wd_ref[...])

    @pl.when(j == pl.num_programs(1) - 1)
    def _():
        y_ref[...] = _layer_norm(y_ref[...], g2_ref[...], b2_ref[...])


def _out_mlp_call(x2d, oa16, ol16, w_out16, g1, b1, w_up16, w_down16, g2, b2, *, tm, tf, alpha):
    rows, d = x2d.shape
    aw = oa16.shape[1]
    lw = ol16.shape[1]
    d_ff = w_up16.shape[1]
    row = lambda i, j: (i, 0)
    const = lambda i, j: (0, 0)
    return pl.pallas_call(
        functools.partial(_out_mlp_kernel, alpha=alpha),
        grid=(rows // tm, d_ff // tf),
        in_specs=[pl.BlockSpec((tm, d), row),
                  pl.BlockSpec((tm, aw), row),
                  pl.BlockSpec((tm, lw), row),
                  pl.BlockSpec(w_out16.shape, const, pipeline_mode=pl.Buffered(1)),
                  pl.BlockSpec((1, d), const), pl.BlockSpec((1, d), const),
                  pl.BlockSpec((d, tf), lambda i, j: (0, j)),
                  pl.BlockSpec((tf, d), lambda i, j: (j, 0)),
                  pl.BlockSpec((1, d), const), pl.BlockSpec((1, d), const)],
        out_specs=pl.BlockSpec((tm, d), row),
        out_shape=jax.ShapeDtypeStruct((rows, d), F32),
        scratch_shapes=[pltpu.VMEM((tm, d), BF16)],
        compiler_params=pltpu.CompilerParams(dimension_semantics=("arbitrary", "arbitrary"),
                                             vmem_limit_bytes=VMEM_LIMIT_BYTES),
        name="out_mlp",
    )(x2d, oa16, ol16, w_out16, g1, b1, w_up16, w_down16, g2, b2)


def _rope_tables(pos):
    inv = ROPE_THETA ** (-jnp.arange(0, HALF_DIM, 2, dtype=F32) / HALF_DIM)
    ang = pos.astype(F32)[:, None] * inv[None, :]
    cos = jnp.tile(jnp.cos(ang), (1, 4))
    sin = jnp.sin(ang)
    sin = jnp.tile(jnp.concatenate([-sin, sin], axis=-1), (1, 2))
    return cos, sin


def _gate_weights(w_a, w_i):
    per = MXU_DIM // w_a.shape[-1]

    def dense(w):
        nb, bd, _ = w.shape
        w4 = w.reshape(nb // per, per, bd, bd)
        eye = jnp.eye(per, dtype=w.dtype)
        return jnp.einsum('gpij,pq->gpiqj', w4, eye).reshape(nb // per, per * bd, per * bd)

    return jnp.concatenate([dense(w_a), dense(w_i)], axis=-1).astype(BF16)


def _pick_tile(n, pref):
    t = min(n, pref)
    while n % t:
        t //= 2
    return t


def kernel(x_prompt, x_sample, cache_k, cache_v, state_h, state_conv, w_in, lambda_q1, lambda_k1, lambda_q2, lambda_k2, subln_g, conv_w, conv_b, w_rg_a, b_rg_a, w_rg_i, b_rg_i, lru_lambda, w_out, ln1_g, ln1_b, w_up, w_down, ln2_g, ln2_b):
    depth = w_in.shape[0]
    b, s, d = x_prompt.shape
    bd, t, _ = x_sample.shape
    p = cache_k.shape[2]
    lw = conv_w.shape[-1]
    aw = (w_in.shape[-1] - 2 * lw) // 3
    nh = aw // HEAD_DIM
    alpha = (2.0 * depth) ** 0.25

    cos_p, sin_p = _rope_tables(jnp.arange(s, dtype=jnp.int32))
    cos_s, sin_s = _rope_tables(p + jnp.arange(t, dtype=jnp.int32))
    cos_s = jnp.tile(cos_s, (bd, 1))
    sin_s = jnp.tile(sin_s, (bd, 1))

    tm_a = _pick_tile(s, IN_ROWS)
    tq = _pick_tile(s, ATTN_TILE)
    tm_c = _pick_tile(b * s, MLP_ROWS)
    tf = _pick_tile(w_up.shape[-1], MLP_HIDDEN)

    yp = x_prompt.reshape(b * s, d)
    ys = x_sample.reshape(bd * t, d)
    outs = [[] for _ in range(8)]
    for l in range(depth):
        lam0 = _lambda_init(l)
        w_in16, proj_s = _cast_proj_call(ys, w_in[l])
        wg = _gate_weights(w_rg_a[l], w_rg_i[l])
        row = lambda v: v[l].reshape(1, -1)
        lam_tile = _lam_call(row(lambda_q1), row(lambda_k1), row(lambda_q2), row(lambda_k2), lam0)
        g_sub = row(subln_g)
        lru_args = (conv_w[l], row(conv_b), wg, row(b_rg_a), row(b_rg_i), row(lru_lambda))

        qt16, k32, k16, v32, vt16, ol16, hp, cp = _in_lru_call(
            yp, w_in16, cos_p, sin_p, *lru_args,
            jnp.zeros((b, 1, lw), F32), jnp.zeros((b, CONV_WIDTH - 1, lw), F32),
            tm=tm_a, seq_rows=tm_a, rows_per_batch=s, vt_tile=tq)
        oa16, w_out16, w_down16, w_up16 = _attn_call(
            lam_tile, g_sub, qt16, k16.reshape(b, s, aw), vt16, (w_out[l], w_down[l]), (w_up[l],),
            t=tq, lam_init=lam0)
        ln_args = (w_out16, row(ln1_g), row(ln1_b), w_up16, w_down16, row(ln2_g), row(ln2_b))
        yp = _out_mlp_call(yp, oa16.reshape(b * s, aw), ol16, *ln_args, tm=tm_c, tf=tf, alpha=alpha)
        outs[0].append(k32.reshape(b, s, nh, HEAD_DIM))
        outs[1].append(v32.reshape(b, s, nh, HEAD_DIM))
        outs[2].append(hp.reshape(b, lw))
        outs[3].append(cp)

        yp, proj_s = lax.optimization_barrier((yp, proj_s))
        q16, k32, v32, ol16, hn, cn = _in_lru_call(
            ys, proj_s, cos_s, sin_s, *lru_args,
            state_h[l].reshape(bd, 1, lw), state_conv[l],
            tm=bd * t, seq_rows=t, rows_per_batch=t, vt_tile=bd * t)
        oa16 = _dec_attn_call(lam_tile, g_sub, q16.reshape(bd, t, aw),
                              k32.reshape(bd, t, nh, HEAD_DIM), v32.reshape(bd, t, nh, HEAD_DIM),
                              cache_k.reshape(depth * bd, p, nh, HEAD_DIM),
                              cache_v.reshape(depth * bd, p, nh, HEAD_DIM), layer=l, lam_init=lam0)
        ys = _out_mlp_call(ys, oa16.reshape(bd * t, aw), ol16, *ln_args,
                           tm=_pick_tile(bd * t, MLP_ROWS), tf=tf, alpha=alpha)
        outs[4].append(k32.reshape(bd, t, nh, HEAD_DIM))
        outs[5].append(v32.reshape(bd, t, nh, HEAD_DIM))
        outs[6].append(hn.reshape(bd, lw))
        outs[7].append(cn)

    stk = [jnp.stack(o) for o in outs]
    return (yp.reshape(b, s, d), ys.reshape(bd, t, d), stk[0], stk[1], stk[2], stk[3],
            stk[4], stk[5], stk[6], stk[7])
```

```python
import functools
import math

import jax
import jax.numpy as jnp
from jax import lax
from jax.experimental import pallas as pl
from jax.experimental.pallas import tpu as pltpu

CHUNK = 64
HEAD_DIM = 128
HALF_DIM = HEAD_DIM // 2
LRU_BLOCKS = 16
CONV_WIDTH = 4
LRU_C = 8.0
ROPE_THETA = 10000.0
LN_EPS = 1e-5
RMS_EPS = 1e-5
NEG_INF = -1e30

SUBLANES = 8
LANES = 128
MXU_DIM = 256
ONES_ROWS = 16
VMEM_LIMIT_BYTES = 60 * 1024 * 1024

IN_ROWS = 512
ATTN_TILE = 512
DEC_KEY_CHUNK = 256
MLP_ROWS = 512
MLP_HIDDEN = 1024

Q_SCALE = HALF_DIM ** -0.5 * math.log2(math.e)

BF16 = jnp.bfloat16
F32 = jnp.float32


def _lambda_init(layer):
    return 0.8 - 0.6 * math.exp(-0.3 * layer)


def _dot(a, b):
    return jnp.dot(a, b, preferred_element_type=F32)


def _dot_nt(a, b):
    return lax.dot_general(a, b, (((1,), (1,)), ((), ())), preferred_element_type=F32)


def _layer_norm(z, g, b):
    mu = jnp.mean(z, axis=-1, keepdims=True)
    d = z - mu
    var = jnp.mean(d * d, axis=-1, keepdims=True)
    return d * lax.rsqrt(var + LN_EPS) * g + b


def _lam_kernel(q1_ref, k1_ref, q2_ref, k2_ref, o_ref, *, lam_init):
    s1 = jnp.sum(q1_ref[...] * k1_ref[...], axis=-1, keepdims=True)
    s2 = jnp.sum(q2_ref[...] * k2_ref[...], axis=-1, keepdims=True)
    lam = jnp.exp(s1) - jnp.exp(s2) + lam_init
    o_ref[...] = jnp.broadcast_to(lam, o_ref.shape)


def _lam_call(lq1, lk1, lq2, lk2, lam_init):
    return pl.pallas_call(
        functools.partial(_lam_kernel, lam_init=lam_init),
        out_shape=jax.ShapeDtypeStruct((SUBLANES, LANES), F32),
        name="lam",
    )(lq1, lk1, lq2, lk2)


def _group_scan(a, u, sub):
    for s in (1, 2, 4):
        keep = sub >= s
        a_sh = pltpu.roll(a, s, axis=0)
        u_sh = pltpu.roll(u, s, axis=0)
        u = u + jnp.where(keep, a * u_sh, 0.0)
        a = a * jnp.where(keep, a_sh, 1.0)
    return a, u


def _in_lru_kernel(x_ref, w_ref, cos_ref, sin_ref, cw_ref, cb_ref, wg_ref, ba_ref, bi_ref,
                   lam_ref, h0_ref, c0_ref, *outs_and_scratch,
                   tm, seq_rows, tiles_per_seq, aw, lw):
    carried = seq_rows == tm
    nseq = tm // seq_rows
    gps = seq_rows // SUBLANES
    nh = aw // HEAD_DIM
    i = pl.program_id(0)
    xp_scr, xc32_scr, xc16_scr, gate_scr, g_scr, h_scr = outs_and_scratch[-6:]
    if carried:
        qt_ref, k32_ref, k16_ref, v32_ref, vt_ref, ol_ref, hl_ref, cl_ref = outs_and_scratch[:-6]
    else:
        q_ref, k32_ref, v32_ref, ol_ref, hl_ref, cl_ref = outs_and_scratch[:-6]

    if carried:
        xb16 = x_ref[...].astype(BF16)
        proj = lambda c0, width: _dot(xb16, w_ref[:, c0:c0 + width])
    else:
        proj = lambda c0, width: w_ref[:, c0:c0 + width]

    cos = cos_ref[...]
    sin = sin_ref[...]
    lane = lax.broadcasted_iota(jnp.int32, (tm, LANES), 1)
    first_half = (lane % HALF_DIM) < (HALF_DIM // 2)

    def rope(xs):
        rot = jnp.where(first_half, pltpu.roll(xs, LANES - HALF_DIM // 2, axis=1),
                        pltpu.roll(xs, HALF_DIM // 2, axis=1))
        return xs * cos + rot * sin

    pieces_per_seg = aw // MXU_DIM
    n_pieces = 3 * pieces_per_seg

    def qkv_piece(p):
        seg, c0 = p // pieces_per_seg, (p % pieces_per_seg) * MXU_DIM
        pre = proj(seg * aw + c0, MXU_DIM)
        for h in range(MXU_DIM // LANES):
            sl = slice(c0 + h * LANES, c0 + (h + 1) * LANES)
            ph = pre[:, h * LANES:(h + 1) * LANES]
            head = c0 // LANES + h
            if seg == 0:
                qs = rope(ph) * Q_SCALE
                if carried:
                    qt_ref[head] = qs.T.astype(BF16)
                else:
                    q_ref[:, sl] = qs.astype(BF16)
            elif seg == 1:
                kr = rope(ph)
                k32_ref[pl.ds(head, tm, stride=nh), :] = kr
                if carried:
                    k16_ref[:, sl] = kr.astype(BF16)
            else:
                v32_ref[pl.ds(head, tm, stride=nh), :] = ph
                if carried:
                    vt_ref[head, 0:HEAD_DIM, :] = ph.T.astype(BF16)
                    vt_ref[head, HEAD_DIM:HEAD_DIM + ONES_ROWS, :] = jnp.ones((ONES_ROWS, tm), BF16)

    xbp = proj(3 * aw, lw)
    g_scr[...] = proj(3 * aw + lw, lw)

    if carried:
        @pl.when(i % tiles_per_seq == 0)
        def _():
            xp_scr[0, 0:SUBLANES, :] = jnp.zeros((SUBLANES, lw), F32)
            h_scr[...] = jnp.zeros_like(h_scr)
    else:
        for sq in range(nseq):
            xp_scr[sq, SUBLANES - (CONV_WIDTH - 1):SUBLANES, :] = c0_ref[sq]
    for sq in range(nseq):
        xp_scr[sq, SUBLANES:, :] = xbp[sq * seq_rows:(sq + 1) * seq_rows, :]

    sub = lax.broadcasted_iota(jnp.int32, (SUBLANES, lw), 0)
    cwb = [jnp.broadcast_to(cw_ref[j:j + 1, :], (SUBLANES, lw)) for j in range(CONV_WIDTH)]
    cbb = jnp.broadcast_to(cb_ref[...], (SUBLANES, lw))
    for sq in range(nseq):
        prev = xp_scr[sq, 0:SUBLANES, :]
        prev_rolled = [pltpu.roll(prev, d, axis=0) for d in range(1, CONV_WIDTH)]
        for gg in range(gps):
            r0 = SUBLANES + gg * SUBLANES
            cur = xp_scr[sq, r0:r0 + SUBLANES, :]
            cur_rolled = [pltpu.roll(cur, d, axis=0) for d in range(1, CONV_WIDTH)]
            xc = cbb + cwb[CONV_WIDTH - 1] * cur
            for d in range(1, CONV_WIDTH):
                shifted = jnp.where(sub >= d, cur_rolled[d - 1], prev_rolled[d - 1])
                xc = xc + cwb[CONV_WIDTH - 1 - d] * shifted
            prev_rolled = cur_rolled
            g0 = (sq * gps + gg) * SUBLANES
            xc32_scr[g0:g0 + SUBLANES, :] = xc
            xc16_scr[g0:g0 + SUBLANES, :] = xc.astype(BF16)

    for sq in range(nseq):
        cl_ref[sq] = xp_scr[sq, SUBLANES + seq_rows - (CONV_WIDTH - 1):SUBLANES + seq_rows, :]
    if carried:
        xp_scr[0, 0:SUBLANES, :] = xp_scr[0, seq_rows:seq_rows + SUBLANES, :]

    for blk in range(lw // MXU_DIM):
        sl = slice(blk * MXU_DIM, (blk + 1) * MXU_DIM)
        gate_scr[:, 2 * blk * MXU_DIM:2 * (blk + 1) * MXU_DIM] = _dot(xc16_scr[:, sl], wg_ref[blk])

    nl = -lam_ref[...]
    softplus = jnp.maximum(nl, 0.0) + jnp.log1p(jnp.exp(-jnp.abs(nl)))
    decay = jnp.broadcast_to(-LRU_C * softplus, (SUBLANES, lw))
    ba = jnp.broadcast_to(ba_ref[...], (SUBLANES, lw))
    bi = jnp.broadcast_to(bi_ref[...], (SUBLANES, lw))

    def sigmoid(z):
        return 0.5 * jnp.tanh(0.5 * z) + 0.5

    n_groups = nseq * gps
    pieces_done = 0
    after = None
    for sq in range(nseq):
        hin = h_scr[...] if carried else h0_ref[sq]
        for gg in range(gps):
            while pieces_done * n_groups < (sq * gps + gg) * n_pieces:
                qkv_piece(pieces_done)
                pieces_done += 1
            g0 = (sq * gps + gg) * SUBLANES
            rows = slice(g0, g0 + SUBLANES)
            pre = gate_scr[rows, :]
            ra = jnp.concatenate(
                [pre[:, 2 * b * MXU_DIM:(2 * b + 1) * MXU_DIM] for b in range(lw // MXU_DIM)], axis=1)
            ri = jnp.concatenate(
                [pre[:, (2 * b + 1) * MXU_DIM:(2 * b + 2) * MXU_DIM] for b in range(lw // MXU_DIM)], axis=1)
            if after is not None:
                ra = ra + after
                ri = ri + after
            r = sigmoid(ra + ba)
            ig = sigmoid(ri + bi)
            log_a = decay * r
            a = jnp.exp(log_a)
            y = (1.0 + a * a) * jnp.tanh(-log_a)
            root = jnp.where(y > 0.0, y * lax.rsqrt(y), 0.0)
            u = root * (ig * xc32_scr[rows, :])
            acum, ucum = _group_scan(a, u, sub)
            hs = acum * hin + ucum
            hin = hs[SUBLANES - 1:SUBLANES, :]
            out = hs * jax.nn.gelu(g_scr[rows, :])
            ol_ref[rows, :] = out.astype(BF16)
            after = pltpu.bitcast(
                lax.shift_right_logical(pltpu.bitcast(out, jnp.uint32), jnp.uint32(32)), F32)
        hl_ref[sq] = hin
        if carried:
            h_scr[...] = hin
    while pieces_done < n_pieces:
        qkv_piece(pieces_done)
        pieces_done += 1


def _cast_proj_kernel(x_ref, w_ref, w16_ref, p_ref):
    w16 = w_ref[...].astype(BF16)
    w16_ref[...] = w16
    p_ref[...] = _dot(x_ref[...].astype(BF16), w16)


def _cast_proj_call(x2d, w_in):
    rows, d_model = x2d.shape
    n = w_in.shape[1]
    col = pl.BlockSpec((d_model, MXU_DIM), lambda p: (0, p))
    return pl.pallas_call(
        _cast_proj_kernel,
        grid=(n // MXU_DIM,),
        in_specs=[pl.BlockSpec((rows, d_model), lambda p: (0, 0)), col],
        out_specs=[col, pl.BlockSpec((rows, MXU_DIM), lambda p: (0, p))],
        out_shape=[jax.ShapeDtypeStruct(w_in.shape, BF16), jax.ShapeDtypeStruct((rows, n), F32)],
        compiler_params=pltpu.CompilerParams(dimension_semantics=("arbitrary",),
                                             vmem_limit_bytes=VMEM_LIMIT_BYTES),
        name="cast_proj",
    )(x2d, w_in)


def _in_lru_call(x2d, w_in16, cos, sin, conv_w, conv_b, wg, b_a, b_i, lru_lambda, h0, c0,
                 *, tm, seq_rows, rows_per_batch, vt_tile):
    rows, d_model = x2d.shape
    aw = (w_in16.shape[1] - 2 * conv_w.shape[1]) // 3
    lw = conv_w.shape[1]
    nh = aw // HEAD_DIM
    carried = seq_rows == tm
    n_tiles = rows // tm
    tiles_per_seq = rows_per_batch // tm if carried else 1
    nseq_tile = tm // seq_rows
    nseq_total = rows // rows_per_batch
    if carried:
        state_map = lambda i: (i // tiles_per_seq, 0, 0)
    else:
        state_map = lambda i: (i, 0, 0)
    const2 = lambda i: (0, 0)
    const3 = lambda i: (0, 0, 0)
    row_map = lambda i: (i, 0)
    resident = dict(pipeline_mode=pl.Buffered(1))
    in_specs = [
        pl.BlockSpec((tm, d_model), row_map),
        pl.BlockSpec(w_in16.shape, const2, **resident) if carried
        else pl.BlockSpec((tm, w_in16.shape[1]), row_map),
        pl.BlockSpec((tm, LANES), (lambda i: (i % tiles_per_seq, 0)) if carried else const2),
        pl.BlockSpec((tm, LANES), (lambda i: (i % tiles_per_seq, 0)) if carried else const2),
        pl.BlockSpec(conv_w.shape, const2),
        pl.BlockSpec(conv_b.shape, const2),
        pl.BlockSpec(wg.shape, const3, **resident),
        pl.BlockSpec(b_a.shape, const2),
        pl.BlockSpec(b_i.shape, const2),
        pl.BlockSpec(lru_lambda.shape, const2),
        pl.BlockSpec((nseq_tile, 1, lw), state_map),
        pl.BlockSpec((nseq_tile, CONV_WIDTH - 1, lw), state_map),
    ]
    rows_aw = lambda dt: (jax.ShapeDtypeStruct((rows, aw), dt), pl.BlockSpec((tm, aw), row_map))
    k32_out = v32_out = (jax.ShapeDtypeStruct((rows * nh, HEAD_DIM), F32),
                         pl.BlockSpec((tm * nh, HEAD_DIM), row_map))
    tail_outs = [
        (jax.ShapeDtypeStruct((rows, lw), BF16), pl.BlockSpec((tm, lw), row_map)),
        (jax.ShapeDtypeStruct((nseq_total, 1, lw), F32), pl.BlockSpec((nseq_tile, 1, lw), state_map)),
        (jax.ShapeDtypeStruct((nseq_total, CONV_WIDTH - 1, lw), F32),
         pl.BlockSpec((nseq_tile, CONV_WIDTH - 1, lw), state_map)),
    ]
    if carried:
        t_lead = (nseq_total, nh, rows_per_batch // vt_tile)
        t_map = lambda i: (i // tiles_per_seq, 0, ((i % tiles_per_seq) * tm) // vt_tile, 0,
                           (((i % tiles_per_seq) * tm) % vt_tile) // tm)
        transposed = lambda r: (jax.ShapeDtypeStruct(t_lead + (r, vt_tile), BF16),
                                pl.BlockSpec((None, nh, None, r, tm), t_map))
        outs = [transposed(HEAD_DIM), k32_out, rows_aw(BF16), v32_out,
                transposed(HEAD_DIM + ONES_ROWS)] + tail_outs
    else:
        outs = [rows_aw(BF16), k32_out, v32_out] + tail_outs
    out_shape = tuple(o[0] for o in outs)
    out_specs = tuple(o[1] for o in outs)
    scratch = [
        pltpu.VMEM((nseq_tile, SUBLANES + seq_rows, lw), F32),
        pltpu.VMEM((tm, lw), F32),
        pltpu.VMEM((tm, lw), BF16),
        pltpu.VMEM((tm, 2 * lw), F32),
        pltpu.VMEM((tm, lw), F32),
        pltpu.VMEM((1, lw), F32),
    ]
    kern = functools.partial(_in_lru_kernel, tm=tm, seq_rows=seq_rows, tiles_per_seq=tiles_per_seq,
                             aw=aw, lw=lw)
    return pl.pallas_call(
        kern, grid=(n_tiles,), in_specs=in_specs, out_specs=out_specs, out_shape=out_shape,
        scratch_shapes=scratch,
        compiler_params=pltpu.CompilerParams(dimension_semantics=("arbitrary",),
                                             vmem_limit_bytes=VMEM_LIMIT_BYTES),
        name="in_lru",
    )(x2d, w_in16, cos, sin, conv_w, conv_b, wg, b_a, b_i, lru_lambda, h0, c0)


def _attn_kernel(lam_ref, g_ref, bias_ref, qt_ref, k_ref, vt_ref, *rest, t, n_q, lam_init):
    n_w = (len(rest) - 4) // 2
    w32_refs, o_ref, w16_refs = rest[:n_w], rest[n_w], rest[n_w + 1:2 * n_w + 1]
    s_scr, m_scr, acc_scr = rest[2 * n_w + 1:]

    def scores(i, j, slot):
        qt = qt_ref[i]
        dim = lax.broadcasted_iota(jnp.int32, (HEAD_DIM, t), 0)
        zero = jnp.zeros_like(qt)
        qq = jnp.concatenate([jnp.where(dim < HALF_DIM, qt, zero),
                              jnp.where(dim >= HALF_DIM, qt, zero)], axis=1)
        kj = k_ref[pl.ds(pl.multiple_of(j * t, t), t), :]
        s_scr[slot] = _dot(kj, qq)

    def softmax_pv(j, slot, masked):
        s = s_scr[slot]
        if masked:
            s = s + bias_ref[...]
        m = m_scr[...]
        m_new = jnp.maximum(m, jnp.max(s, axis=0, keepdims=True))
        alpha = jnp.exp2(m - m_new)
        p = jnp.exp2(s - m_new)
        m_scr[...] = m_new
        acc_scr[...] = alpha * acc_scr[...] + _dot(vt_ref[j], p.astype(BF16))

    def reset():
        m_scr[...] = jnp.full(m_scr.shape, NEG_INF, F32)
        acc_scr[...] = jnp.zeros(acc_scr.shape, F32)

    def finish(i):
        acc = acc_scr[...]
        o = acc[0:HEAD_DIM] / acc[HEAD_DIM:HEAD_DIM + 1]
        o = o[:, 0:t] - lam_ref[0:1, 0:1] * o[:, t:2 * t]
        ms = jnp.mean(o * o, axis=0, keepdims=True)
        o = o * lax.rsqrt(ms + RMS_EPS) * g_ref[...] * (1.0 - lam_init)
        o_ref[pl.ds(pl.multiple_of(i * t, t), t), :] = o.T.astype(BF16)
        reset()

    def following(i, j):
        edge = j == i
        return jnp.where(edge, i + 1, i), jnp.where(edge, 0, j + 1)

    def pair(_, ij):
        ia, ja = ij
        ib, jb = following(ia, ja)
        ic, jc = following(ib, jb)
        icc = jnp.minimum(ic, n_q - 1)
        edge_a = ja == ia
        edge_b = jb == ib

        def body(mask_a, mask_b):
            scores(ib, jb, 1)
            softmax_pv(ja, 0, mask_a)
            if mask_a:
                finish(ia)
            scores(icc, jc, 0)
            softmax_pv(jb, 1, mask_b)
            if mask_b:
                finish(ib)

        pl.when(edge_a)(lambda: body(True, False))
        pl.when(edge_b)(lambda: body(False, True))
        pl.when(jnp.logical_not(jnp.logical_or(edge_a, edge_b)))(lambda: body(False, False))
        return ic, jc

    reset()
    scores(0, 0, 0)
    for w32_ref, w16_ref in zip(w32_refs, w16_refs):
        w16_ref[...] = w32_ref[...].astype(BF16)
    n_items = n_q * (n_q + 1) // 2
    lax.fori_loop(0, n_items // 2, pair, (jnp.int32(0), jnp.int32(0)))
    if n_items % 2:
        softmax_pv(n_q - 1, 0, True)
        finish(n_q - 1)


def _attn_call(lam_tile, subln_g, qt, k16, vt, cast_rows, cast_cols, *, t, lam_init):
    b, s, aw = k16.shape
    nh = aw // HEAD_DIM
    steps = b * nh
    key_chunk = jnp.arange(t, dtype=jnp.int32)[:, None] // CHUNK
    qry_chunk = (jnp.arange(2 * t, dtype=jnp.int32)[None, :] % t) // CHUNK
    bias = jnp.where(key_chunk <= qry_chunk, 0.0, NEG_INF).astype(F32)
    seq_spec = pl.BlockSpec((None, s, HEAD_DIM), lambda bi, hi: (bi, 0, hi))
    const = lambda bi, hi: (0, 0)
    w_specs = [pl.BlockSpec((w.shape[0] // steps, w.shape[1]), lambda bi, hi: (bi * nh + hi, 0))
               for w in cast_rows]
    w_specs += [pl.BlockSpec((w.shape[0], w.shape[1] // steps), lambda bi, hi: (0, bi * nh + hi))
                for w in cast_cols]
    weights = list(cast_rows) + list(cast_cols)
    return pl.pallas_call(
        functools.partial(_attn_kernel, t=t, n_q=s // t, lam_init=lam_init),
        grid=(b, nh),
        in_specs=[pl.BlockSpec((SUBLANES, LANES), const),
                  pl.BlockSpec((HEAD_DIM, 1), const),
                  pl.BlockSpec((t, 2 * t), const),
                  pl.BlockSpec((None, None, s // t, HEAD_DIM, t), lambda bi, hi: (bi, hi, 0, 0, 0)),
                  seq_spec,
                  pl.BlockSpec((None, None, s // t, HEAD_DIM + ONES_ROWS, t),
                               lambda bi, hi: (bi, hi, 0, 0, 0))] + w_specs,
        out_specs=[seq_spec] + w_specs,
        out_shape=[jax.ShapeDtypeStruct((b, s, aw), BF16)]
        + [jax.ShapeDtypeStruct(w.shape, BF16) for w in weights],
        scratch_shapes=[pltpu.VMEM((2, t, 2 * t), F32),
                        pltpu.VMEM((1, 2 * t), F32),
                        pltpu.VMEM((HEAD_DIM + ONES_ROWS, 2 * t), F32)],
        compiler_params=pltpu.CompilerParams(
            dimension_semantics=("arbitrary", "arbitrary"),
            vmem_limit_bytes=VMEM_LIMIT_BYTES),
        name="attn",
    )(lam_tile, subln_g.reshape(HEAD_DIM, 1), bias, qt, k16, vt, *weights)


def _dec_attn_kernel(lam_ref, g_ref, bias_ref, q_ref, kn_ref, vn_ref, ck_ref, cv_ref, o_ref,
                     *, t, nh, kc, lam_init):
    rows = 2 * nh * t
    p_len = ck_ref.shape[0]
    q = q_ref[...]
    lane = lax.broadcasted_iota(jnp.int32, (t, HEAD_DIM), 1)
    parts = []
    for c in range(2):
        keep = (lane < HALF_DIM) if c == 0 else (lane >= HALF_DIM)
        for h in range(nh):
            qh = q[:, h * HEAD_DIM:(h + 1) * HEAD_DIM]
            parts.append(jnp.where(keep, qh, jnp.zeros_like(qh)))
    qall = jnp.concatenate(parts, axis=0)

    def scores(kf):
        s = _dot_nt(qall, kf.astype(BF16))
        return s + bias_ref[:, 0:s.shape[1]]

    def update(carry, s, vf):
        m, l, acc = carry
        m_new = jnp.maximum(m, jnp.max(s, axis=-1, keepdims=True))
        alpha = jnp.exp2(m - m_new)
        p = jnp.exp2(s - m_new)
        l = alpha * l + jnp.sum(p, axis=-1, keepdims=True)
        acc = alpha * acc + _dot(p.astype(BF16), vf.astype(BF16))
        return m_new, l, acc

    carry = (jnp.full((rows, 1), NEG_INF, F32), jnp.zeros((rows, 1), F32),
             jnp.zeros((rows, HEAD_DIM), F32))
    n_chunks = p_len // kc
    s_next = scores(ck_ref[0:kc].reshape(kc * nh, HEAD_DIM))
    for j in range(n_chunks):
        s_cur = s_next
        if j + 1 < n_chunks:
            s_next = scores(ck_ref[(j + 1) * kc:(j + 2) * kc].reshape(kc * nh, HEAD_DIM))
        else:
            s_next = scores(kn_ref[...].reshape(t * nh, HEAD_DIM))
        carry = update(carry, s_cur, cv_ref[j * kc:(j + 1) * kc].reshape(kc * nh, HEAD_DIM))
    _, l, acc = update(carry, s_next, vn_ref[...].reshape(t * nh, HEAD_DIM))

    o = acc / l
    half = rows // 2
    o = o[0:half] - lam_ref[0:1, 0:1] * o[half:rows]
    ms = jnp.mean(o * o, axis=-1, keepdims=True)
    o = o * lax.rsqrt(ms + RMS_EPS) * g_ref[...] * (1.0 - lam_init)
    for h in range(nh):
        o_ref[:, h * HEAD_DIM:(h + 1) * HEAD_DIM] = o[h * t:(h + 1) * t].astype(BF16)


def _dec_attn_call(lam_tile, subln_g, q16, kn32, vn32, cache_k, cache_v, *, layer, lam_init):
    bd, t, aw = q16.shape
    _, p, nh, hd = cache_k.shape
    new_spec = pl.BlockSpec((None, t, nh, hd), lambda b: (b, 0, 0, 0))
    cache_spec = pl.BlockSpec((None, p, nh, hd), lambda b: (layer * bd + b, 0, 0, 0))
    io_spec = pl.BlockSpec((None, t, aw), lambda b: (b, 0, 0))
    kc = _pick_tile(p, DEC_KEY_CHUNK)
    ncol = max(kc, t) * nh
    row_head = (jnp.arange(2 * nh * t, dtype=jnp.int32)[:, None] // t) % nh
    col_head = jnp.arange(ncol, dtype=jnp.int32)[None, :] % nh
    bias = jnp.where(row_head == col_head, 0.0, NEG_INF).astype(F32)
    return pl.pallas_call(
        functools.partial(_dec_attn_kernel, t=t, nh=nh, kc=kc, lam_init=lam_init),
        grid=(bd,),
        in_specs=[pl.BlockSpec((SUBLANES, LANES), lambda b: (0, 0)),
                  pl.BlockSpec((1, HEAD_DIM), lambda b: (0, 0)),
                  pl.BlockSpec(bias.shape, lambda b: (0, 0)),
                  io_spec, new_spec, new_spec, cache_spec, cache_spec],
        out_specs=io_spec,
        out_shape=jax.ShapeDtypeStruct((bd, t, aw), BF16),
        compiler_params=pltpu.CompilerParams(dimension_semantics=("arbitrary",),
                                             vmem_limit_bytes=VMEM_LIMIT_BYTES),
        name="dec_attn",
    )(lam_tile, subln_g, bias, q16, kn32, vn32, cache_k, cache_v)


def _out_mlp_kernel(x_ref, oa_ref, ol_ref, wo_ref, g1_ref, b1_ref, wu_ref, wd_ref, g2_ref, b2_ref,
                    y_ref, x1_scr, *, alpha):
    j = pl.program_id(1)

    @pl.when(j == 0)
    def _():
        cat = jnp.concatenate([oa_ref[...], ol_ref[...]], axis=1)
        x1 = _layer_norm(alpha * x_ref[...] + _dot(cat, wo_ref[...]), g1_ref[...], b1_ref[...])
        x1_scr[...] = x1.astype(BF16)
        y_ref[...] = alpha * x1

    hid = jnp.square(jnp.maximum(_dot(x1_scr[...], wu_ref[...]), 0.0)).astype(BF16)
    y_ref[...] += _dot(hid, wd_ref[...])

    @pl.when(j == pl.num_programs(1) - 1)
    def _():
        y_ref[...] = _layer_norm(y_ref[...], g2_ref[...], b2_ref[...])


def _out_mlp_call(x2d, oa16, ol16, w_out16, g1, b1, w_up16, w_down16, g2, b2, *, tm, tf, alpha):
    rows, d = x2d.shape
    aw = oa16.shape[1]
    lw = ol16.shape[1]
    d_ff = w_up16.shape[1]
    row = lambda i, j: (i, 0)
    const = lambda i, j: (0, 0)
    return pl.pallas_call(
        functools.partial(_out_mlp_kernel, alpha=alpha),
        grid=(rows // tm, d_ff // tf),
        in_specs=[pl.BlockSpec((tm, d), row),
                  pl.BlockSpec((tm, aw), row),
                  pl.BlockSpec((tm, lw), row),
                  pl.BlockSpec(w_out16.shape, const, pipeline_mode=pl.Buffered(1)),
                  pl.BlockSpec((1, d), const), pl.BlockSpec((1, d), const),
                  pl.BlockSpec((d, tf), lambda i, j: (0, j)),
                  pl.BlockSpec((tf, d), lambda i, j: (j, 0)),
                  pl.BlockSpec((1, d), const), pl.BlockSpec((1, d), const)],
        out_specs=pl.BlockSpec((tm, d), row),
        out_shape=jax.ShapeDtypeStruct((rows, d), F32),
        scratch_shapes=[pltpu.VMEM((tm, d), BF16)],
        compiler_params=pltpu.CompilerParams(dimension_semantics=("arbitrary", "arbitrary"),
                                             vmem_limit_bytes=VMEM_LIMIT_BYTES),
        name="out_mlp",
    )(x2d, oa16, ol16, w_out16, g1, b1, w_up16, w_down16, g2, b2)


def _rope_tables(pos):
    inv = ROPE_THETA ** (-jnp.arange(0, HALF_DIM, 2, dtype=F32) / HALF_DIM)
    ang = pos.astype(F32)[:, None] * inv[None, :]
    cos = jnp.tile(jnp.cos(ang), (1, 4))
    sin = jnp.sin(ang)
    sin = jnp.tile(jnp.concatenate([-sin, sin], axis=-1), (1, 2))
    return cos, sin


def _gate_weights(w_a, w_i):
    per = MXU_DIM // w_a.shape[-1]

    def dense(w):
        nb, bd, _ = w.shape
        w4 = w.reshape(nb // per, per, bd, bd)
        eye = jnp.eye(per, dtype=w.dtype)
        return jnp.einsum('gpij,pq->gpiqj', w4, eye).reshape(nb // per, per * bd, per * bd)

    return jnp.concatenate([dense(w_a), dense(w_i)], axis=-1).astype(BF16)


def _pick_tile(n, pref):
    t = min(n, pref)
    while n % t:
        t //= 2
    return t


def kernel(x_prompt, x_sample, cache_k, cache_v, state_h, state_conv, w_in, lambda_q1, lambda_k1, lambda_q2, lambda_k2, subln_g, conv_w, conv_b, w_rg_a, b_rg_a, w_rg_i, b_rg_i, lru_lambda, w_out, ln1_g, ln1_b, w_up, w_down, ln2_g, ln2_b):
    depth = w_in.shape[0]
    b, s, d = x_prompt.shape
    bd, t, _ = x_sample.shape
    p = cache_k.shape[2]
    lw = conv_w.shape[-1]
    aw = (w_in.shape[-1] - 2 * lw) // 3
    nh = aw // HEAD_DIM
    alpha = (2.0 * depth) ** 0.25

    cos_p, sin_p = _rope_tables(jnp.arange(s, dtype=jnp.int32))
    cos_s, sin_s = _rope_tables(p + jnp.arange(t, dtype=jnp.int32))
    cos_s = jnp.tile(cos_s, (bd, 1))
    sin_s = jnp.tile(sin_s, (bd, 1))

    tm_a = _pick_tile(s, IN_ROWS)
    tq = _pick_tile(s, ATTN_TILE)
    tm_c = _pick_tile(b * s, MLP_ROWS)
    tf = _pick_tile(w_up.shape[-1], MLP_HIDDEN)

    yp = x_prompt.reshape(b * s, d)
    ys = x_sample.reshape(bd * t, d)
    outs = [[] for _ in range(8)]
    for l in range(depth):
        lam0 = _lambda_init(l)
        w_in16, proj_s = _cast_proj_call(ys, w_in[l])
        wg = _gate_weights(w_rg_a[l], w_rg_i[l])
        row = lambda v: v[l].reshape(1, -1)
        lam_tile = _lam_call(row(lambda_q1), row(lambda_k1), row(lambda_q2), row(lambda_k2), lam0)
        g_sub = row(subln_g)
        lru_args = (conv_w[l], row(conv_b), wg, row(b_rg_a), row(b_rg_i), row(lru_lambda))

        qt16, k32, k16, v32, vt16, ol16, hp, cp = _in_lru_call(
            yp, w_in16, cos_p, sin_p, *lru_args,
            jnp.zeros((b, 1, lw), F32), jnp.zeros((b, CONV_WIDTH - 1, lw), F32),
            tm=tm_a, seq_rows=tm_a, rows_per_batch=s, vt_tile=tq)
        oa16, w_out16, w_down16, w_up16 = _attn_call(
            lam_tile, g_sub, qt16, k16.reshape(b, s, aw), vt16, (w_out[l], w_down[l]), (w_up[l],),
            t=tq, lam_init=lam0)
        ln_args = (w_out16, row(ln1_g), row(ln1_b), w_up16, w_down16, row(ln2_g), row(ln2_b))
        yp = _out_mlp_call(yp, oa16.reshape(b * s, aw), ol16, *ln_args, tm=tm_c, tf=tf, alpha=alpha)
        outs[0].append(k32.reshape(b, s, nh, HEAD_DIM))
        outs[1].append(v32.reshape(b, s, nh, HEAD_DIM))
        outs[2].append(hp.reshape(b, lw))
        outs[3].append(cp)

        yp, proj_s = lax.optimization_barrier((yp, proj_s))
        q16, k32, v32, ol16, hn, cn = _in_lru_call(
            ys, proj_s, cos_s, sin_s, *lru_args,
            state_h[l].reshape(bd, 1, lw), state_conv[l],
            tm=bd * t, seq_rows=t, rows_per_batch=t, vt_tile=bd * t)
        oa16 = _dec_attn_call(lam_tile, g_sub, q16.reshape(bd, t, aw),
                              k32.reshape(bd, t, nh, HEAD_DIM), v32.reshape(bd, t, nh, HEAD_DIM),
                              cache_k.reshape(depth * bd, p, nh, HEAD_DIM),
                              cache_v.reshape(depth * bd, p, nh, HEAD_DIM), layer=l, lam_init=lam0)
        ys = _out_mlp_call(ys, oa16.reshape(bd * t, aw), ol16, *ln_args,
                           tm=_pick_tile(bd * t, MLP_ROWS), tf=tf, alpha=alpha)
        outs[4].append(k32.reshape(bd, t, nh, HEAD_DIM))
        outs[5].append(v32.reshape(bd, t, nh, HEAD_DIM))
        outs[6].append(hn.reshape(bd, lw))
        outs[7].append(cn)

    stk = [jnp.stack(o) for o in outs]
    return (yp.reshape(b, s, d), ys.reshape(bd, t, d), stk[0], stk[1], stk[2], stk[3],
            stk[4], stk[5], stk[6], stk[7])
```

```python
import functools
import math

import jax
import jax.numpy as jnp
from jax import lax
from jax.experimental import pallas as pl
from jax.experimental.pallas import tpu as pltpu

CHUNK = 64
HEAD_DIM = 128
HALF_DIM = HEAD_DIM // 2
LRU_BLOCKS = 16
CONV_WIDTH = 4
LRU_C = 8.0
ROPE_THETA = 10000.0
LN_EPS = 1e-5
RMS_EPS = 1e-5
NEG_INF = -1e30

SUBLANES = 8
LANES = 128
MXU_DIM = 256
ONES_ROWS = 16
VMEM_LIMIT_BYTES = 60 * 1024 * 1024

IN_ROWS = 512
ATTN_TILE = 512
DEC_KEY_CHUNK = 256
MLP_ROWS = 512
MLP_HIDDEN = 1024
LRU_SPLIT = 2

Q_SCALE = HALF_DIM ** -0.5 * math.log2(math.e)

BF16 = jnp.bfloat16
F32 = jnp.float32


def _lambda_init(layer):
    return 0.8 - 0.6 * math.exp(-0.3 * layer)


def _dot(a, b):
    return jnp.dot(a, b, preferred_element_type=F32)


def _dot_nt(a, b):
    return lax.dot_general(a, b, (((1,), (1,)), ((), ())), preferred_element_type=F32)


def _layer_norm(z, g, b):
    mu = jnp.mean(z, axis=-1, keepdims=True)
    d = z - mu
    var = jnp.mean(d * d, axis=-1, keepdims=True)
    return d * lax.rsqrt(var + LN_EPS) * g + b


def _lam_kernel(q1_ref, k1_ref, q2_ref, k2_ref, o_ref, *, lam_init):
    s1 = jnp.sum(q1_ref[...] * k1_ref[...], axis=-1, keepdims=True)
    s2 = jnp.sum(q2_ref[...] * k2_ref[...], axis=-1, keepdims=True)
    lam = jnp.exp(s1) - jnp.exp(s2) + lam_init
    o_ref[...] = jnp.broadcast_to(lam, o_ref.shape)


def _lam_call(lq1, lk1, lq2, lk2, lam_init):
    return pl.pallas_call(
        functools.partial(_lam_kernel, lam_init=lam_init),
        out_shape=jax.ShapeDtypeStruct((SUBLANES, LANES), F32),
        name="lam",
    )(lq1, lk1, lq2, lk2)


def _group_scan(a, u, sub):
    for s in (1, 2, 4):
        keep = sub >= s
        a_sh = pltpu.roll(a, s, axis=0)
        u_sh = pltpu.roll(u, s, axis=0)
        u = u + jnp.where(keep, a * u_sh, 0.0)
        a = a * jnp.where(keep, a_sh, 1.0)
    return a, u


def _in_lru_kernel(x_ref, w_ref, cos_ref, sin_ref, cw_ref, cb_ref, wg_ref, ba_ref, bi_ref,
                   lam_ref, h0_ref, c0_ref, *outs_and_scratch,
                   tm, seq_rows, tiles_per_seq, aw, lw):
    carried = seq_rows == tm
    nseq = tm // seq_rows
    gps = seq_rows // SUBLANES
    nh = aw // HEAD_DIM
    i = pl.program_id(0)
    xp_scr, xc32_scr, xc16_scr, gate_scr, g_scr, h_scr = outs_and_scratch[-6:]
    if carried:
        qt_ref, k32_ref, k16_ref, v32_ref, vt_ref, ol_ref, hl_ref, cl_ref = outs_and_scratch[:-6]
    else:
        q_ref, k32_ref, v32_ref, ol_ref, hl_ref, cl_ref = outs_and_scratch[:-6]

    if carried:
        xb16 = x_ref[...].astype(BF16)
        proj = lambda c0, width: _dot(xb16, w_ref[:, c0:c0 + width])
    else:
        proj = lambda c0, width: w_ref[:, c0:c0 + width]

    cos = cos_ref[...]
    sin = sin_ref[...]
    lane = lax.broadcasted_iota(jnp.int32, (tm, LANES), 1)
    first_half = (lane % HALF_DIM) < (HALF_DIM // 2)

    def rope(xs):
        rot = jnp.where(first_half, pltpu.roll(xs, LANES - HALF_DIM // 2, axis=1),
                        pltpu.roll(xs, HALF_DIM // 2, axis=1))
        return xs * cos + rot * sin

    pieces_per_seg = aw // MXU_DIM
    n_pieces = 3 * pieces_per_seg

    def qkv_piece(p):
        seg, c0 = p // pieces_per_seg, (p % pieces_per_seg) * MXU_DIM
        pre = proj(seg * aw + c0, MXU_DIM)
        for h in range(MXU_DIM // LANES):
            sl = slice(c0 + h * LANES, c0 + (h + 1) * LANES)
            ph = pre[:, h * LANES:(h + 1) * LANES]
            head = c0 // LANES + h
            if seg == 0:
                qs = rope(ph) * Q_SCALE
                if carried:
                    qt_ref[head] = qs.T.astype(BF16)
                else:
                    q_ref[:, sl] = qs.astype(BF16)
            elif seg == 1:
                kr = rope(ph)
                k32_ref[pl.ds(head, tm, stride=nh), :] = kr
                if carried:
                    k16_ref[:, sl] = kr.astype(BF16)
            else:
                v32_ref[pl.ds(head, tm, stride=nh), :] = ph
                if carried:
                    vt_ref[head, 0:HEAD_DIM, :] = ph.T.astype(BF16)
                    vt_ref[head, HEAD_DIM:HEAD_DIM + ONES_ROWS, :] = jnp.ones((ONES_ROWS, tm), BF16)

    xbp = proj(3 * aw, lw)
    g_scr[...] = proj(3 * aw + lw, lw)

    if carried:
        @pl.when(i % tiles_per_seq == 0)
        def _():
            xp_scr[0, 0:SUBLANES, :] = jnp.zeros((SUBLANES, lw), F32)
            h_scr[...] = jnp.zeros_like(h_scr)
    else:
        for sq in range(nseq):
            xp_scr[sq, SUBLANES - (CONV_WIDTH - 1):SUBLANES, :] = c0_ref[sq]
    for sq in range(nseq):
        xp_scr[sq, SUBLANES:, :] = xbp[sq * seq_rows:(sq + 1) * seq_rows, :]

    sub = lax.broadcasted_iota(jnp.int32, (SUBLANES, lw), 0)
    cwb = [jnp.broadcast_to(cw_ref[j:j + 1, :], (SUBLANES, lw)) for j in range(CONV_WIDTH)]
    cbb = jnp.broadcast_to(cb_ref[...], (SUBLANES, lw))
    for sq in range(nseq):
        prev = xp_scr[sq, 0:SUBLANES, :]
        prev_rolled = [pltpu.roll(prev, d, axis=0) for d in range(1, CONV_WIDTH)]
        for gg in range(gps):
            r0 = SUBLANES + gg * SUBLANES
            cur = xp_scr[sq, r0:r0 + SUBLANES, :]
            cur_rolled = [pltpu.roll(cur, d, axis=0) for d in range(1, CONV_WIDTH)]
            xc = cbb + cwb[CONV_WIDTH - 1] * cur
            for d in range(1, CONV_WIDTH):
                shifted = jnp.where(sub >= d, cur_rolled[d - 1], prev_rolled[d - 1])
                xc = xc + cwb[CONV_WIDTH - 1 - d] * shifted
            prev_rolled = cur_rolled
            g0 = (sq * gps + gg) * SUBLANES
            xc32_scr[g0:g0 + SUBLANES, :] = xc
            xc16_scr[g0:g0 + SUBLANES, :] = xc.astype(BF16)

    for sq in range(nseq):
        cl_ref[sq] = xp_scr[sq, SUBLANES + seq_rows - (CONV_WIDTH - 1):SUBLANES + seq_rows, :]
    if carried:
        xp_scr[0, 0:SUBLANES, :] = xp_scr[0, seq_rows:seq_rows + SUBLANES, :]

    for blk in range(lw // MXU_DIM):
        sl = slice(blk * MXU_DIM, (blk + 1) * MXU_DIM)
        gate_scr[:, 2 * blk * MXU_DIM:2 * (blk + 1) * MXU_DIM] = _dot(xc16_scr[:, sl], wg_ref[blk])

    wpart = lw // LRU_SPLIT
    decay, ba, bi = [], [], []
    for c in range(LRU_SPLIT):
        cols = slice(c * wpart, (c + 1) * wpart)
        nl = -lam_ref[:, cols]
        softplus = jnp.maximum(nl, 0.0) + jnp.log1p(jnp.exp(-jnp.abs(nl)))
        decay.append(jnp.broadcast_to(-LRU_C * softplus, (SUBLANES, wpart)))
        ba.append(jnp.broadcast_to(ba_ref[:, cols], (SUBLANES, wpart)))
        bi.append(jnp.broadcast_to(bi_ref[:, cols], (SUBLANES, wpart)))
    sub_part = lax.broadcasted_iota(jnp.int32, (SUBLANES, wpart), 0)

    def sigmoid(z):
        return 0.5 * jnp.tanh(0.5 * z) + 0.5

    n_groups = nseq * gps
    pieces_done = 0
    after = None
    for sq in range(nseq):
        hin = [h_scr[:, c * wpart:(c + 1) * wpart] if carried
               else h0_ref[sq, :, c * wpart:(c + 1) * wpart] for c in range(LRU_SPLIT)]
        for gg in range(gps):
            while pieces_done * n_groups < (sq * gps + gg) * n_pieces:
                qkv_piece(pieces_done)
                pieces_done += 1
            g0 = (sq * gps + gg) * SUBLANES
            rows = slice(g0, g0 + SUBLANES)
            for c in range(LRU_SPLIT):
                cols = slice(c * wpart, (c + 1) * wpart)
                blocks = range(c * wpart // MXU_DIM, (c + 1) * wpart // MXU_DIM)
                ra = jnp.concatenate(
                    [gate_scr[rows, 2 * b * MXU_DIM:(2 * b + 1) * MXU_DIM] for b in blocks], axis=1)
                ri = jnp.concatenate(
                    [gate_scr[rows, (2 * b + 1) * MXU_DIM:(2 * b + 2) * MXU_DIM] for b in blocks], axis=1)
                if after is not None:
                    ra = ra + after
                    ri = ri + after
                r = sigmoid(ra + ba[c])
                ig = sigmoid(ri + bi[c])
                log_a = decay[c] * r
                a = jnp.exp(log_a)
                y = (1.0 + a * a) * jnp.tanh(-log_a)
                root = jnp.where(y > 0.0, y * lax.rsqrt(y), 0.0)
                u = root * (ig * xc32_scr[rows, cols])
                acum, ucum = _group_scan(a, u, sub_part)
                hs = acum * hin[c] + ucum
                hin[c] = hs[SUBLANES - 1:SUBLANES, :]
                out = hs * jax.nn.gelu(g_scr[rows, cols])
                ol_ref[rows, cols] = out.astype(BF16)
                after = pltpu.bitcast(
                    lax.shift_right_logical(pltpu.bitcast(out, jnp.uint32), jnp.uint32(32)), F32)
        for c in range(LRU_SPLIT):
            hl_ref[sq, :, c * wpart:(c + 1) * wpart] = hin[c]
            if carried:
                h_scr[:, c * wpart:(c + 1) * wpart] = hin[c]
    while pieces_done < n_pieces:
        qkv_piece(pieces_done)
        pieces_done += 1


def _cast_proj_kernel(x_ref, w_ref, w16_ref, p_ref):
    w16 = w_ref[...].astype(BF16)
    w16_ref[...] = w16
    p_ref[...] = _dot(x_ref[...].astype(BF16), w16)


def _cast_proj_call(x2d, w_in):
    rows, d_model = x2d.shape
    n = w_in.shape[1]
    col = pl.BlockSpec((d_model, MXU_DIM), lambda p: (0, p))
    return pl.pallas_call(
        _cast_proj_kernel,
        grid=(n // MXU_DIM,),
        in_specs=[pl.BlockSpec((rows, d_model), lambda p: (0, 0)), col],
        out_specs=[col, pl.BlockSpec((rows, MXU_DIM), lambda p: (0, p))],
        out_shape=[jax.ShapeDtypeStruct(w_in.shape, BF16), jax.ShapeDtypeStruct((rows, n), F32)],
        compiler_params=pltpu.CompilerParams(dimension_semantics=("arbitrary",),
                                             vmem_limit_bytes=VMEM_LIMIT_BYTES),
        name="cast_proj",
    )(x2d, w_in)


def _in_lru_call(x2d, w_in16, cos, sin, conv_w, conv_b, wg, b_a, b_i, lru_lambda, h0, c0,
                 *, tm, seq_rows, rows_per_batch, vt_tile):
    rows, d_model = x2d.shape
    aw = (w_in16.shape[1] - 2 * conv_w.shape[1]) // 3
    lw = conv_w.shape[1]
    nh = aw // HEAD_DIM
    carried = seq_rows == tm
    n_tiles = rows // tm
    tiles_per_seq = rows_per_batch // tm if carried else 1
    nseq_tile = tm // seq_rows
    nseq_total = rows // rows_per_batch
    if carried:
        state_map = lambda i: (i // tiles_per_seq, 0, 0)
    else:
        state_map = lambda i: (i, 0, 0)
    const2 = lambda i: (0, 0)
    const3 = lambda i: (0, 0, 0)
    row_map = lambda i: (i, 0)
    resident = dict(pipeline_mode=pl.Buffered(1))
    in_specs = [
        pl.BlockSpec((tm, d_model), row_map),
        pl.BlockSpec(w_in16.shape, const2, **resident) if carried
        else pl.BlockSpec((tm, w_in16.shape[1]), row_map),
        pl.BlockSpec((tm, LANES), (lambda i: (i % tiles_per_seq, 0)) if carried else const2),
        pl.BlockSpec((tm, LANES), (lambda i: (i % tiles_per_seq, 0)) if carried else const2),
        pl.BlockSpec(conv_w.shape, const2),
        pl.BlockSpec(conv_b.shape, const2),
        pl.BlockSpec(wg.shape, const3, **resident),
        pl.BlockSpec(b_a.shape, const2),
        pl.BlockSpec(b_i.shape, const2),
        pl.BlockSpec(lru_lambda.shape, const2),
        pl.BlockSpec((nseq_tile, 1, lw), state_map),
        pl.BlockSpec((nseq_tile, CONV_WIDTH - 1, lw), state_map),
    ]
    rows_aw = lambda dt: (jax.ShapeDtypeStruct((rows, aw), dt), pl.BlockSpec((tm, aw), row_map))
    k32_out = v32_out = (jax.ShapeDtypeStruct((rows * nh, HEAD_DIM), F32),
                         pl.BlockSpec((tm * nh, HEAD_DIM), row_map))
    tail_outs = [
        (jax.ShapeDtypeStruct((rows, lw), BF16), pl.BlockSpec((tm, lw), row_map)),
        (jax.ShapeDtypeStruct((nseq_total, 1, lw), F32), pl.BlockSpec((nseq_tile, 1, lw), state_map)),
        (jax.ShapeDtypeStruct((nseq_total, CONV_WIDTH - 1, lw), F32),
         pl.BlockSpec((nseq_tile, CONV_WIDTH - 1, lw), state_map)),
    ]
    if carried:
        t_lead = (nseq_total, nh, rows_per_batch // vt_tile)
        t_map = lambda i: (i // tiles_per_seq, 0, ((i % tiles_per_seq) * tm) // vt_tile, 0,
                           (((i % tiles_per_seq) * tm) % vt_tile) // tm)
        transposed = lambda r: (jax.ShapeDtypeStruct(t_lead + (r, vt_tile), BF16),
                                pl.BlockSpec((None, nh, None, r, tm), t_map))
        outs = [transposed(HEAD_DIM), k32_out, rows_aw(BF16), v32_out,
                transposed(HEAD_DIM + ONES_ROWS)] + tail_outs
    else:
        outs = [rows_aw(BF16), k32_out, v32_out] + tail_outs
    out_shape = tuple(o[0] for o in outs)
    out_specs = tuple(o[1] for o in outs)
    scratch = [
        pltpu.VMEM((nseq_tile, SUBLANES + seq_rows, lw), F32),
        pltpu.VMEM((tm, lw), F32),
        pltpu.VMEM((tm, lw), BF16),
        pltpu.VMEM((tm, 2 * lw), F32),
        pltpu.VMEM((tm, lw), F32),
        pltpu.VMEM((1, lw), F32),
    ]
    kern = functools.partial(_in_lru_kernel, tm=tm, seq_rows=seq_rows, tiles_per_seq=tiles_per_seq,
                             aw=aw, lw=lw)
    return pl.pallas_call(
        kern, grid=(n_tiles,), in_specs=in_specs, out_specs=out_specs, out_shape=out_shape,
        scratch_shapes=scratch,
        compiler_params=pltpu.CompilerParams(dimension_semantics=("arbitrary",),
                                             vmem_limit_bytes=VMEM_LIMIT_BYTES),
        name="in_lru",
    )(x2d, w_in16, cos, sin, conv_w, conv_b, wg, b_a, b_i, lru_lambda, h0, c0)


def _attn_kernel(lam_ref, g_ref, bias_ref, qt_ref, k_ref, vt_ref, *rest, t, n_q, lam_init):
    n_w = (len(rest) - 4) // 2
    w32_refs, o_ref, w16_refs = rest[:n_w], rest[n_w], rest[n_w + 1:2 * n_w + 1]
    s_scr, m_scr, acc_scr = rest[2 * n_w + 1:]

    def scores(i, j, slot):
        qt = qt_ref[i]
        dim = lax.broadcasted_iota(jnp.int32, (HEAD_DIM, t), 0)
        zero = jnp.zeros_like(qt)
        qq = jnp.concatenate([jnp.where(dim < HALF_DIM, qt, zero),
                              jnp.where(dim >= HALF_DIM, qt, zero)], axis=1)
        kj = k_ref[pl.ds(pl.multiple_of(j * t, t), t), :]
        s_scr[slot] = _dot(kj, qq)

    def softmax_pv(j, slot, masked):
        s = s_scr[slot]
        if masked:
            s = s + bias_ref[...]
        m = m_scr[...]
        m_new = jnp.maximum(m, jnp.max(s, axis=0, keepdims=True))
        alpha = jnp.exp2(m - m_new)
        p = jnp.exp2(s - m_new)
        m_scr[...] = m_new
        acc_scr[...] = alpha * acc_scr[...] + _dot(vt_ref[j], p.astype(BF16))

    def reset():
        m_scr[...] = jnp.full(m_scr.shape, NEG_INF, F32)
        acc_scr[...] = jnp.zeros(acc_scr.shape, F32)

    def finish(i):
        acc = acc_scr[...]
        o = acc[0:HEAD_DIM] / acc[HEAD_DIM:HEAD_DIM + 1]
        o = o[:, 0:t] - lam_ref[0:1, 0:1] * o[:, t:2 * t]
        ms = jnp.mean(o * o, axis=0, keepdims=True)
        o = o * lax.rsqrt(ms + RMS_EPS) * g_ref[...] * (1.0 - lam_init)
        o_ref[pl.ds(pl.multiple_of(i * t, t), t), :] = o.T.astype(BF16)
        reset()

    def following(i, j):
        edge = j == i
        return jnp.where(edge, i + 1, i), jnp.where(edge, 0, j + 1)

    def pair(_, ij):
        ia, ja = ij
        ib, jb = following(ia, ja)
        ic, jc = following(ib, jb)
        icc = jnp.minimum(ic, n_q - 1)
        edge_a = ja == ia
        edge_b = jb == ib

        def body(mask_a, mask_b):
            scores(ib, jb, 1)
            softmax_pv(ja, 0, mask_a)
            if mask_a:
                finish(ia)
            scores(icc, jc, 0)
            softmax_pv(jb, 1, mask_b)
            if mask_b:
                finish(ib)

        pl.when(edge_a)(lambda: body(True, False))
        pl.when(edge_b)(lambda: body(False, True))
        pl.when(jnp.logical_not(jnp.logical_or(edge_a, edge_b)))(lambda: body(False, False))
        return ic, jc

    reset()
    scores(0, 0, 0)
    for w32_ref, w16_ref in zip(w32_refs, w16_refs):
        w16_ref[...] = w32_ref[...].astype(BF16)
    n_items = n_q * (n_q + 1) // 2
    lax.fori_loop(0, n_items // 2, pair, (jnp.int32(0), jnp.int32(0)))
    if n_items % 2:
        softmax_pv(n_q - 1, 0, True)
        finish(n_q - 1)


def _attn_call(lam_tile, subln_g, qt, k16, vt, cast_rows, cast_cols, *, t, lam_init):
    b, s, aw = k16.shape
    nh = aw // HEAD_DIM
    steps = b * nh
    key_chunk = jnp.arange(t, dtype=jnp.int32)[:, None] // CHUNK
    qry_chunk = (jnp.arange(2 * t, dtype=jnp.int32)[None, :] % t) // CHUNK
    bias = jnp.where(key_chunk <= qry_chunk, 0.0, NEG_INF).astype(F32)
    seq_spec = pl.BlockSpec((None, s, HEAD_DIM), lambda bi, hi: (bi, 0, hi))
    const = lambda bi, hi: (0, 0)
    w_specs = [pl.BlockSpec((w.shape[0] // steps, w.shape[1]), lambda bi, hi: (bi * nh + hi, 0))
               for w in cast_rows]
    w_specs += [pl.BlockSpec((w.shape[0], w.shape[1] // steps), lambda bi, hi: (0, bi * nh + hi))
                for w in cast_cols]
    weights = list(cast_rows) + list(cast_cols)
    return pl.pallas_call(
        functools.partial(_attn_kernel, t=t, n_q=s // t, lam_init=lam_init),
        grid=(b, nh),
        in_specs=[pl.BlockSpec((SUBLANES, LANES), const),
                  pl.BlockSpec((HEAD_DIM, 1), const),
                  pl.BlockSpec((t, 2 * t), const),
                  pl.BlockSpec((None, None, s // t, HEAD_DIM, t), lambda bi, hi: (bi, hi, 0, 0, 0)),
                  seq_spec,
                  pl.BlockSpec((None, None, s // t, HEAD_DIM + ONES_ROWS, t),
                               lambda bi, hi: (bi, hi, 0, 0, 0))] + w_specs,
        out_specs=[seq_spec] + w_specs,
        out_shape=[jax.ShapeDtypeStruct((b, s, aw), BF16)]
        + [jax.ShapeDtypeStruct(w.shape, BF16) for w in weights],
        scratch_shapes=[pltpu.VMEM((2, t, 2 * t), F32),
                        pltpu.VMEM((1, 2 * t), F32),
                        pltpu.VMEM((HEAD_DIM + ONES_ROWS, 2 * t), F32)],
        compiler_params=pltpu.CompilerParams(
            dimension_semantics=("arbitrary", "arbitrary"),
            vmem_limit_bytes=VMEM_LIMIT_BYTES),
        name="attn",
    )(lam_tile, subln_g.reshape(HEAD_DIM, 1), bias, qt, k16, vt, *weights)


def _dec_attn_kernel(lam_ref, g_ref, bias_ref, q_ref, kn_ref, vn_ref, ck_ref, cv_ref, o_ref,
                     *, t, nh, kc, lam_init):
    rows = 2 * nh * t
    p_len = ck_ref.shape[0]
    q = q_ref[...]
    lane = lax.broadcasted_iota(jnp.int32, (t, HEAD_DIM), 1)
    parts = []
    for c in range(2):
        keep = (lane < HALF_DIM) if c == 0 else (lane >= HALF_DIM)
        for h in range(nh):
            qh = q[:, h * HEAD_DIM:(h + 1) * HEAD_DIM]
            parts.append(jnp.where(keep, qh, jnp.zeros_like(qh)))
    qall = jnp.concatenate(parts, axis=0)

    def scores(kf):
        s = _dot_nt(qall, kf.astype(BF16))
        return s + bias_ref[:, 0:s.shape[1]]

    def update(carry, s, vf):
        m, l, acc = carry
        m_new = jnp.maximum(m, jnp.max(s, axis=-1, keepdims=True))
        alpha = jnp.exp2(m - m_new)
        p = jnp.exp2(s - m_new)
        l = alpha * l + jnp.sum(p, axis=-1, keepdims=True)
        acc = alpha * acc + _dot(p.astype(BF16), vf.astype(BF16))
        return m_new, l, acc

    carry = (jnp.full((rows, 1), NEG_INF, F32), jnp.zeros((rows, 1), F32),
             jnp.zeros((rows, HEAD_DIM), F32))
    n_chunks = p_len // kc
    s_next = scores(ck_ref[0:kc].reshape(kc * nh, HEAD_DIM))
    for j in range(n_chunks):
        s_cur = s_next
        if j + 1 < n_chunks:
            s_next = scores(ck_ref[(j + 1) * kc:(j + 2) * kc].reshape(kc * nh, HEAD_DIM))
        else:
            s_next = scores(kn_ref[...].reshape(t * nh, HEAD_DIM))
        carry = update(carry, s_cur, cv_ref[j * kc:(j + 1) * kc].reshape(kc * nh, HEAD_DIM))
    _, l, acc = update(carry, s_next, vn_ref[...].reshape(t * nh, HEAD_DIM))

    o = acc / l
    half = rows // 2
    o = o[0:half] - lam_ref[0:1, 0:1] * o[half:rows]
    ms = jnp.mean(o * o, axis=-1, keepdims=True)
    o = o * lax.rsqrt(ms + RMS_EPS) * g_ref[...] * (1.0 - lam_init)
    for h in range(nh):
        o_ref[:, h * HEAD_DIM:(h + 1) * HEAD_DIM] = o[h * t:(h + 1) * t].astype(BF16)


def _dec_attn_call(lam_tile, subln_g, q16, kn32, vn32, cache_k, cache_v, *, layer, lam_init):
    bd, t, aw = q16.shape
    _, p, nh, hd = cache_k.shape
    new_spec = pl.BlockSpec((None, t, nh, hd), lambda b: (b, 0, 0, 0))
    cache_spec = pl.BlockSpec((None, p, nh, hd), lambda b: (layer * bd + b, 0, 0, 0))
    io_spec = pl.BlockSpec((None, t, aw), lambda b: (b, 0, 0))
    kc = _pick_tile(p, DEC_KEY_CHUNK)
    ncol = max(kc, t) * nh
    row_head = (jnp.arange(2 * nh * t, dtype=jnp.int32)[:, None] // t) % nh
    col_head = jnp.arange(ncol, dtype=jnp.int32)[None, :] % nh
    bias = jnp.where(row_head == col_head, 0.0, NEG_INF).astype(F32)
    return pl.pallas_call(
        functools.partial(_dec_attn_kernel, t=t, nh=nh, kc=kc, lam_init=lam_init),
        grid=(bd,),
        in_specs=[pl.BlockSpec((SUBLANES, LANES), lambda b: (0, 0)),
                  pl.BlockSpec((1, HEAD_DIM), lambda b: (0, 0)),
                  pl.BlockSpec(bias.shape, lambda b: (0, 0)),
                  io_spec, new_spec, new_spec, cache_spec, cache_spec],
        out_specs=io_spec,
        out_shape=jax.ShapeDtypeStruct((bd, t, aw), BF16),
        compiler_params=pltpu.CompilerParams(dimension_semantics=("arbitrary",),
                                             vmem_limit_bytes=VMEM_LIMIT_BYTES),
        name="dec_attn",
    )(lam_tile, subln_g, bias, q16, kn32, vn32, cache_k, cache_v)


def _out_mlp_kernel(x_ref, oa_ref, ol_ref, wo_ref, g1_ref, b1_ref, wu_ref, wd_ref, g2_ref, b2_ref,
                    y_ref, x1_scr, *, alpha):
    j = pl.program_id(1)

    @pl.when(j == 0)
    def _():
        cat = jnp.concatenate([oa_ref[...], ol_ref[...]], axis=1)
        x1 = _layer_norm(alpha * x_ref[...] + _dot(cat, wo_ref[...]), g1_ref[...], b1_ref[...])
        x1_scr[...] = x1.astype(BF16)
        y_ref[...] = alpha * x1

    hid = jnp.square(jnp.maximum(_dot(x1_scr[...], wu_ref[...]), 0.0)).astype(BF16)
    y_ref[...] += _dot(hid, wd_ref[...])

    @pl.when(j == pl.num_programs(1) - 1)
    def _():
        y_ref[...] = _layer_norm(y_ref[...], g2_ref[...], b2_ref[...])


def _out_mlp_call(x2d, oa16, ol16, w_out16, g1, b1, w_up16, w_down16, g2, b2, *, tm, tf, alpha):
    rows, d = x2d.shape
    aw = oa16.shape[1]
    lw = ol16.shape[1]
    d_ff = w_up16.shape[1]
    row = lambda i, j: (i, 0)
    const = lambda i, j: (0, 0)
    return pl.pallas_call(
        functools.partial(_out_mlp_kernel, alpha=alpha),
        grid=(rows // tm, d_ff // tf),
        in_specs=[pl.BlockSpec((tm, d), row),
                  pl.BlockSpec((tm, aw), row),
                  pl.BlockSpec((tm, lw), row),
                  pl.BlockSpec(w_out16.shape, const, pipeline_mode=pl.Buffered(1)),
                  pl.BlockSpec((1, d), const), pl.BlockSpec((1, d), const),
                  pl.BlockSpec((d, tf), lambda i, j: (0, j)),
                  pl.BlockSpec((tf, d), lambda i, j: (j, 0)),
                  pl.BlockSpec((1, d), const), pl.BlockSpec((1, d), const)],
        out_specs=pl.BlockSpec((tm, d), row),
        out_shape=jax.ShapeDtypeStruct((rows, d), F32),
        scratch_shapes=[pltpu.VMEM((tm, d), BF16)],
        compiler_params=pltpu.CompilerParams(dimension_semantics=("arbitrary", "arbitrary"),
                                             vmem_limit_bytes=VMEM_LIMIT_BYTES),
        name="out_mlp",
    )(x2d, oa16, ol16, w_out16, g1, b1, w_up16, w_down16, g2, b2)


def _rope_tables(pos):
    inv = ROPE_THETA ** (-jnp.arange(0, HALF_DIM, 2, dtype=F32) / HALF_DIM)
    ang = pos.astype(F32)[:, None] * inv[None, :]
    cos = jnp.tile(jnp.cos(ang), (1, 4))
    sin = jnp.sin(ang)
    sin = jnp.tile(jnp.concatenate([-sin, sin], axis=-1), (1, 2))
    return cos, sin


def _gate_weights(w_a, w_i):
    per = MXU_DIM // w_a.shape[-1]

    def dense(w):
        nb, bd, _ = w.shape
        w4 = w.reshape(nb // per, per, bd, bd)
        eye = jnp.eye(per, dtype=w.dtype)
        return jnp.einsum('gpij,pq->gpiqj', w4, eye).reshape(nb // per, per * bd, per * bd)

    return jnp.concatenate([dense(w_a), dense(w_i)], axis=-1).astype(BF16)


def _pick_tile(n, pref):
    t = min(n, pref)
    while n % t:
        t //= 2
    return t


def kernel(x_prompt, x_sample, cache_k, cache_v, state_h, state_conv, w_in, lambda_q1, lambda_k1, lambda_q2, lambda_k2, subln_g, conv_w, conv_b, w_rg_a, b_rg_a, w_rg_i, b_rg_i, lru_lambda, w_out, ln1_g, ln1_b, w_up, w_down, ln2_g, ln2_b):
    depth = w_in.shape[0]
    b, s, d = x_prompt.shape
    bd, t, _ = x_sample.shape
    p = cache_k.shape[2]
    lw = conv_w.shape[-1]
    aw = (w_in.shape[-1] - 2 * lw) // 3
    nh = aw // HEAD_DIM
    alpha = (2.0 * depth) ** 0.25

    cos_p, sin_p = _rope_tables(jnp.arange(s, dtype=jnp.int32))
    cos_s, sin_s = _rope_tables(p + jnp.arange(t, dtype=jnp.int32))
    cos_s = jnp.tile(cos_s, (bd, 1))
    sin_s = jnp.tile(sin_s, (bd, 1))

    tm_a = _pick_tile(s, IN_ROWS)
    tq = _pick_tile(s, ATTN_TILE)
    tm_c = _pick_tile(b * s, MLP_ROWS)
    tf = _pick_tile(w_up.shape[-1], MLP_HIDDEN)

    yp = x_prompt.reshape(b * s, d)
    ys = x_sample.reshape(bd * t, d)
    outs = [[] for _ in range(8)]
    for l in range(depth):
        lam0 = _lambda_init(l)
        w_in16, proj_s = _cast_proj_call(ys, w_in[l])
        wg = _gate_weights(w_rg_a[l], w_rg_i[l])
        row = lambda v: v[l].reshape(1, -1)
        lam_tile = _lam_call(row(lambda_q1), row(lambda_k1), row(lambda_q2), row(lambda_k2), lam0)
        g_sub = row(subln_g)
        lru_args = (conv_w[l], row(conv_b), wg, row(b_rg_a), row(b_rg_i), row(lru_lambda))

        qt16, k32, k16, v32, vt16, ol16, hp, cp = _in_lru_call(
            yp, w_in16, cos_p, sin_p, *lru_args,
            jnp.zeros((b, 1, lw), F32), jnp.zeros((b, CONV_WIDTH - 1, lw), F32),
            tm=tm_a, seq_rows=tm_a, rows_per_batch=s, vt_tile=tq)
        oa16, w_out16, w_down16, w_up16 = _attn_call(
            lam_tile, g_sub, qt16, k16.reshape(b, s, aw), vt16, (w_out[l], w_down[l]), (w_up[l],),
            t=tq, lam_init=lam0)
        ln_args = (w_out16, row(ln1_g), row(ln1_b), w_up16, w_down16, row(ln2_g), row(ln2_b))
        yp = _out_mlp_call(yp, oa16.reshape(b * s, aw), ol16, *ln_args, tm=tm_c, tf=tf, alpha=alpha)
        outs[0].append(k32.reshape(b, s, nh, HEAD_DIM))
        outs[1].append(v32.reshape(b, s, nh, HEAD_DIM))
        outs[2].append(hp.reshape(b, lw))
        outs[3].append(cp)

        yp, proj_s = lax.optimization_barrier((yp, proj_s))
        q16, k32, v32, ol16, hn, cn = _in_lru_call(
            ys, proj_s, cos_s, sin_s, *lru_args,
            state_h[l].reshape(bd, 1, lw), state_conv[l],
            tm=bd * t, seq_rows=t, rows_per_batch=t, vt_tile=bd * t)
        oa16 = _dec_attn_call(lam_tile, g_sub, q16.reshape(bd, t, aw),
                              k32.reshape(bd, t, nh, HEAD_DIM), v32.reshape(bd, t, nh, HEAD_DIM),
                              cache_k.reshape(depth * bd, p, nh, HEAD_DIM),
                              cache_v.reshape(depth * bd, p, nh, HEAD_DIM), layer=l, lam_init=lam0)
        ys = _out_mlp_call(ys, oa16.reshape(bd * t, aw), ol16, *ln_args,
                           tm=_pick_tile(bd * t, MLP_ROWS), tf=tf, alpha=alpha)
        outs[4].append(k32.reshape(bd, t, nh, HEAD_DIM))
        outs[5].append(v32.reshape(bd, t, nh, HEAD_DIM))
        outs[6].append(hn.reshape(bd, lw))
        outs[7].append(cn)

    stk = [jnp.stack(o) for o in outs]
    return (yp.reshape(b, s, d), ys.reshape(bd, t, d), stk[0], stk[1], stk[2], stk[3],
            stk[4], stk[5], stk[6], stk[7])
```

```python
import functools
import math

import jax
import jax.numpy as jnp
from jax import lax
from jax.experimental import pallas as pl
from jax.experimental.pallas import tpu as pltpu

CHUNK = 64
HEAD_DIM = 128
HALF_DIM = HEAD_DIM // 2
LRU_BLOCKS = 16
CONV_WIDTH = 4
LRU_C = 8.0
ROPE_THETA = 10000.0
LN_EPS = 1e-5
RMS_EPS = 1e-5
NEG_INF = -1e30

SUBLANES = 8
LANES = 128
MXU_DIM = 256
ONES_ROWS = 16
VMEM_LIMIT_BYTES = 60 * 1024 * 1024

IN_ROWS = 512
ATTN_TILE = 512
DEC_KEY_CHUNK = 256
MLP_ROWS = 512
MLP_HIDDEN = 1024
LRU_SPLIT = 2

Q_SCALE = HALF_DIM ** -0.5 * math.log2(math.e)

BF16 = jnp.bfloat16
F32 = jnp.float32


def _lambda_init(layer):
    return 0.8 - 0.6 * math.exp(-0.3 * layer)


def _dot(a, b):
    return jnp.dot(a, b, preferred_element_type=F32)


def _dot_nt(a, b):
    return lax.dot_general(a, b, (((1,), (1,)), ((), ())), preferred_element_type=F32)


def _layer_norm(z, g, b):
    mu = jnp.mean(z, axis=-1, keepdims=True)
    d = z - mu
    var = jnp.mean(d * d, axis=-1, keepdims=True)
    return d * lax.rsqrt(var + LN_EPS) * g + b


def _lam_kernel(q1_ref, k1_ref, q2_ref, k2_ref, o_ref, *, lam_init):
    s1 = jnp.sum(q1_ref[...] * k1_ref[...], axis=-1, keepdims=True)
    s2 = jnp.sum(q2_ref[...] * k2_ref[...], axis=-1, keepdims=True)
    lam = jnp.exp(s1) - jnp.exp(s2) + lam_init
    o_ref[...] = jnp.broadcast_to(lam, o_ref.shape)


def _lam_call(lq1, lk1, lq2, lk2, lam_init):
    return pl.pallas_call(
        functools.partial(_lam_kernel, lam_init=lam_init),
        out_shape=jax.ShapeDtypeStruct((SUBLANES, LANES), F32),
        name="lam",
    )(lq1, lk1, lq2, lk2)


def _group_scan(a, u, sub):
    for s in (1, 2, 4):
        keep = sub >= s
        a_sh = pltpu.roll(a, s, axis=0)
        u_sh = pltpu.roll(u, s, axis=0)
        u = u + jnp.where(keep, a * u_sh, 0.0)
        a = a * jnp.where(keep, a_sh, 1.0)
    return a, u


def _in_lru_kernel(x_ref, w_ref, cos_ref, sin_ref, cw_ref, cb_ref, wg_ref, ba_ref, bi_ref,
                   lam_ref, h0_ref, c0_ref, *outs_and_scratch,
                   tm, seq_rows, tiles_per_seq, aw, lw):
    carried = seq_rows == tm
    nseq = tm // seq_rows
    gps = seq_rows // SUBLANES
    nh = aw // HEAD_DIM
    i = pl.program_id(0)
    xp_scr, xc32_scr, xc16_scr, gate_scr, g_scr, h_scr = outs_and_scratch[-6:]
    if carried:
        qt_ref, k32_ref, k16_ref, v32_ref, vt_ref, ol_ref, hl_ref, cl_ref = outs_and_scratch[:-6]
    else:
        q_ref, k32_ref, v32_ref, ol_ref, hl_ref, cl_ref = outs_and_scratch[:-6]

    if carried:
        xb16 = x_ref[...].astype(BF16)
        proj = lambda c0, width: _dot(xb16, w_ref[:, c0:c0 + width])
    else:
        proj = lambda c0, width: w_ref[:, c0:c0 + width]

    cos = cos_ref[...]
    sin = sin_ref[...]
    lane = lax.broadcasted_iota(jnp.int32, (tm, LANES), 1)
    first_half = (lane % HALF_DIM) < (HALF_DIM // 2)

    def rope(xs):
        rot = jnp.where(first_half, pltpu.roll(xs, LANES - HALF_DIM // 2, axis=1),
                        pltpu.roll(xs, HALF_DIM // 2, axis=1))
        return xs * cos + rot * sin

    pieces_per_seg = aw // MXU_DIM
    n_pieces = 3 * pieces_per_seg

    def qkv_piece(p):
        seg, c0 = p // pieces_per_seg, (p % pieces_per_seg) * MXU_DIM
        pre = proj(seg * aw + c0, MXU_DIM)
        for h in range(MXU_DIM // LANES):
            sl = slice(c0 + h * LANES, c0 + (h + 1) * LANES)
            ph = pre[:, h * LANES:(h + 1) * LANES]
            head = c0 // LANES + h
            if seg == 0:
                qs = rope(ph) * Q_SCALE
                if carried:
                    qt_ref[head] = qs.T.astype(BF16)
                else:
                    q_ref[:, sl] = qs.astype(BF16)
            elif seg == 1:
                kr = rope(ph)
                k32_ref[pl.ds(head, tm, stride=nh), :] = kr
                if carried:
                    k16_ref[:, sl] = kr.astype(BF16)
            else:
                v32_ref[pl.ds(head, tm, stride=nh), :] = ph
                if carried:
                    vt_ref[head, 0:HEAD_DIM, :] = ph.T.astype(BF16)
                    vt_ref[head, HEAD_DIM:HEAD_DIM + ONES_ROWS, :] = jnp.ones((ONES_ROWS, tm), BF16)

    xbp = proj(3 * aw, lw)

    if carried:
        @pl.when(i % tiles_per_seq == 0)
        def _():
            xp_scr[0, 0:SUBLANES, :] = jnp.zeros((SUBLANES, lw), F32)
            h_scr[...] = jnp.zeros_like(h_scr)
    else:
        for sq in range(nseq):
            xp_scr[sq, SUBLANES - (CONV_WIDTH - 1):SUBLANES, :] = c0_ref[sq]
    for sq in range(nseq):
        xp_scr[sq, SUBLANES:, :] = xbp[sq * seq_rows:(sq + 1) * seq_rows, :]

    sub = lax.broadcasted_iota(jnp.int32, (SUBLANES, lw), 0)
    cwb = [jnp.broadcast_to(cw_ref[j:j + 1, :], (SUBLANES, lw)) for j in range(CONV_WIDTH)]
    cbb = jnp.broadcast_to(cb_ref[...], (SUBLANES, lw))
    for sq in range(nseq):
        prev = xp_scr[sq, 0:SUBLANES, :]
        prev_rolled = [pltpu.roll(prev, d, axis=0) for d in range(1, CONV_WIDTH)]
        for gg in range(gps):
            r0 = SUBLANES + gg * SUBLANES
            cur = xp_scr[sq, r0:r0 + SUBLANES, :]
            cur_rolled = [pltpu.roll(cur, d, axis=0) for d in range(1, CONV_WIDTH)]
            xc = cbb + cwb[CONV_WIDTH - 1] * cur
            for d in range(1, CONV_WIDTH):
                shifted = jnp.where(sub >= d, cur_rolled[d - 1], prev_rolled[d - 1])
                xc = xc + cwb[CONV_WIDTH - 1 - d] * shifted
            prev_rolled = cur_rolled
            g0 = (sq * gps + gg) * SUBLANES
            xc32_scr[g0:g0 + SUBLANES, :] = xc
            xc16_scr[g0:g0 + SUBLANES, :] = xc.astype(BF16)

    for sq in range(nseq):
        cl_ref[sq] = xp_scr[sq, SUBLANES + seq_rows - (CONV_WIDTH - 1):SUBLANES + seq_rows, :]
    if carried:
        xp_scr[0, 0:SUBLANES, :] = xp_scr[0, seq_rows:seq_rows + SUBLANES, :]

    g_scr[...] = proj(3 * aw + lw, lw)

    for blk in range(lw // MXU_DIM):
        sl = slice(blk * MXU_DIM, (blk + 1) * MXU_DIM)
        gate_scr[:, 2 * blk * MXU_DIM:2 * (blk + 1) * MXU_DIM] = _dot(xc16_scr[:, sl], wg_ref[blk])

    wpart = lw // LRU_SPLIT
    decay, ba, bi = [], [], []
    for c in range(LRU_SPLIT):
        cols = slice(c * wpart, (c + 1) * wpart)
        nl = -lam_ref[:, cols]
        softplus = jnp.maximum(nl, 0.0) + jnp.log1p(jnp.exp(-jnp.abs(nl)))
        decay.append(jnp.broadcast_to(-LRU_C * softplus, (SUBLANES, wpart)))
        ba.append(jnp.broadcast_to(ba_ref[:, cols], (SUBLANES, wpart)))
        bi.append(jnp.broadcast_to(bi_ref[:, cols], (SUBLANES, wpart)))
    sub_part = lax.broadcasted_iota(jnp.int32, (SUBLANES, wpart), 0)

    def sigmoid(z):
        return 0.5 * jnp.tanh(0.5 * z) + 0.5

    n_groups = nseq * gps
    pieces_done = 0
    after = None
    for sq in range(nseq):
        hin = [h_scr[:, c * wpart:(c + 1) * wpart] if carried
               else h0_ref[sq, :, c * wpart:(c + 1) * wpart] for c in range(LRU_SPLIT)]
        for gg in range(gps):
            while pieces_done * n_groups < (sq * gps + gg) * n_pieces:
                qkv_piece(pieces_done)
                pieces_done += 1
            g0 = (sq * gps + gg) * SUBLANES
            rows = slice(g0, g0 + SUBLANES)
            for c in range(LRU_SPLIT):
                cols = slice(c * wpart, (c + 1) * wpart)
                blocks = range(c * wpart // MXU_DIM, (c + 1) * wpart // MXU_DIM)
                ra = jnp.concatenate(
                    [gate_scr[rows, 2 * b * MXU_DIM:(2 * b + 1) * MXU_DIM] for b in blocks], axis=1)
                ri = jnp.concatenate(
                    [gate_scr[rows, (2 * b + 1) * MXU_DIM:(2 * b + 2) * MXU_DIM] for b in blocks], axis=1)
                if after is not None:
                    ra = ra + after
                    ri = ri + after
                r = sigmoid(ra + ba[c])
                ig = sigmoid(ri + bi[c])
                log_a = decay[c] * r
                a = jnp.exp(log_a)
                y = (1.0 + a * a) * jnp.tanh(-log_a)
                root = jnp.where(y > 0.0, y * lax.rsqrt(y), 0.0)
                u = root * (ig * xc32_scr[rows, cols])
                acum, ucum = _group_scan(a, u, sub_part)
                hs = acum * hin[c] + ucum
                hin[c] = hs[SUBLANES - 1:SUBLANES, :]
                out = hs * jax.nn.gelu(g_scr[rows, cols])
                ol_ref[rows, cols] = out.astype(BF16)
                after = pltpu.bitcast(
                    lax.shift_right_logical(pltpu.bitcast(out, jnp.uint32), jnp.uint32(32)), F32)
        for c in range(LRU_SPLIT):
            hl_ref[sq, :, c * wpart:(c + 1) * wpart] = hin[c]
            if carried:
                h_scr[:, c * wpart:(c + 1) * wpart] = hin[c]
    while pieces_done < n_pieces:
        qkv_piece(pieces_done)
        pieces_done += 1


def _cast_proj_kernel(x_ref, w_ref, w16_ref, p_ref):
    w16 = w_ref[...].astype(BF16)
    w16_ref[...] = w16
    p_ref[...] = _dot(x_ref[...].astype(BF16), w16)


def _cast_proj_call(x2d, w_in):
    rows, d_model = x2d.shape
    n = w_in.shape[1]
    col = pl.BlockSpec((d_model, MXU_DIM), lambda p: (0, p))
    return pl.pallas_call(
        _cast_proj_kernel,
        grid=(n // MXU_DIM,),
        in_specs=[pl.BlockSpec((rows, d_model), lambda p: (0, 0)), col],
        out_specs=[col, pl.BlockSpec((rows, MXU_DIM), lambda p: (0, p))],
        out_shape=[jax.ShapeDtypeStruct(w_in.shape, BF16), jax.ShapeDtypeStruct((rows, n), F32)],
        compiler_params=pltpu.CompilerParams(dimension_semantics=("arbitrary",),
                                             vmem_limit_bytes=VMEM_LIMIT_BYTES),
        name="cast_proj",
    )(x2d, w_in)


def _in_lru_call(x2d, w_in16, cos, sin, conv_w, conv_b, wg, b_a, b_i, lru_lambda, h0, c0,
                 *, tm, seq_rows, rows_per_batch, vt_tile):
    rows, d_model = x2d.shape
    aw = (w_in16.shape[1] - 2 * conv_w.shape[1]) // 3
    lw = conv_w.shape[1]
    nh = aw // HEAD_DIM
    carried = seq_rows == tm
    n_tiles = rows // tm
    tiles_per_seq = rows_per_batch // tm if carried else 1
    nseq_tile = tm // seq_rows
    nseq_total = rows // rows_per_batch
    if carried:
        state_map = lambda i: (i // tiles_per_seq, 0, 0)
    else:
        state_map = lambda i: (i, 0, 0)
    const2 = lambda i: (0, 0)
    const3 = lambda i: (0, 0, 0)
    row_map = lambda i: (i, 0)
    resident = dict(pipeline_mode=pl.Buffered(1))
    in_specs = [
        pl.BlockSpec((tm, d_model), row_map),
        pl.BlockSpec(w_in16.shape, const2, **resident) if carried
        else pl.BlockSpec((tm, w_in16.shape[1]), row_map),
        pl.BlockSpec((tm, LANES), (lambda i: (i % tiles_per_seq, 0)) if carried else const2),
        pl.BlockSpec((tm, LANES), (lambda i: (i % tiles_per_seq, 0)) if carried else const2),
        pl.BlockSpec(conv_w.shape, const2),
        pl.BlockSpec(conv_b.shape, const2),
        pl.BlockSpec(wg.shape, const3, **resident),
        pl.BlockSpec(b_a.shape, const2),
        pl.BlockSpec(b_i.shape, const2),
        pl.BlockSpec(lru_lambda.shape, const2),
        pl.BlockSpec((nseq_tile, 1, lw), state_map),
        pl.BlockSpec((nseq_tile, CONV_WIDTH - 1, lw), state_map),
    ]
    rows_aw = lambda dt: (jax.ShapeDtypeStruct((rows, aw), dt), pl.BlockSpec((tm, aw), row_map))
    k32_out = v32_out = (jax.ShapeDtypeStruct((rows * nh, HEAD_DIM), F32),
                         pl.BlockSpec((tm * nh, HEAD_DIM), row_map))
    tail_outs = [
        (jax.ShapeDtypeStruct((rows, lw), BF16), pl.BlockSpec((tm, lw), row_map)),
        (jax.ShapeDtypeStruct((nseq_total, 1, lw), F32), pl.BlockSpec((nseq_tile, 1, lw), state_map)),
        (jax.ShapeDtypeStruct((nseq_total, CONV_WIDTH - 1, lw), F32),
         pl.BlockSpec((nseq_tile, CONV_WIDTH - 1, lw), state_map)),
    ]
    if carried:
        t_lead = (nseq_total, nh, rows_per_batch // vt_tile)
        t_map = lambda i: (i // tiles_per_seq, 0, ((i % tiles_per_seq) * tm) // vt_tile, 0,
                           (((i % tiles_per_seq) * tm) % vt_tile) // tm)
        transposed = lambda r: (jax.ShapeDtypeStruct(t_lead + (r, vt_tile), BF16),
                                pl.BlockSpec((None, nh, None, r, tm), t_map))
        outs = [transposed(HEAD_DIM), k32_out, rows_aw(BF16), v32_out,
                transposed(HEAD_DIM + ONES_ROWS)] + tail_outs
    else:
        outs = [rows_aw(BF16), k32_out, v32_out] + tail_outs
    out_shape = tuple(o[0] for o in outs)
    out_specs = tuple(o[1] for o in outs)
    scratch = [
        pltpu.VMEM((nseq_tile, SUBLANES + seq_rows, lw), F32),
        pltpu.VMEM((tm, lw), F32),
        pltpu.VMEM((tm, lw), BF16),
        pltpu.VMEM((tm, 2 * lw), F32),
        pltpu.VMEM((tm, lw), F32),
        pltpu.VMEM((1, lw), F32),
    ]
    kern = functools.partial(_in_lru_kernel, tm=tm, seq_rows=seq_rows, tiles_per_seq=tiles_per_seq,
                             aw=aw, lw=lw)
    return pl.pallas_call(
        kern, grid=(n_tiles,), in_specs=in_specs, out_specs=out_specs, out_shape=out_shape,
        scratch_shapes=scratch,
        compiler_params=pltpu.CompilerParams(dimension_semantics=("arbitrary",),
                                             vmem_limit_bytes=VMEM_LIMIT_BYTES),
        name="in_lru",
    )(x2d, w_in16, cos, sin, conv_w, conv_b, wg, b_a, b_i, lru_lambda, h0, c0)


def _attn_kernel(lam_ref, g_ref, bias_ref, qt_ref, k_ref, vt_ref, *rest, t, n_q, lam_init):
    n_w = (len(rest) - 4) // 2
    w32_refs, o_ref, w16_refs = rest[:n_w], rest[n_w], rest[n_w + 1:2 * n_w + 1]
    s_scr, m_scr, acc_scr = rest[2 * n_w + 1:]

    def scores(i, j, slot):
        qt = qt_ref[i]
        dim = lax.broadcasted_iota(jnp.int32, (HEAD_DIM, t), 0)
        zero = jnp.zeros_like(qt)
        qq = jnp.concatenate([jnp.where(dim < HALF_DIM, qt, zero),
                              jnp.where(dim >= HALF_DIM, qt, zero)], axis=1)
        kj = k_ref[pl.ds(pl.multiple_of(j * t, t), t), :]
        s_scr[slot] = _dot(kj, qq)

    def softmax_pv(j, slot, masked):
        s = s_scr[slot]
        if masked:
            s = s + bias_ref[...]
        m = m_scr[...]
        m_new = jnp.maximum(m, jnp.max(s, axis=0, keepdims=True))
        alpha = jnp.exp2(m - m_new)
        p = jnp.exp2(s - m_new)
        m_scr[...] = m_new
        acc_scr[...] = alpha * acc_scr[...] + _dot(vt_ref[j], p.astype(BF16))

    def reset():
        m_scr[...] = jnp.full(m_scr.shape, NEG_INF, F32)
        acc_scr[...] = jnp.zeros(acc_scr.shape, F32)

    def finish(i):
        acc = acc_scr[...]
        o = acc[0:HEAD_DIM] / acc[HEAD_DIM:HEAD_DIM + 1]
        o = o[:, 0:t] - lam_ref[0:1, 0:1] * o[:, t:2 * t]
        ms = jnp.mean(o * o, axis=0, keepdims=True)
        o = o * lax.rsqrt(ms + RMS_EPS) * g_ref[...] * (1.0 - lam_init)
        o_ref[pl.ds(pl.multiple_of(i * t, t), t), :] = o.T.astype(BF16)
        reset()

    def following(i, j):
        edge = j == i
        return jnp.where(edge, i + 1, i), jnp.where(edge, 0, j + 1)

    def pair(_, ij):
        ia, ja = ij
        ib, jb = following(ia, ja)
        ic, jc = following(ib, jb)
        icc = jnp.minimum(ic, n_q - 1)
        edge_a = ja == ia
        edge_b = jb == ib

        def body(mask_a, mask_b):
            scores(ib, jb, 1)
            softmax_pv(ja, 0, mask_a)
            if mask_a:
                finish(ia)
            scores(icc, jc, 0)
            softmax_pv(jb, 1, mask_b)
            if mask_b:
                finish(ib)

        pl.when(edge_a)(lambda: body(True, False))
        pl.when(edge_b)(lambda: body(False, True))
        pl.when(jnp.logical_not(jnp.logical_or(edge_a, edge_b)))(lambda: body(False, False))
        return ic, jc

    reset()
    scores(0, 0, 0)
    for w32_ref, w16_ref in zip(w32_refs, w16_refs):
        w16_ref[...] = w32_ref[...].astype(BF16)
    n_items = n_q * (n_q + 1) // 2
    lax.fori_loop(0, n_items // 2, pair, (jnp.int32(0), jnp.int32(0)))
    if n_items % 2:
        softmax_pv(n_q - 1, 0, True)
        finish(n_q - 1)


def _attn_call(lam_tile, subln_g, qt, k16, vt, cast_rows, cast_cols, *, t, lam_init):
    b, s, aw = k16.shape
    nh = aw // HEAD_DIM
    steps = b * nh
    key_chunk = jnp.arange(t, dtype=jnp.int32)[:, None] // CHUNK
    qry_chunk = (jnp.arange(2 * t, dtype=jnp.int32)[None, :] % t) // CHUNK
    bias = jnp.where(key_chunk <= qry_chunk, 0.0, NEG_INF).astype(F32)
    seq_spec = pl.BlockSpec((None, s, HEAD_DIM), lambda bi, hi: (bi, 0, hi))
    const = lambda bi, hi: (0, 0)
    w_specs = [pl.BlockSpec((w.shape[0] // steps, w.shape[1]), lambda bi, hi: (bi * nh + hi, 0))
               for w in cast_rows]
    w_specs += [pl.BlockSpec((w.shape[0], w.shape[1] // steps), lambda bi, hi: (0, bi * nh + hi))
                for w in cast_cols]
    weights = list(cast_rows) + list(cast_cols)
    return pl.pallas_call(
        functools.partial(_attn_kernel, t=t, n_q=s // t, lam_init=lam_init),
        grid=(b, nh),
        in_specs=[pl.BlockSpec((SUBLANES, LANES), const),
                  pl.BlockSpec((HEAD_DIM, 1), const),
                  pl.BlockSpec((t, 2 * t), const),
                  pl.BlockSpec((None, None, s // t, HEAD_DIM, t), lambda bi, hi: (bi, hi, 0, 0, 0)),
                  seq_spec,
                  pl.BlockSpec((None, None, s // t, HEAD_DIM + ONES_ROWS, t),
                               lambda bi, hi: (bi, hi, 0, 0, 0))] + w_specs,
        out_specs=[seq_spec] + w_specs,
        out_shape=[jax.ShapeDtypeStruct((b, s, aw), BF16)]
        + [jax.ShapeDtypeStruct(w.shape, BF16) for w in weights],
        scratch_shapes=[pltpu.VMEM((2, t, 2 * t), F32),
                        pltpu.VMEM((1, 2 * t), F32),
                        pltpu.VMEM((HEAD_DIM + ONES_ROWS, 2 * t), F32)],
        compiler_params=pltpu.CompilerParams(
            dimension_semantics=("arbitrary", "arbitrary"),
            vmem_limit_bytes=VMEM_LIMIT_BYTES),
        name="attn",
    )(lam_tile, subln_g.reshape(HEAD_DIM, 1), bias, qt, k16, vt, *weights)


def _dec_attn_kernel(lam_ref, g_ref, bias_ref, q_ref, kn_ref, vn_ref, ck_ref, cv_ref, o_ref,
                     *, t, nh, kc, lam_init):
    rows = 2 * nh * t
    p_len = ck_ref.shape[0]
    q = q_ref[...]
    lane = lax.broadcasted_iota(jnp.int32, (t, HEAD_DIM), 1)
    parts = []
    for c in range(2):
        keep = (lane < HALF_DIM) if c == 0 else (lane >= HALF_DIM)
        for h in range(nh):
            qh = q[:, h * HEAD_DIM:(h + 1) * HEAD_DIM]
            parts.append(jnp.where(keep, qh, jnp.zeros_like(qh)))
    qall = jnp.concatenate(parts, axis=0)

    def scores(kf):
        s = _dot_nt(qall, kf.astype(BF16))
        return s + bias_ref[:, 0:s.shape[1]]

    def update(carry, s, vf):
        m, l, acc = carry
        m_new = jnp.maximum(m, jnp.max(s, axis=-1, keepdims=True))
        alpha = jnp.exp2(m - m_new)
        p = jnp.exp2(s - m_new)
        l = alpha * l + jnp.sum(p, axis=-1, keepdims=True)
        acc = alpha * acc + _dot(p.astype(BF16), vf.astype(BF16))
        return m_new, l, acc

    carry = (jnp.full((rows, 1), NEG_INF, F32), jnp.zeros((rows, 1), F32),
             jnp.zeros((rows, HEAD_DIM), F32))
    n_chunks = p_len // kc
    s_next = scores(ck_ref[0:kc].reshape(kc * nh, HEAD_DIM))
    for j in range(n_chunks):
        s_cur = s_next
        if j + 1 < n_chunks:
            s_next = scores(ck_ref[(j + 1) * kc:(j + 2) * kc].reshape(kc * nh, HEAD_DIM))
        else:
            s_next = scores(kn_ref[...].reshape(t * nh, HEAD_DIM))
        carry = update(carry, s_cur, cv_ref[j * kc:(j + 1) * kc].reshape(kc * nh, HEAD_DIM))
    _, l, acc = update(carry, s_next, vn_ref[...].reshape(t * nh, HEAD_DIM))

    o = acc / l
    half = rows // 2
    o = o[0:half] - lam_ref[0:1, 0:1] * o[half:rows]
    ms = jnp.mean(o * o, axis=-1, keepdims=True)
    o = o * lax.rsqrt(ms + RMS_EPS) * g_ref[...] * (1.0 - lam_init)
    for h in range(nh):
        o_ref[:, h * HEAD_DIM:(h + 1) * HEAD_DIM] = o[h * t:(h + 1) * t].astype(BF16)


def _dec_attn_call(lam_tile, subln_g, q16, kn32, vn32, cache_k, cache_v, *, layer, lam_init):
    bd, t, aw = q16.shape
    _, p, nh, hd = cache_k.shape
    new_spec = pl.BlockSpec((None, t, nh, hd), lambda b: (b, 0, 0, 0))
    cache_spec = pl.BlockSpec((None, p, nh, hd), lambda b: (layer * bd + b, 0, 0, 0))
    io_spec = pl.BlockSpec((None, t, aw), lambda b: (b, 0, 0))
    kc = _pick_tile(p, DEC_KEY_CHUNK)
    ncol = max(kc, t) * nh
    row_head = (jnp.arange(2 * nh * t, dtype=jnp.int32)[:, None] // t) % nh
    col_head = jnp.arange(ncol, dtype=jnp.int32)[None, :] % nh
    bias = jnp.where(row_head == col_head, 0.0, NEG_INF).astype(F32)
    return pl.pallas_call(
        functools.partial(_dec_attn_kernel, t=t, nh=nh, kc=kc, lam_init=lam_init),
        grid=(bd,),
        in_specs=[pl.BlockSpec((SUBLANES, LANES), lambda b: (0, 0)),
                  pl.BlockSpec((1, HEAD_DIM), lambda b: (0, 0)),
                  pl.BlockSpec(bias.shape, lambda b: (0, 0)),
                  io_spec, new_spec, new_spec, cache_spec, cache_spec],
        out_specs=io_spec,
        out_shape=jax.ShapeDtypeStruct((bd, t, aw), BF16),
        compiler_params=pltpu.CompilerParams(dimension_semantics=("arbitrary",),
                                             vmem_limit_bytes=VMEM_LIMIT_BYTES),
        name="dec_attn",
    )(lam_tile, subln_g, bias, q16, kn32, vn32, cache_k, cache_v)


def _out_mlp_kernel(x_ref, oa_ref, ol_ref, wo_ref, g1_ref, b1_ref, wu_ref, wd_ref, g2_ref, b2_ref,
                    y_ref, x1_scr, *, alpha):
    j = pl.program_id(1)

    @pl.when(j == 0)
    def _():
        cat = jnp.concatenate([oa_ref[...], ol_ref[...]], axis=1)
        x1 = _layer_norm(alpha * x_ref[...] + _dot(cat, wo_ref[...]), g1_ref[...], b1_ref[...])
        x1_scr[...] = x1.astype(BF16)
        y_ref[...] = alpha * x1

    hid = jnp.square(jnp.maximum(_dot(x1_scr[...], wu_ref[...]), 0.0)).astype(BF16)
    y_ref[...] += _dot(hid, wd_ref[...])

    @pl.when(j == pl.num_programs(1) - 1)
    def _():
        y_ref[...] = _layer_norm(y_ref[...], g2_ref[...], b2_ref[...])


def _out_mlp_call(x2d, oa16, ol16, w_out16, g1, b1, w_up16, w_down16, g2, b2, *, tm, tf, alpha):
    rows, d = x2d.shape
    aw = oa16.shape[1]
    lw = ol16.shape[1]
    d_ff = w_up16.shape[1]
    row = lambda i, j: (i, 0)
    const = lambda i, j: (0, 0)
    return pl.pallas_call(
        functools.partial(_out_mlp_kernel, alpha=alpha),
        grid=(rows // tm, d_ff // tf),
        in_specs=[pl.BlockSpec((tm, d), row),
                  pl.BlockSpec((tm, aw), row),
                  pl.BlockSpec((tm, lw), row),
                  pl.BlockSpec(w_out16.shape, const, pipeline_mode=pl.Buffered(1)),
                  pl.BlockSpec((1, d), const), pl.BlockSpec((1, d), const),
                  pl.BlockSpec((d, tf), lambda i, j: (0, j)),
                  pl.BlockSpec((tf, d), lambda i, j: (j, 0)),
                  pl.BlockSpec((1, d), const), pl.BlockSpec((1, d), const)],
        out_specs=pl.BlockSpec((tm, d), row),
        out_shape=jax.ShapeDtypeStruct((rows, d), F32),
        scratch_shapes=[pltpu.VMEM((tm, d), BF16)],
        compiler_params=pltpu.CompilerParams(dimension_semantics=("arbitrary", "arbitrary"),
                                             vmem_limit_bytes=VMEM_LIMIT_BYTES),
        name="out_mlp",
    )(x2d, oa16, ol16, w_out16, g1, b1, w_up16, w_down16, g2, b2)


def _rope_tables(pos):
    inv = ROPE_THETA ** (-jnp.arange(0, HALF_DIM, 2, dtype=F32) / HALF_DIM)
    ang = pos.astype(F32)[:, None] * inv[None, :]
    cos = jnp.tile(jnp.cos(ang), (1, 4))
    sin = jnp.sin(ang)
    sin = jnp.tile(jnp.concatenate([-sin, sin], axis=-1), (1, 2))
    return cos, sin


def _gate_weights(w_a, w_i):
    per = MXU_DIM // w_a.shape[-1]

    def dense(w):
        nb, bd, _ = w.shape
        w4 = w.reshape(nb // per, per, bd, bd)
        eye = jnp.eye(per, dtype=w.dtype)
        return jnp.einsum('gpij,pq->gpiqj', w4, eye).reshape(nb // per, per * bd, per * bd)

    return jnp.concatenate([dense(w_a), dense(w_i)], axis=-1).astype(BF16)


def _pick_tile(n, pref):
    t = min(n, pref)
    while n % t:
        t //= 2
    return t


def kernel(x_prompt, x_sample, cache_k, cache_v, state_h, state_conv, w_in, lambda_q1, lambda_k1, lambda_q2, lambda_k2, subln_g, conv_w, conv_b, w_rg_a, b_rg_a, w_rg_i, b_rg_i, lru_lambda, w_out, ln1_g, ln1_b, w_up, w_down, ln2_g, ln2_b):
    depth = w_in.shape[0]
    b, s, d = x_prompt.shape
    bd, t, _ = x_sample.shape
    p = cache_k.shape[2]
    lw = conv_w.shape[-1]
    aw = (w_in.shape[-1] - 2 * lw) // 3
    nh = aw // HEAD_DIM
    alpha = (2.0 * depth) ** 0.25

    cos_p, sin_p = _rope_tables(jnp.arange(s, dtype=jnp.int32))
    cos_s, sin_s = _rope_tables(p + jnp.arange(t, dtype=jnp.int32))
    cos_s = jnp.tile(cos_s, (bd, 1))
    sin_s = jnp.tile(sin_s, (bd, 1))

    tm_a = _pick_tile(s, IN_ROWS)
    tq = _pick_tile(s, ATTN_TILE)
    tm_c = _pick_tile(b * s, MLP_ROWS)
    tf = _pick_tile(w_up.shape[-1], MLP_HIDDEN)

    yp = x_prompt.reshape(b * s, d)
    ys = x_sample.reshape(bd * t, d)
    outs = [[] for _ in range(8)]
    for l in range(depth):
        lam0 = _lambda_init(l)
        w_in16, proj_s = _cast_proj_call(ys, w_in[l])
        wg = _gate_weights(w_rg_a[l], w_rg_i[l])
        row = lambda v: v[l].reshape(1, -1)
        lam_tile = _lam_call(row(lambda_q1), row(lambda_k1), row(lambda_q2), row(lambda_k2), lam0)
        g_sub = row(subln_g)
        lru_args = (conv_w[l], row(conv_b), wg, row(b_rg_a), row(b_rg_i), row(lru_lambda))

        qt16, k32, k16, v32, vt16, ol16, hp, cp = _in_lru_call(
            yp, w_in16, cos_p, sin_p, *lru_args,
            jnp.zeros((b, 1, lw), F32), jnp.zeros((b, CONV_WIDTH - 1, lw), F32),
            tm=tm_a, seq_rows=tm_a, rows_per_batch=s, vt_tile=tq)
        oa16, w_out16, w_down16, w_up16 = _attn_call(
            lam_tile, g_sub, qt16, k16.reshape(b, s, aw), vt16, (w_out[l], w_down[l]), (w_up[l],),
            t=tq, lam_init=lam0)
        ln_args = (w_out16, row(ln1_g), row(ln1_b), w_up16, w_down16, row(ln2_g), row(ln2_b))
        yp = _out_mlp_call(yp, oa16.reshape(b * s, aw), ol16, *ln_args, tm=tm_c, tf=tf, alpha=alpha)
        outs[0].append(k32.reshape(b, s, nh, HEAD_DIM))
        outs[1].append(v32.reshape(b, s, nh, HEAD_DIM))
        outs[2].append(hp.reshape(b, lw))
        outs[3].append(cp)

        yp, proj_s = lax.optimization_barrier((yp, proj_s))
        q16, k32, v32, ol16, hn, cn = _in_lru_call(
            ys, proj_s, cos_s, sin_s, *lru_args,
            state_h[l].reshape(bd, 1, lw), state_conv[l],
            tm=bd * t, seq_rows=t, rows_per_batch=t, vt_tile=bd * t)
        oa16 = _dec_attn_call(lam_tile, g_sub, q16.reshape(bd, t, aw),
                              k32.reshape(bd, t, nh, HEAD_DIM), v32.reshape(bd, t, nh, HEAD_DIM),
                              cache_k.reshape(depth * bd, p, nh, HEAD_DIM),
                              cache_v.reshape(depth * bd, p, nh, HEAD_DIM), layer=l, lam_init=lam0)
        ys = _out_mlp_call(ys, oa16.reshape(bd * t, aw), ol16, *ln_args,
                           tm=_pick_tile(bd * t, MLP_ROWS), tf=tf, alpha=alpha)
        outs[4].append(k32.reshape(bd, t, nh, HEAD_DIM))
        outs[5].append(v32.reshape(bd, t, nh, HEAD_DIM))
        outs[6].append(hn.reshape(bd, lw))
        outs[7].append(cn)

    stk = [jnp.stack(o) for o in outs]
    return (yp.reshape(b, s, d), ys.reshape(bd, t, d), stk[0], stk[1], stk[2], stk[3],
            stk[4], stk[5], stk[6], stk[7])
```
